```python
import math, functools
import jax, jax.numpy as jnp
from jax import lax
import numpy as np

D_MODEL = 1024
BATCH = 2
SEQ = 8192
DEPTH = 2
DEC_BATCH = 32
DEC_SEQ = 8
PAST_LEN = 16384
PAGE_SIZE = 128

HEAD_DIM = 64
A_WIDTH = D_MODEL // 4
A_BLOCKS = A_WIDTH // HEAD_DIM
CONV_WIDTH = 4
RG_C = 8.0
B_HEADS = D_MODEL // 2 // HEAD_DIM
B_KV_HEADS = B_HEADS // 2
KV_GROUP = B_HEADS // B_KV_HEADS
B_WIDTH = B_HEADS * HEAD_DIM
KV_WIDTH = B_KV_HEADS * HEAD_DIM
C_HEADS = D_MODEL // 4 // HEAD_DIM
C_WIDTH = C_HEADS * HEAD_DIM
D_MIX = A_WIDTH + B_WIDTH + C_WIDTH
MOBA_BLOCK = 256
MOBA_TOPK = 3
MOBA_QCHUNK = 64
MLSTM_CHUNK = 64
PLE_DIM = 256
NORM_EPS = 1e-6
SPLIT_SIZES = (A_WIDTH, A_WIDTH, B_WIDTH, KV_WIDTH, KV_WIDTH, B_WIDTH, C_WIDTH, C_WIDTH, C_WIDTH, C_HEADS, C_HEADS, C_WIDTH, C_WIDTH)
D_IN = sum(SPLIT_SIZES)

kernel_name = 'hybrid_rglru_moba_mlstm_step'


def rmsnorm(x, g):
    xf = x.astype(jnp.float32)
    y = xf * lax.rsqrt(jnp.mean(xf * xf, axis=-1, keepdims=True) + NORM_EPS)
    return (y * g.astype(jnp.float32)).astype(x.dtype)


def alibi_slopes():
    return jnp.exp2(-8.0 * jnp.arange(1, B_HEADS + 1, dtype=jnp.float32) / B_HEADS)


def _linear_combine(c1, c2):
    a1, b1 = c1
    a2, b2 = c2
    return a1 * a2, a2 * b1 + b2


def rglru_branch(xa, conv_buf, h0, conv_w, conv_b, w_r, b_r, w_i, b_i, lam):
    bsz, t = xa.shape[:2]
    xp = jnp.concatenate([conv_buf.astype(xa.dtype), xa], axis=1)
    xc = conv_b
    for j in range(CONV_WIDTH):
        xc = xc + xp[:, j:j + t] * conv_w[j]
    new_buf = xp[:, t:]
    xf = xc.astype(jnp.float32)
    xblk = xf.reshape(bsz, t, A_BLOCKS, HEAD_DIM)
    r = jax.nn.sigmoid(jnp.einsum('btni,nij->btnj', xblk, w_r.astype(jnp.float32)).reshape(bsz, t, A_WIDTH) + b_r)
    i = jax.nn.sigmoid(jnp.einsum('btni,nij->btnj', xblk, w_i.astype(jnp.float32)).reshape(bsz, t, A_WIDTH) + b_i)
    log_a = -RG_C * r * jax.nn.softplus(-lam.astype(jnp.float32))
    a = jnp.exp(log_a)
    b = jnp.sqrt(-jnp.expm1(2.0 * log_a)) * (i * xf)
    b = b.at[:, 0].add(a[:, 0] * h0.astype(jnp.float32))
    _, h = lax.associative_scan(_linear_combine, (a, b), axis=1)
    return h, new_buf, h[:, -1]


def mlstm_branch(q, k, v, i_pre, f_pre, c0, n0, m0):
    bsz, t = q.shape[:2]
    L = math.gcd(t, MLSTM_CHUNK)
    nc = t // L

    def to_chunks(a):
        a = a.astype(jnp.float32).reshape(bsz, nc, L, *a.shape[2:])
        return jnp.moveaxis(a, (1, 3), (0, 2))

    qs, ks, vs = to_chunks(q), to_chunks(k) * HEAD_DIM ** -0.5, to_chunks(v)
    ig = to_chunks(i_pre)
    lf = to_chunks(jax.nn.log_sigmoid(f_pre.astype(jnp.float32)))
    causal = jnp.tril(jnp.ones((L, L), dtype=bool))

    def step(carry, inp):
        c, n, m = carry
        qc, kc, vc, ic, lfc = inp
        F = jnp.cumsum(lfc, axis=-1)
        d_log = jnp.where(causal, F[..., :, None] - F[..., None, :] + ic[..., None, :], -jnp.inf)
        inter = F + m[..., None]
        m_t = jnp.maximum(jnp.max(d_log, axis=-1), inter)
        w_intra = jnp.einsum('bhtd,bhsd->bhts', qc, kc) * jnp.exp(d_log - m_t[..., None])
        w_inter = jnp.exp(inter - m_t)
        num = w_inter[..., None] * jnp.einsum('bhtd,bhde->bhte', qc, c) + jnp.einsum('bhts,bhse->bhte', w_intra, vc)
        den = w_inter * jnp.einsum('bhtd,bhd->bht', qc, n) + jnp.sum(w_intra, axis=-1)
        h = num / jnp.maximum(jnp.abs(den), jnp.exp(-m_t))[..., None]
        f_tot = F[..., -1]
        w_log = f_tot[..., None] - F + ic
        m_new = jnp.maximum(f_tot + m, jnp.max(w_log, axis=-1))
        decay = jnp.exp(f_tot + m - m_new)
        w = jnp.exp(w_log - m_new[..., None])
        c_new = decay[..., None, None] * c + jnp.einsum('bhs,bhsd,bhse->bhde', w, kc, vc)
        n_new = decay[..., None] * n + jnp.einsum('bhs,bhsd->bhd', w, kc)
        return (c_new, n_new, m_new), h

    carry0 = (c0.astype(jnp.float32), n0.astype(jnp.float32), m0.astype(jnp.float32))
    (c_f, n_f, m_f), h = lax.scan(step, carry0, (qs, ks, vs, ig, lf))
    h = jnp.moveaxis(h, (0, 2), (1, 3)).reshape(bsz, t, C_HEADS, HEAD_DIM)
    return h, c_f, n_f, m_f


def select_blocks(q, kmean, n_full):
    nbc = kmean.shape[1]
    kmh = jnp.repeat(kmean, KV_GROUP, axis=2)
    sc = jnp.einsum('bhqd,bjhd->bhqj', q, kmh).astype(jnp.float32)
    sc = jnp.where(jnp.arange(nbc) < n_full, sc, -jnp.inf)
    topk = min(MOBA_TOPK, nbc)
    _, idx = lax.top_k(sc, topk)
    valid = jnp.repeat(jnp.arange(topk) < n_full, MOBA_BLOCK)
    sel_pos = (idx[..., None] * MOBA_BLOCK + jnp.arange(MOBA_BLOCK)).reshape(*idx.shape[:-1], topk * MOBA_BLOCK)
    return idx, valid, sel_pos


def moba_core(q, q_pos, k_sel, v_sel, sel_pos, sel_valid, k_own, v_own, own_pos, own_valid):
    slopes = alibi_slopes()[:, None, None]
    k_own = jnp.repeat(k_own, KV_GROUP, axis=2)
    v_own = jnp.repeat(v_own, KV_GROUP, axis=2)
    s_sel = jnp.einsum('bhqd,bhqkd->bhqk', q, k_sel).astype(jnp.float32) - slopes * (q_pos[:, None] - sel_pos).astype(jnp.float32)
    s_sel = jnp.where(sel_valid, s_sel, -jnp.inf)
    s_own = jnp.einsum('bhqd,bkhd->bhqk', q, k_own).astype(jnp.float32) - slopes * (q_pos[:, None] - own_pos[None, :]).astype(jnp.float32)
    s_own = jnp.where(own_valid, s_own, -jnp.inf)
    p = jax.nn.softmax(jnp.concatenate([s_sel, s_own], axis=-1), axis=-1).astype(q.dtype)
    ks = k_sel.shape[3]
    return jnp.einsum('bhqk,bhqkd->bhqd', p[..., :ks], v_sel) + jnp.einsum('bhqk,bkhd->bhqd', p[..., ks:], v_own)


def moba_prompt(q, k, v):
    bsz, s = q.shape[:2]
    nb = -(-s // MOBA_BLOCK)
    pad = nb * MOBA_BLOCK - s
    kb = jnp.pad(k, ((0, 0), (0, pad), (0, 0), (0, 0))).reshape(bsz, nb, MOBA_BLOCK, B_KV_HEADS, HEAD_DIM)
    vb = jnp.pad(v, ((0, 0), (0, pad), (0, 0), (0, 0))).reshape(bsz, nb, MOBA_BLOCK, B_KV_HEADS, HEAD_DIM)
    kmean = jnp.mean(kb.astype(jnp.float32), axis=2).astype(k.dtype)
    nq = s // MOBA_QCHUNK
    qch = q.reshape(bsz, nq, MOBA_QCHUNK, B_HEADS, HEAD_DIM).transpose(1, 0, 3, 2, 4)
    bidx = jnp.arange(bsz)[:, None, None, None]
    kvh = (jnp.arange(B_HEADS) // KV_GROUP)[None, :, None, None]

    def chunk(args):
        qc, ci = args
        t0 = ci * MOBA_QCHUNK
        q_pos = t0 + jnp.arange(MOBA_QCHUNK)
        own = t0 // MOBA_BLOCK
        idx, sel_valid, sel_pos = select_blocks(qc, kmean, own)
        k_sel = kb[bidx, idx, :, kvh].reshape(bsz, B_HEADS, MOBA_QCHUNK, -1, HEAD_DIM)
        v_sel = vb[bidx, idx, :, kvh].reshape(bsz, B_HEADS, MOBA_QCHUNK, -1, HEAD_DIM)
        k_own = lax.dynamic_index_in_dim(kb, own, axis=1, keepdims=False)
        v_own = lax.dynamic_index_in_dim(vb, own, axis=1, keepdims=False)
        own_pos = own * MOBA_BLOCK + jnp.arange(MOBA_BLOCK)
        own_valid = own_pos[None, :] <= q_pos[:, None]
        return moba_core(qc, q_pos, k_sel, v_sel, sel_pos, sel_valid, k_own, v_own, own_pos, own_valid)

    out = lax.map(chunk, (qch, jnp.arange(nq)))
    return out.transpose(1, 0, 3, 2, 4).reshape(bsz, s, B_WIDTH)


def moba_sample(q, k_new, v_new, cache_k, cache_v, page_table):
    bsz, t = q.shape[:2]
    n_pages = page_table.shape[1]
    past = n_pages * PAGE_SIZE
    ppb = MOBA_BLOCK // PAGE_SIZE
    n_full = past // MOBA_BLOCK
    if n_full > 0:
        page_mean = jnp.mean(cache_k[page_table[:, :n_full * ppb]].astype(jnp.float32), axis=2)
        kmean = page_mean.reshape(bsz, n_full, ppb, B_KV_HEADS, HEAD_DIM).mean(axis=2).astype(cache_k.dtype)
    else:
        kmean = jnp.zeros((bsz, 1, B_KV_HEADS, HEAD_DIM), cache_k.dtype)
    qh = q.transpose(0, 2, 1, 3)
    q_pos = past + jnp.arange(t)
    idx, sel_valid, sel_pos = select_blocks(qh, kmean, n_full)
    pages = jnp.minimum(idx[..., None] * ppb + jnp.arange(ppb), n_pages - 1)
    phys = page_table[jnp.arange(bsz)[:, None, None, None, None], pages]
    kvh = (jnp.arange(B_HEADS) // KV_GROUP)[None, :, None, None, None]
    k_sel = cache_k[phys, :, kvh].reshape(bsz, B_HEADS, t, -1, HEAD_DIM)
    v_sel = cache_v[phys, :, kvh].reshape(bsz, B_HEADS, t, -1, HEAD_DIM)
    rem_rows = (n_pages - n_full * ppb) * PAGE_SIZE
    own_pages = page_table[:, n_full * ppb:]
    k_own = jnp.concatenate([cache_k[own_pages].reshape(bsz, rem_rows, B_KV_HEADS, HEAD_DIM), k_new], axis=1)
    v_own = jnp.concatenate([cache_v[own_pages].reshape(bsz, rem_rows, B_KV_HEADS, HEAD_DIM), v_new], axis=1)
    own_pos = jnp.concatenate([n_full * MOBA_BLOCK + jnp.arange(rem_rows), q_pos])
    own_valid = jnp.concatenate([jnp.ones((t, rem_rows), dtype=bool), jnp.tril(jnp.ones((t, t), dtype=bool))], axis=1)
    out = moba_core(qh, q_pos, k_sel, v_sel, sel_pos, sel_valid, k_own, v_own, own_pos, own_valid)
    return out.transpose(0, 2, 1, 3).reshape(bsz, t, B_WIDTH)


def trunk_layer(x, pl, conv_buf, h0, c0, n0, m0, attend, lw):
    (norm_g, w_in, a_conv_w, a_conv_b, a_w_r, a_b_r, a_w_i, a_b_i, a_lambda,
     b_qnorm_g, b_knorm_g, c_b_i, c_b_f, c_onorm_g, w_out, ple_gate, ple_proj, ple_norm_g) = lw
    bsz, t = x.shape[:2]
    u = rmsnorm(x, norm_g) @ w_in
    offsets = np.cumsum(SPLIT_SIZES)[:-1].tolist()
    (xa, ga, q_b, k_b, v_b, gb, q_c, k_c, v_c, i_c, f_c, o_c, gc) = jnp.split(u, offsets, axis=-1)
    ha, conv_new, h_new = rglru_branch(xa, conv_buf, h0, a_conv_w, a_conv_b, a_w_r, a_b_r, a_w_i, a_b_i, a_lambda)
    ya = ha.astype(x.dtype) * jax.nn.silu(ga)
    q = rmsnorm(q_b.reshape(bsz, t, B_HEADS, HEAD_DIM), b_qnorm_g) * HEAD_DIM ** -0.5
    k = rmsnorm(k_b.reshape(bsz, t, B_KV_HEADS, HEAD_DIM), b_knorm_g)
    v = v_b.reshape(bsz, t, B_KV_HEADS, HEAD_DIM)
    yb = attend(q, k, v) * jax.nn.silu(gb)
    hc, c_new, n_new, m_new = mlstm_branch(
        q_c.reshape(bsz, t, C_HEADS, HEAD_DIM), k_c.reshape(bsz, t, C_HEADS, HEAD_DIM),
        v_c.reshape(bsz, t, C_HEADS, HEAD_DIM), i_c + c_b_i, f_c + c_b_f, c0, n0, m0)
    o = jax.nn.sigmoid(o_c.astype(jnp.float32)).reshape(bsz, t, C_HEADS, HEAD_DIM)
    yc = rmsnorm(hc * o, c_onorm_g.reshape(C_HEADS, HEAD_DIM)).reshape(bsz, t, C_WIDTH).astype(x.dtype) * jax.nn.silu(gc)
    x = x + jnp.concatenate([ya, yb, yc], axis=-1) @ w_out
    x = x + jax.nn.sigmoid(x @ ple_gate) * rmsnorm(pl @ ple_proj, ple_norm_g)
    return x, (k, v, conv_new, h_new, c_new, n_new, m_new)


def setup_inputs(seed: int = 0) -> dict:
    key = jax.random.key(seed)
    keys = list(jax.random.split(key, 40))

    def nrm(shape, scale=1.0):
        return jax.random.normal(keys.pop(), shape, jnp.float32) * scale

    n_pages = PAST_LEN // PAGE_SIZE
    n_used = DEC_BATCH * n_pages
    n_phys = (5 * n_used + 3) // 4
    page_table = jax.random.permutation(keys.pop(), n_phys)[:n_used].reshape(DEC_BATCH, n_pages).astype(jnp.int32)
    a0 = jax.random.uniform(keys.pop(), (DEPTH, A_WIDTH), jnp.float32, 0.9, 0.999)
    return {
        'x_prompt': nrm((BATCH, SEQ, D_MODEL)),
        'x_sample': nrm((DEC_BATCH, DEC_SEQ, D_MODEL)),
        'cache_k': nrm((DEPTH, n_phys, PAGE_SIZE, B_KV_HEADS, HEAD_DIM)),
        'cache_v': nrm((DEPTH, n_phys, PAGE_SIZE, B_KV_HEADS, HEAD_DIM)),
        'state_rglru_conv': nrm((DEPTH, DEC_BATCH, CONV_WIDTH - 1, A_WIDTH)),
        'state_rglru_h': nrm((DEPTH, DEC_BATCH, A_WIDTH), 0.5),
        'state_mlstm_C': nrm((DEPTH, DEC_BATCH, C_HEADS, HEAD_DIM, HEAD_DIM), 0.1),
        'state_mlstm_n': nrm((DEPTH, DEC_BATCH, C_HEADS, HEAD_DIM), 0.1),
        'state_mlstm_m': nrm((DEPTH, DEC_BATCH, C_HEADS)),
        'page_table': page_table,
        'p_prompt': nrm((DEPTH, BATCH, SEQ, PLE_DIM)),
        'p_sample': nrm((DEPTH, DEC_BATCH, DEC_SEQ, PLE_DIM)),
        'norm_g': 1.0 + nrm((DEPTH, D_MODEL), 0.1),
        'w_in': nrm((DEPTH, D_MODEL, D_IN), D_MODEL ** -0.5),
        'a_conv_w': nrm((DEPTH, CONV_WIDTH, A_WIDTH), CONV_WIDTH ** -0.5),
        'a_conv_b': nrm((DEPTH, A_WIDTH), 0.02),
        'a_w_r': nrm((DEPTH, A_BLOCKS, HEAD_DIM, HEAD_DIM), HEAD_DIM ** -0.5),
        'a_b_r': nrm((DEPTH, A_WIDTH), 0.1),
        'a_w_i': nrm((DEPTH, A_BLOCKS, HEAD_DIM, HEAD_DIM), HEAD_DIM ** -0.5),
        'a_b_i': nrm((DEPTH, A_WIDTH), 0.1),
        'a_lambda': jnp.log(a0) - jnp.log1p(-a0),
        'b_qnorm_g': 1.0 + nrm((DEPTH, HEAD_DIM), 0.1),
        'b_knorm_g': 1.0 + nrm((DEPTH, HEAD_DIM), 0.1),
        'c_b_i': nrm((DEPTH, C_HEADS), 0.1),
        'c_b_f': jnp.linspace(3.0, 6.0, C_HEADS, dtype=jnp.float32) + nrm((DEPTH, C_HEADS), 0.1),
        'c_onorm_g': 1.0 + nrm((DEPTH, C_WIDTH), 0.1),
        'w_out': nrm((DEPTH, D_MIX, D_MODEL), D_MIX ** -0.5),
        'ple_gate': nrm((DEPTH, D_MODEL, D_MODEL), D_MODEL ** -0.5),
        'ple_proj': nrm((DEPTH, PLE_DIM, D_MODEL), PLE_DIM ** -0.5),
        'ple_norm_g': 1.0 + nrm((DEPTH, D_MODEL), 0.1),
    }


def reference(x_prompt, x_sample, cache_k, cache_v, state_rglru_conv, state_rglru_h, state_mlstm_C, state_mlstm_n, state_mlstm_m, page_table, p_prompt, p_sample, norm_g, w_in, a_conv_w, a_conv_b, a_w_r, a_b_r, a_w_i, a_b_i, a_lambda, b_qnorm_g, b_knorm_g, c_b_i, c_b_f, c_onorm_g, w_out, ple_gate, ple_proj, ple_norm_g):
    bp = x_prompt.shape[0]
    conv0 = jnp.zeros((bp, CONV_WIDTH - 1, A_WIDTH), x_prompt.dtype)
    h0 = jnp.zeros((bp, A_WIDTH), jnp.float32)
    c0 = jnp.zeros((bp, C_HEADS, HEAD_DIM, HEAD_DIM), jnp.float32)
    n0 = jnp.zeros((bp, C_HEADS, HEAD_DIM), jnp.float32)
    m0 = jnp.zeros((bp, C_HEADS), jnp.float32)
    xp, xs = x_prompt, x_sample
    sp_all, ss_all = [], []
    for l in range(DEPTH):
        lw = (norm_g[l], w_in[l], a_conv_w[l], a_conv_b[l], a_w_r[l], a_b_r[l], a_w_i[l], a_b_i[l], a_lambda[l],
              b_qnorm_g[l], b_knorm_g[l], c_b_i[l], c_b_f[l], c_onorm_g[l], w_out[l], ple_gate[l], ple_proj[l], ple_norm_g[l])
        xp, sp = trunk_layer(xp, p_prompt[l], conv0, h0, c0, n0, m0, moba_prompt, lw)
        attend_s = functools.partial(moba_sample, cache_k=cache_k[l], cache_v=cache_v[l], page_table=page_table)
        xs, ss = trunk_layer(xs, p_sample[l], state_rglru_conv[l], state_rglru_h[l], state_mlstm_C[l],
                             state_mlstm_n[l], state_mlstm_m[l], attend_s, lw)
        sp_all.append(sp)
        ss_all.append(ss)

    def stacked(states, j):
        return jnp.stack([s_[j] for s_ in states])

    return (xp, xs,
            stacked(sp_all, 0), stacked(sp_all, 1), stacked(ss_all, 0), stacked(ss_all, 1),
            stacked(sp_all, 2), stacked(ss_all, 2), stacked(sp_all, 3), stacked(ss_all, 3),
            stacked(sp_all, 4), stacked(ss_all, 4), stacked(sp_all, 5), stacked(ss_all, 5),
            stacked(sp_all, 6), stacked(ss_all, 6))
```

```python
import functools

import numpy as np
import jax
import jax.numpy as jnp
from jax import lax
from jax.experimental import pallas as pl
from jax.experimental.pallas import tpu as pltpu

F32 = jnp.float32
BF16 = jnp.bfloat16
HIGHEST = lax.Precision.HIGHEST

LANES = 128
SUBLANES = 8

HEAD_DIM = 64
D_MODEL = 1024
A_WIDTH = 256
CONV_WIDTH = 4
RG_C = 8.0
B_HEADS = 8
B_KV_HEADS = 4
C_HEADS = 4
MOBA_BLOCK = 256
MOBA_TOPK = 3
PAGE_SIZE = 128
MLSTM_CHUNK = 64
PLE_DIM = 256
NORM_EPS = 1e-6
NB_MAX = 32
NEG = -1e30

U_XA, U_GA = 0, 256
U_QB, U_GB, U_KB, U_VB = 512, 1024, 1536, 1792
U_CQ, U_CK, U_CV, U_CO, U_CG, U_CIF = 2048, 2304, 2560, 2816, 3072, 3328
D_U = 3456
_REF_SPLITS = (256, 256, 512, 256, 256, 512, 256, 256, 256, 4, 4, 256, 256)
_REF_OFFS = np.concatenate([[0], np.cumsum(_REF_SPLITS)]).tolist()

VMEM_LIMIT = 48 * 1024 * 1024


def _cparams(sem):
    return pltpu.CompilerParams(dimension_semantics=sem, vmem_limit_bytes=VMEM_LIMIT)


def _lane_iota(shape):
    return lax.broadcasted_iota(jnp.int32, shape, len(shape) - 1)


def _row_iota(shape):
    return lax.broadcasted_iota(jnp.int32, shape, len(shape) - 2)


def _sigmoid(x):
    return 1.0 / (1.0 + jnp.exp(-x))


def _silu(x):
    return x * _sigmoid(x)


def _softplus(x):
    return jnp.maximum(x, 0.0) + jnp.log1p(jnp.exp(-jnp.abs(x)))


def _dot_nt(a, b, **kw):
    return lax.dot_general(a, b, (((1,), (1,)), ((), ())), preferred_element_type=F32, **kw)


def _inproj_kernel(x_ref, g_ref, w_ref, qg_ref, kg_ref, bd_ref, u_ref, k_ref, v_ref):
    x = x_ref[...]
    xn = x * lax.rsqrt(jnp.mean(x * x, axis=-1, keepdims=True) + NORM_EPS) * g_ref[...]
    u = jnp.dot(xn.astype(BF16), w_ref[...], preferred_element_type=F32)
    u_ref[...] = u
    bd = bd_ref[...]
    q = u[:, U_QB:U_QB + 512]
    qss = jnp.dot((q * q).astype(BF16), bd, preferred_element_type=F32) * (1.0 / HEAD_DIM)
    qn = q * lax.rsqrt(qss + NORM_EPS) * qg_ref[...] * (HEAD_DIM ** -0.5)
    u_ref[:, U_QB:U_QB + 512] = qn
    k = u[:, U_KB:U_KB + 256]
    kss = jnp.dot((k * k).astype(BF16), bd[:256, :256], preferred_element_type=F32) * (1.0 / HEAD_DIM)
    kn = k * lax.rsqrt(kss + NORM_EPS) * kg_ref[...]
    u_ref[:, U_KB:U_KB + 256] = kn
    k_ref[...] = kn
    v_ref[...] = u[:, U_VB:U_VB + 256]


def _inproj(x2d, norm_g, w_u, qg, kg, bd, tm):
    m = x2d.shape[0]
    const = lambda i: (0, 0)
    return pl.pallas_call(
        _inproj_kernel,
        grid=(m // tm,),
        in_specs=[
            pl.BlockSpec((tm, D_MODEL), lambda i: (i, 0)),
            pl.BlockSpec((1, D_MODEL), const),
            pl.BlockSpec((D_MODEL, D_U), const),
            pl.BlockSpec((1, 512), const),
            pl.BlockSpec((1, 256), const),
            pl.BlockSpec((512, 512), const),
        ],
        out_specs=[
            pl.BlockSpec((tm, D_U), lambda i: (i, 0)),
            pl.BlockSpec((tm, 256), lambda i: (i, 0)),
            pl.BlockSpec((tm, 256), lambda i: (i, 0)),
        ],
        out_shape=[
            jax.ShapeDtypeStruct((m, D_U), F32),
            jax.ShapeDtypeStruct((m, 256), F32),
            jax.ShapeDtypeStruct((m, 256), F32),
        ],
        compiler_params=_cparams(("parallel",)),
        name="inproj",
    )(x2d, norm_g, w_u, qg, kg, bd)


def _rglru_kernel(xa_ref, ga_ref, conv0_ref, h0_ref, cw_ref, cb_ref, wri_ref, bri_ref, lam_ref,
                  ya_ref, convn_ref, hl_ref, xbuf, abuf, bbuf, hbuf, hcar, *, tt):
    ti = pl.program_id(1)
    nt = pl.num_programs(1)

    @pl.when(ti == 0)
    def _():
        xbuf[5:8, :] = conv0_ref[0]
        hcar[...] = jnp.broadcast_to(h0_ref[0], (SUBLANES, A_WIDTH))

    xa = xa_ref[...]
    xbuf[8:8 + tt, :] = xa
    cw = cw_ref[...]
    xc = jnp.broadcast_to(cb_ref[...], (tt, A_WIDTH))
    for j in range(CONV_WIDTH):
        xc = xc + xbuf[5 + j:5 + j + tt, :] * cw[j:j + 1, :]
    xbuf[5:8, :] = xa[tt - 3:tt, :]

    gates = jnp.dot(xc.astype(BF16), wri_ref[...], preferred_element_type=F32) + bri_ref[...]
    r = _sigmoid(gates[:, :A_WIDTH])
    ig = _sigmoid(gates[:, A_WIDTH:])
    log_a = (-RG_C) * r * _softplus(-lam_ref[...])
    a = jnp.exp(log_a)
    abuf[...] = a
    bbuf[...] = jnp.sqrt(-jnp.tanh(log_a) * (a * a + 1.0)) * (ig * xc)

    rows = _row_iota((SUBLANES, A_WIDTH))

    def group(gi, carry):
        r0 = pl.multiple_of(gi * SUBLANES, SUBLANES)
        a = abuf[pl.ds(r0, SUBLANES), :]
        b = bbuf[pl.ds(r0, SUBLANES), :]
        for d in (1, 2, 4):
            a_sh = pltpu.roll(a, d, 0)
            b_sh = pltpu.roll(b, d, 0)
            keep = rows >= d
            b = jnp.where(keep, a * b_sh + b, b)
            a = jnp.where(keep, a * a_sh, a)
        h = a * hcar[...] + b
        hbuf[pl.ds(r0, SUBLANES), :] = h
        hcar[...] = jnp.broadcast_to(h[SUBLANES - 1:SUBLANES, :], (SUBLANES, A_WIDTH))
        return carry

    lax.fori_loop(0, tt // SUBLANES, group, 0)

    ya_ref[...] = hbuf[...] * _silu(ga_ref[...])

    @pl.when(ti == nt - 1)
    def _():
        convn_ref[0] = xa[tt - 3:tt, :]
        hl_ref[0] = hcar[0:1, :]


def _rglru(u, conv0, h0, cw, cb, wri, bri, lam, bsz, t, tt):
    nt = t // tt
    kern = functools.partial(_rglru_kernel, tt=tt)
    const = lambda b, i: (0, 0)
    return pl.pallas_call(
        kern,
        grid=(bsz, nt),
        in_specs=[
            pl.BlockSpec((tt, A_WIDTH), lambda b, i: (b * nt + i, U_XA // A_WIDTH)),
            pl.BlockSpec((tt, A_WIDTH), lambda b, i: (b * nt + i, U_GA // A_WIDTH)),
            pl.BlockSpec((1, 3, A_WIDTH), lambda b, i: (b, 0, 0)),
            pl.BlockSpec((1, 1, A_WIDTH), lambda b, i: (b, 0, 0)),
            pl.BlockSpec((CONV_WIDTH, A_WIDTH), const),
            pl.BlockSpec((1, A_WIDTH), const),
            pl.BlockSpec((A_WIDTH, 2 * A_WIDTH), const),
            pl.BlockSpec((1, 2 * A_WIDTH), const),
            pl.BlockSpec((1, A_WIDTH), const),
        ],
        out_specs=[
            pl.BlockSpec((tt, A_WIDTH), lambda b, i: (b * nt + i, 0)),
            pl.BlockSpec((1, 3, A_WIDTH), lambda b, i: (b, 0, 0)),
            pl.BlockSpec((1, 1, A_WIDTH), lambda b, i: (b, 0, 0)),
        ],
        out_shape=[
            jax.ShapeDtypeStruct((bsz * t, A_WIDTH), F32),
            jax.ShapeDtypeStruct((bsz, 3, A_WIDTH), F32),
            jax.ShapeDtypeStruct((bsz, 1, A_WIDTH), F32),
        ],
        scratch_shapes=[
            pltpu.VMEM((tt + 8, A_WIDTH), F32),
            pltpu.VMEM((tt, A_WIDTH), F32),
            pltpu.VMEM((tt, A_WIDTH), F32),
            pltpu.VMEM((tt, A_WIDTH), F32),
            pltpu.VMEM((SUBLANES, A_WIDTH), F32),
        ],
        compiler_params=_cparams(("parallel", "arbitrary")),
        name="rglru",
    )(u, u, conv0, h0, cw, cb, wri, bri, lam)


def _kmean_kernel(k_ref, o_ref, *, nblk):
    k = k_ref[...].reshape(nblk, MOBA_BLOCK, 256)
    o_ref[...] = jnp.sum(k, axis=1) * (1.0 / MOBA_BLOCK)


def _kmean(k2d, nblk):
    m = k2d.shape[0]
    rows = nblk * MOBA_BLOCK
    return pl.pallas_call(
        functools.partial(_kmean_kernel, nblk=nblk),
        grid=(m // rows,),
        in_specs=[pl.BlockSpec((rows, 256), lambda i: (i, 0))],
        out_specs=pl.BlockSpec((nblk, 256), lambda i: (i, 0)),
        out_shape=jax.ShapeDtypeStruct((m // MOBA_BLOCK, 256), F32),
        compiler_params=_cparams(("parallel",)),
        name="kmean",
    )(k2d)


def _select_kernel(q_ref, k_ref, v_ref, w_ref, qa_ref, ka_ref, va_ref, *, ts):
    t0 = pl.program_id(1) * ts
    q = q_ref[...]
    sc = jnp.dot(q, w_ref[0], preferred_element_type=F32, precision=HIGHEST)
    shp = (ts, 256)
    pos = t0 + _row_iota(shp)
    j = _lane_iota(shp) & (NB_MAX - 1)
    n_full = pos >> 8
    valid = j < n_full
    scm = jnp.where(valid, sc, -jnp.inf)
    rank = jnp.zeros(shp, jnp.int32)
    for s in range(1, NB_MAX):
        a = pltpu.roll(scm, 256 - s, 1)
        b = pltpu.roll(scm, NB_MAX - s, 1)
        wrapped = j >= NB_MAX - s
        p = jnp.where(wrapped, b, a)
        rank = rank + jnp.where(wrapped, jnp.where(p >= scm, 1, 0), jnp.where(p > scm, 1, 0))
    allowed = (valid & (rank < MOBA_TOPK)) | (j == n_full)
    bias = jnp.where(allowed, 0.0, NEG)

    shp1 = (ts, LANES)
    lane = _lane_iota(shp1)
    pos1 = t0 + _row_iota(shp1)
    pos_hi = (pos1 >> 6).astype(F32)
    pos_lo = (pos1 & 63).astype(F32)
    blk_onehot = jnp.where((lane - 64) == (pos1 >> 8), 1.0, 0.0)
    kfeat = jnp.where(lane == 96, pos_hi, jnp.where(lane == 97, pos_lo,
                      jnp.where((lane == 98) | (lane == 99), 1.0, 0.0)))
    kextra = jnp.where(lane < 96, blk_onehot, kfeat)
    vextra = jnp.where(lane == 64, 1.0, 0.0)
    k = k_ref[...]
    v = v_ref[...]
    for g in range(B_KV_HEADS):
        kt = k[:, (g // 2) * LANES:(g // 2 + 1) * LANES]
        vt = v[:, (g // 2) * LANES:(g // 2 + 1) * LANES]
        if g % 2:
            kt = pltpu.roll(kt, 64, 1)
            vt = pltpu.roll(vt, 64, 1)
        ka_ref[0, g] = jnp.where(lane < 64, kt, kextra).astype(BF16)
        va_ref[0, g] = jnp.where(lane < 64, vt, vextra).astype(BF16)
        qt = q[:, g * LANES:(g + 1) * LANES]
        for h2 in range(2):
            h = 2 * g + h2
            base = qt if h2 == 0 else pltpu.roll(qt, 64, 1)
            bt = bias[:, (h // 4) * LANES:(h // 4 + 1) * LANES]
            sh = (64 - (h % 4) * NB_MAX) % LANES
            if sh:
                bt = pltpu.roll(bt, sh, 1)
            slope = 2.0 ** (-(h + 1))
            qfeat = jnp.where(lane == 96, slope * 64.0, jnp.where(lane == 97, slope,
                              jnp.where(lane == 98, (-slope * 64.0) * pos_hi,
                                        jnp.where(lane == 99, (-slope) * pos_lo, 0.0))))
            aug = jnp.where(lane < 64, base, jnp.where(lane < 96, bt, qfeat))
            qa_ref[0, h] = aug.astype(BF16)


def _select(u, k2d, v2d, wsel, bsz, s, ts):
    nt = s // ts
    return pl.pallas_call(
        functools.partial(_select_kernel, ts=ts),
        grid=(bsz, nt),
        in_specs=[
            pl.BlockSpec((ts, 512), lambda b, i: (b * nt + i, U_QB // 512)),
            pl.BlockSpec((ts, 256), lambda b, i: (b * nt + i, 0)),
            pl.BlockSpec((ts, 256), lambda b, i: (b * nt + i, 0)),
            pl.BlockSpec((1, 512, 256), lambda b, i: (b, 0, 0)),
        ],
        out_specs=[
            pl.BlockSpec((1, B_HEADS, ts, LANES), lambda b, i: (b, 0, i, 0)),
            pl.BlockSpec((1, B_KV_HEADS, ts, LANES), lambda b, i: (b, 0, i, 0)),
            pl.BlockSpec((1, B_KV_HEADS, ts, LANES), lambda b, i: (b, 0, i, 0)),
        ],
        out_shape=[
            jax.ShapeDtypeStruct((bsz, B_HEADS, s, LANES), BF16),
            jax.ShapeDtypeStruct((bsz, B_KV_HEADS, s, LANES), BF16),
            jax.ShapeDtypeStruct((bsz, B_KV_HEADS, s, LANES), BF16),
        ],
        compiler_params=_cparams(("parallel", "parallel")),
        name="moba_select",
    )(u, k2d, v2d, wsel)


def _attn_kernel(qa_ref, ka_ref, va_ref, o_ref, m_sc, acc_sc, *, tq):
    qi = pl.program_id(2)
    rows = 2 * tq
    q = qa_ref[0].reshape(rows, LANES)

    k0 = ka_ref[0, 0, pl.ds(pl.multiple_of(qi * tq, tq), tq), :]
    v0 = va_ref[0, 0, pl.ds(pl.multiple_of(qi * tq, tq), tq), :]
    s = _dot_nt(q, k0)
    shp = (rows, tq)
    qpos = _row_iota(shp) & (tq - 1)
    s = jnp.where(_lane_iota(shp) <= qpos, s, NEG)
    m0 = jnp.max(s, axis=-1, keepdims=True)
    p = jnp.exp(s - m0)
    m_sc[...] = jnp.broadcast_to(m0, (rows, LANES))
    acc_sc[...] = jnp.dot(p.astype(BF16), v0, preferred_element_type=F32)

    def body(jb, carry):
        r0 = pl.multiple_of(jb * tq, tq)
        kj = ka_ref[0, 0, pl.ds(r0, tq), :]
        vj = va_ref[0, 0, pl.ds(r0, tq), :]
        sj = _dot_nt(q, kj)
        m_prev = m_sc[...]
        m_new = jnp.maximum(m_prev, jnp.max(sj, axis=-1, keepdims=True))
        alpha = jnp.exp(m_prev - m_new)
        pj = jnp.exp(sj - m_new[:, 0:1])
        acc_sc[...] = alpha * acc_sc[...] + jnp.dot(pj.astype(BF16), vj, preferred_element_type=F32)
        m_sc[...] = m_new
        return carry

    lax.fori_loop(0, qi, body, 0)

    acc = acc_sc[...]
    res = acc / acc[:, HEAD_DIM:HEAD_DIM + 1]
    lane = _lane_iota((tq, LANES))
    o_ref[...] = jnp.where(lane < 64, res[:tq], pltpu.roll(res[tq:], 64, 1))


def _attn(qa, ka, va, bsz, s):
    tq = MOBA_BLOCK
    nq = s // tq
    return pl.pallas_call(
        functools.partial(_attn_kernel, tq=tq),
        grid=(bsz, B_KV_HEADS, nq),
        in_specs=[
            pl.BlockSpec((1, 2, tq, LANES), lambda b, g, i: (b, g, i, 0)),
            pl.BlockSpec((1, 1, s, LANES), lambda b, g, i: (b, g, 0, 0)),
            pl.BlockSpec((1, 1, s, LANES), lambda b, g, i: (b, g, 0, 0)),
        ],
        out_specs=pl.BlockSpec((tq, LANES), lambda b, g, i: (b * nq + i, g)),
        out_shape=jax.ShapeDtypeStruct((bsz * s, 512), F32),
        scratch_shapes=[
            pltpu.VMEM((2 * tq, LANES), F32),
            pltpu.VMEM((2 * tq, LANES), F32),
        ],
        compiler_params=_cparams(("parallel", "parallel", "arbitrary")),
        name="moba_attn",
    )(qa, ka, va)


PAGES_PER_STEP = 16


def _page_copies(pt_ref, src_hbm, buf, sem, b, c, slot):
    return [pltpu.make_async_copy(src_hbm.at[pt_ref[b, c * PAGES_PER_STEP + p]], buf.at[slot, p], sem.at[slot])
            for p in range(PAGES_PER_STEP)]


def _skmean_kernel(pt_ref, ck_hbm, o_ref, kbuf, sem, *, nch):
    b = pl.program_id(0)
    blocks = PAGES_PER_STEP * PAGE_SIZE // MOBA_BLOCK
    for cp in _page_copies(pt_ref, ck_hbm, kbuf, sem, b, 0, 0):
        cp.start()
    for c in range(nch):
        slot = c % 2
        if c + 1 < nch:
            for cp in _page_copies(pt_ref, ck_hbm, kbuf, sem, b, c + 1, 1 - slot):
                cp.start()
        for cp in _page_copies(pt_ref, ck_hbm, kbuf, sem, b, c, slot):
            cp.wait()
        kc = kbuf[slot].reshape(blocks, MOBA_BLOCK, 256)
        o_ref[0, c * blocks:(c + 1) * blocks, :] = jnp.sum(kc, axis=1) * (1.0 / MOBA_BLOCK)


def _skmean(pt, ck, bsz, n_pages):
    nch = n_pages // PAGES_PER_STEP
    nblk = n_pages * PAGE_SIZE // MOBA_BLOCK
    grid_spec = pltpu.PrefetchScalarGridSpec(
        num_scalar_prefetch=1,
        grid=(bsz,),
        in_specs=[pl.BlockSpec(memory_space=pl.ANY)],
        out_specs=pl.BlockSpec((1, nblk, 256), lambda b, pt: (b, 0, 0)),
        scratch_shapes=[
            pltpu.VMEM((2, PAGES_PER_STEP, PAGE_SIZE, 256), F32),
            pltpu.SemaphoreType.DMA((2,)),
        ],
    )
    return pl.pallas_call(
        functools.partial(_skmean_kernel, nch=nch),
        grid_spec=grid_spec,
        out_shape=jax.ShapeDtypeStruct((bsz, nblk, 256), F32),
        compiler_params=_cparams(("arbitrary",)),
        name="sample_kmean",
    )(pt, ck)


def _sattn_kernel(pt_ref, q_ref, km_ref, kn_ref, vn_ref, slope_ref, ck_hbm, cv_hbm, o_ref,
                  kbuf, vbuf, ksem, vsem, *, nch, nblk, tnew, past):
    b = pl.program_id(0)
    rows = 64
    qf = q_ref[0]
    qb = qf.astype(BF16)
    slope = slope_ref[...]
    shp = (rows, LANES)
    lane = _lane_iota(shp)
    tok = _row_iota(shp) & (tnew - 1)

    def start(c, slot):
        for cp in _page_copies(pt_ref, ck_hbm, kbuf, ksem, b, c, slot):
            cp.start()
        for cp in _page_copies(pt_ref, cv_hbm, vbuf, vsem, b, c, slot):
            cp.start()

    def wait(c, slot):
        for cp in _page_copies(pt_ref, ck_hbm, kbuf, ksem, b, c, slot):
            cp.wait()
        for cp in _page_copies(pt_ref, cv_hbm, vbuf, vsem, b, c, slot):
            cp.wait()

    start(0, 0)

    km = km_ref[0]
    km2 = jnp.concatenate([km, km], axis=0)
    sc = _dot_nt(qf, km2, precision=HIGHEST)
    j = lane & (nblk - 1)
    rank = jnp.zeros(shp, jnp.int32)
    for s in range(1, nblk):
        a = pltpu.roll(sc, 2 * nblk - s, 1)
        wrapped = j >= nblk - s
        rank = rank + jnp.where(wrapped, jnp.where(a >= sc, 1, 0), jnp.where(a > sc, 1, 0))
    bias = jnp.where(rank < MOBA_TOPK, 0.0, NEG)

    s_own = _dot_nt(qb, kn_ref[0].astype(BF16))
    s_own = s_own - slope * (tok - lane).astype(F32)
    s_own = jnp.where((lane <= tok) & (lane < tnew), s_own, NEG)
    m = jnp.max(s_own, axis=-1, keepdims=True)
    p = jnp.exp(s_own - m)
    l = jnp.sum(p, axis=-1, keepdims=True)
    acc = jnp.dot(p.astype(BF16), vn_ref[0].astype(BF16), preferred_element_type=F32)

    keys = PAGES_PER_STEP * PAGE_SIZE
    blocks = keys // MOBA_BLOCK
    kshape = (rows, keys)
    klane = _lane_iota(kshape)
    qpos = past + (_row_iota(kshape) & (tnew - 1))
    slope_k = jnp.broadcast_to(slope[:, 0:1], kshape)
    for c in range(nch):
        slot = c % 2
        if c + 1 < nch:
            start(c + 1, 1 - slot)
        wait(c, slot)
        kc = kbuf[slot].reshape(keys, 256).astype(BF16)
        vc = vbuf[slot].reshape(keys, 256).astype(BF16)
        s_c = _dot_nt(qb, kc)
        s_c = s_c - slope_k * (qpos - (c * keys + klane)).astype(F32)
        sel = jnp.concatenate(
            [jnp.broadcast_to(bias[:, c * blocks + jj:c * blocks + jj + 1], (rows, MOBA_BLOCK))
             for jj in range(blocks)], axis=1)
        s_c = s_c + sel
        m_new = jnp.maximum(m, jnp.max(s_c, axis=-1, keepdims=True))
        alpha = jnp.exp(m - m_new)
        p = jnp.exp(s_c - m_new)
        l = alpha * l + jnp.sum(p, axis=-1, keepdims=True)
        acc = alpha * acc + jnp.dot(p.astype(BF16), vc, preferred_element_type=F32)
        m = m_new

    res = acc / l
    oshape = (rows, 256)
    own = (_lane_iota(oshape) >> 6) == (_row_iota(oshape) >> 4)
    res = jnp.where(own, res, 0.0)
    folded = res + pltpu.roll(res, 64, 1) + pltpu.roll(res, 128, 1) + pltpu.roll(res, 192, 1)
    o_ref[0] = folded[:, :HEAD_DIM]


def _sattn(pt, qbd, kmean, knew, vnew, slopes, ck, cv, bsz, n_pages, tnew):
    nch = n_pages // PAGES_PER_STEP
    nblk = n_pages * PAGE_SIZE // MOBA_BLOCK
    past = n_pages * PAGE_SIZE
    grid_spec = pltpu.PrefetchScalarGridSpec(
        num_scalar_prefetch=1,
        grid=(bsz,),
        in_specs=[
            pl.BlockSpec((1, 64, 256), lambda b, pt: (b, 0, 0)),
            pl.BlockSpec((1, nblk, 256), lambda b, pt: (b, 0, 0)),
            pl.BlockSpec((1, LANES, 256), lambda b, pt: (b, 0, 0)),
            pl.BlockSpec((1, LANES, 256), lambda b, pt: (b, 0, 0)),
            pl.BlockSpec((64, LANES), lambda b, pt: (0, 0)),
            pl.BlockSpec(memory_space=pl.ANY),
            pl.BlockSpec(memory_space=pl.ANY),
        ],
        out_specs=pl.BlockSpec((1, 64, HEAD_DIM), lambda b, pt: (b, 0, 0)),
        scratch_shapes=[
            pltpu.VMEM((2, PAGES_PER_STEP, PAGE_SIZE, 256), F32),
            pltpu.VMEM((2, PAGES_PER_STEP, PAGE_SIZE, 256), F32),
            pltpu.SemaphoreType.DMA((2,)),
            pltpu.SemaphoreType.DMA((2,)),
        ],
    )
    return pl.pallas_call(
        functools.partial(_sattn_kernel, nch=nch, nblk=nblk, tnew=tnew, past=past),
        grid_spec=grid_spec,
        out_shape=jax.ShapeDtypeStruct((bsz, 64, HEAD_DIM), F32),
        compiler_params=_cparams(("arbitrary",)),
        name="sample_attn",
    )(pt, qbd, kmean, knew, vnew, slopes, ck, cv)


def _mlstm_kernel(q_ref, k_ref, v_ref, if_ref, o_ref, g_ref, bias_ref, og_ref, c0_ref, m0_ref,
                  y_ref, cout_ref, mout_ref, c_sc, m_sc, *, tc, L):
    ti = pl.program_id(1)
    nt = pl.num_programs(1)

    @pl.when(ti == 0)
    def _():
        c_sc[...] = c0_ref[0]
        m_sc[...] = m0_ref[0]

    gates = if_ref[...] + bias_ref[...]
    lf = jnp.minimum(gates, 0.0) - jnp.log1p(jnp.exp(-jnp.abs(gates)))
    tri = _lane_iota((L, L)) <= _row_iota((L, L))
    tri_f = jnp.where(tri, 1.0, 0.0)
    lane = _lane_iota((L, LANES))
    og = og_ref[...]
    for c in range(tc // L):
        r0 = c * L
        fcum = jnp.dot(tri_f, lf[r0:r0 + L], preferred_element_type=F32, precision=HIGHEST)
        z = gates[r0:r0 + L] - pltpu.roll(fcum, LANES - 4, 1)
        qt = q_ref[r0:r0 + L, :]
        kt = k_ref[r0:r0 + L, :] * (HEAD_DIM ** -0.5)
        vt = v_ref[r0:r0 + L, :]
        ot = _sigmoid(o_ref[r0:r0 + L, :])
        for pair in range(C_HEADS // 2):
            ytile = jnp.zeros((L, LANES), F32)
            for h2 in range(2):
                h = 2 * pair + h2
                sl = slice(pair * LANES, (pair + 1) * LANES)
                qh, kh, vh, oh = qt[:, sl], kt[:, sl], vt[:, sl], ot[:, sl]
                if h2:
                    qh, kh, vh, oh = [pltpu.roll(a, 64, 1) for a in (qh, kh, vh, oh)]
                qh = jnp.where(lane < 64, qh, 0.0).astype(BF16)
                kh = jnp.where(lane < 64, kh, 0.0).astype(BF16)
                vaug = jnp.where(lane < 64, vh, jnp.where(lane == 64, 1.0, 0.0))
                m_prev = m_sc[h:h + 1, 0:1]
                f_col = fcum[:, 4 + h:5 + h]
                z_col = z[:, h:h + 1]
                pick = jnp.where(lane == h, 1.0, 0.0)
                z_row = _dot_nt(pick, z, precision=HIGHEST)
                d_log = jnp.where(tri, f_col + z_row, -jnp.inf)
                inter = f_col + m_prev
                m_t = jnp.maximum(jnp.max(d_log, axis=-1, keepdims=True), inter)
                w_intra = _dot_nt(qh, kh) * jnp.exp(d_log - m_t)
                w_inter = jnp.exp(inter - m_t)
                c_prev = c_sc[h]
                num = (w_inter * jnp.dot(qh, c_prev.astype(BF16), preferred_element_type=F32)
                       + jnp.dot(w_intra.astype(BF16), vaug.astype(BF16), preferred_element_type=F32))
                den = num[:, HEAD_DIM:HEAD_DIM + 1]
                hh = num / jnp.maximum(jnp.abs(den), jnp.exp(-m_t))
                f_tot = f_col[L - 1:L, :]
                w_log = f_tot + z_col
                m_new = jnp.maximum(f_tot + m_prev, jnp.max(w_log, axis=0, keepdims=True))
                decay = jnp.exp(f_tot + m_prev - m_new)
                w = jnp.exp(w_log - m_new)
                upd = lax.dot_general(kh, (w * vaug).astype(BF16), (((0,), (0,)), ((), ())),
                                      preferred_element_type=F32)
                c_sc[h] = decay * c_prev + upd
                m_sc[h:h + 1, :] = jnp.broadcast_to(m_new, (1, LANES))
                ho = jnp.where(lane < 64, hh * oh, 0.0)
                ms = jnp.sum(ho * ho, axis=-1, keepdims=True) * (1.0 / HEAD_DIM)
                yh = ho * lax.rsqrt(ms + NORM_EPS)
                ytile = ytile + (pltpu.roll(yh, 64, 1) if h2 else yh)
            sl = slice(pair * LANES, (pair + 1) * LANES)
            y_ref[r0:r0 + L, sl] = ytile * og[:, sl] * _silu(g_ref[r0:r0 + L, sl])

    @pl.when(ti == nt - 1)
    def _():
        cout_ref[0] = c_sc[...]
        mout_ref[0] = m_sc[...]


def _mlstm(u, gate_bias, onorm_g, c0aug, m0, bsz, t, tc, L):
    nt = t // tc
    const = lambda b, i: (0, 0)
    ucol = lambda off, w: (lambda b, i: (b * nt + i, off // w))
    return pl.pallas_call(
        functools.partial(_mlstm_kernel, tc=tc, L=L),
        grid=(bsz, nt),
        in_specs=[
            pl.BlockSpec((tc, 256), ucol(U_CQ, 256)),
            pl.BlockSpec((tc, 256), ucol(U_CK, 256)),
            pl.BlockSpec((tc, 256), ucol(U_CV, 256)),
            pl.BlockSpec((tc, LANES), ucol(U_CIF, LANES)),
            pl.BlockSpec((tc, 256), ucol(U_CO, 256)),
            pl.BlockSpec((tc, 256), ucol(U_CG, 256)),
            pl.BlockSpec((1, LANES), const),
            pl.BlockSpec((1, 256), const),
            pl.BlockSpec((1, C_HEADS, LANES, LANES), lambda b, i: (b, 0, 0, 0)),
            pl.BlockSpec((1, SUBLANES, LANES), lambda b, i: (b, 0, 0)),
        ],
        out_specs=[
            pl.BlockSpec((tc, 256), lambda b, i: (b * nt + i, 0)),
            pl.BlockSpec((1, C_HEADS, LANES, LANES), lambda b, i: (b, 0, 0, 0)),
            pl.BlockSpec((1, SUBLANES, LANES), lambda b, i: (b, 0, 0)),
        ],
        out_shape=[
            jax.ShapeDtypeStruct((bsz * t, 256), F32),
            jax.ShapeDtypeStruct((bsz, C_HEADS, LANES, LANES), F32),
            jax.ShapeDtypeStruct((bsz, SUBLANES, LANES), F32),
        ],
        scratch_shapes=[
            pltpu.VMEM((C_HEADS, LANES, LANES), F32),
            pltpu.VMEM((SUBLANES, LANES), F32),
        ],
        compiler_params=_cparams(("parallel", "arbitrary")),
        name="mlstm",
    )(u, u, u, u, u, u, gate_bias, onorm_g, c0aug, m0)


def _outproj_kernel(x_ref, ya_ref, yb_ref, gb_ref, yc_ref, p_ref, wa_ref, wb_ref, wc_ref,
                    pg_ref, pp_ref, png_ref, o_ref):
    yb = yb_ref[...] * _silu(gb_ref[...])
    x = (x_ref[...]
         + jnp.dot(ya_ref[...].astype(BF16), wa_ref[...], preferred_element_type=F32)
         + jnp.dot(yb.astype(BF16), wb_ref[...], preferred_element_type=F32)
         + jnp.dot(yc_ref[...].astype(BF16), wc_ref[...], preferred_element_type=F32))
    gate = _sigmoid(jnp.dot(x.astype(BF16), pg_ref[...], preferred_element_type=F32))
    e = jnp.dot(p_ref[...].astype(BF16), pp_ref[...], preferred_element_type=F32)
    e = e * lax.rsqrt(jnp.mean(e * e, axis=-1, keepdims=True) + NORM_EPS) * png_ref[...]
    o_ref[...] = x + gate * e


def _outproj(x2d, ya, yb, u, yc, p2d, wa, wb, wc, pg, pp, png, tm):
    m = x2d.shape[0]
    row = lambda i: (i, 0)
    const = lambda i: (0, 0)
    return pl.pallas_call(
        _outproj_kernel,
        grid=(m // tm,),
        in_specs=[
            pl.BlockSpec((tm, D_MODEL), row),
            pl.BlockSpec((tm, 256), row),
            pl.BlockSpec((tm, 512), row),
            pl.BlockSpec((tm, 512), lambda i: (i, U_GB // 512)),
            pl.BlockSpec((tm, 256), row),
            pl.BlockSpec((tm, PLE_DIM), row),
            pl.BlockSpec((256, D_MODEL), const),
            pl.BlockSpec((512, D_MODEL), const),
            pl.BlockSpec((256, D_MODEL), const),
            pl.BlockSpec((D_MODEL, D_MODEL), const),
            pl.BlockSpec((PLE_DIM, D_MODEL), const),
            pl.BlockSpec((1, D_MODEL), const),
        ],
        out_specs=pl.BlockSpec((tm, D_MODEL), row),
        out_shape=jax.ShapeDtypeStruct((m, D_MODEL), F32),
        compiler_params=_cparams(("parallel",)),
        name="outproj",
    )(x2d, ya, yb, u, yc, p2d, wa, wb, wc, pg, pp, png)


def _block_diag(w):
    n, a, b = w.shape
    eye = jnp.eye(n, dtype=w.dtype)
    return jnp.einsum("nab,nm->namb", w, eye).reshape(n * a, n * b)


def _prep_layer(l, w):
    (norm_g, w_in, a_conv_w, a_conv_b, a_w_r, a_b_r, a_w_i, a_b_i, a_lambda, b_qnorm_g, b_knorm_g,
     c_b_i, c_b_f, c_onorm_g, w_out, ple_gate, ple_proj, ple_norm_g) = [a[l] for a in w]
    col = lambda i: w_in[:, _REF_OFFS[i]:_REF_OFFS[i + 1]]
    w_u = jnp.concatenate(
        [col(0), col(1), col(2), col(5), col(3), col(4), col(6), col(7), col(8), col(11), col(12),
         col(9), col(10), jnp.zeros((D_MODEL, LANES - 2 * C_HEADS), F32)], axis=1).astype(BF16)
    gate_bias = jnp.concatenate([c_b_i, c_b_f, jnp.zeros((LANES - 2 * C_HEADS,), F32)])[None, :]
    return dict(
        norm_g=norm_g[None, :], w_u=w_u,
        qg=jnp.tile(b_qnorm_g, B_HEADS)[None, :], kg=jnp.tile(b_knorm_g, B_KV_HEADS)[None, :],
        bd=_block_diag(jnp.ones((B_HEADS, HEAD_DIM, HEAD_DIM), BF16)),
        cw=a_conv_w, cb=a_conv_b[None, :],
        wri=jnp.concatenate([_block_diag(a_w_r), _block_diag(a_w_i)], axis=1).astype(BF16),
        bri=jnp.concatenate([a_b_r, a_b_i])[None, :], lam=a_lambda[None, :],
        gate_bias=gate_bias, onorm_g=c_onorm_g[None, :],
        wa=w_out[:256].astype(BF16), wb=w_out[256:768].astype(BF16), wc=w_out[768:].astype(BF16),
        pg=ple_gate.astype(BF16), pp=ple_proj.astype(BF16), png=ple_norm_g[None, :],
    )


def _pick_tile(n, cap):
    t = min(n, cap)
    while n % t:
        t //= 2
    return t


def _layer(x2d, p2d, bsz, t, conv0, h0, c0, n0, m0, lw, attend):
    m = bsz * t
    tm = _pick_tile(m, 256)
    u, k2d, v2d = _inproj(x2d, lw["norm_g"], lw["w_u"], lw["qg"], lw["kg"], lw["bd"], tm)
    ya, conv_new, h_last = _rglru(u, conv0, h0[:, None, :], lw["cw"], lw["cb"], lw["wri"], lw["bri"],
                                  lw["lam"], bsz, t, _pick_tile(t, 512))
    yb = attend(u, k2d, v2d)
    c0aug = jnp.zeros((bsz, C_HEADS, LANES, LANES), F32)
    c0aug = c0aug.at[:, :, :HEAD_DIM, :HEAD_DIM].set(c0).at[:, :, :HEAD_DIM, HEAD_DIM].set(n0)
    m0b = jnp.zeros((bsz, SUBLANES, LANES), F32).at[:, :C_HEADS, :].set(
        jnp.broadcast_to(m0[:, :, None], (bsz, C_HEADS, LANES)))
    L = int(np.gcd(t, MLSTM_CHUNK))
    yc, caug, mout = _mlstm(u, lw["gate_bias"], lw["onorm_g"], c0aug, m0b, bsz, t, _pick_tile(t, 512), L)
    xo = _outproj(x2d, ya, yb, u, yc, p2d, lw["wa"], lw["wb"], lw["wc"], lw["pg"], lw["pp"], lw["png"], tm)
    state = (k2d.reshape(bsz, t, B_KV_HEADS, HEAD_DIM), v2d.reshape(bsz, t, B_KV_HEADS, HEAD_DIM),
             conv_new, h_last[:, 0, :],
             caug[:, :, :HEAD_DIM, :HEAD_DIM], caug[:, :, :HEAD_DIM, HEAD_DIM], mout[:, :C_HEADS, 0])
    return xo, state


def _attend_prompt(u, k2d, v2d, bsz, s):
    nb = s // MOBA_BLOCK
    assert s % MOBA_BLOCK == 0 and nb <= NB_MAX
    kmean = _kmean(k2d, _pick_tile(bsz * nb, 8)).reshape(bsz, nb, B_KV_HEADS, HEAD_DIM)
    kmh = jnp.repeat(kmean, B_HEADS // B_KV_HEADS, axis=2).transpose(0, 2, 3, 1)
    kmh = jnp.pad(kmh, ((0, 0), (0, 0), (0, 0), (0, NB_MAX - nb)))
    wsel = jnp.einsum("bhdj,hg->bhdgj", kmh, jnp.eye(B_HEADS, dtype=F32)).reshape(bsz, 512, B_HEADS * NB_MAX)
    qa, ka, va = _select(u, k2d, v2d, wsel, bsz, s, _pick_tile(s, 512))
    return _attn(qa, ka, va, bsz, s)


def _attend_sample(u, k2d, v2d, bsz, t, pt, ck, cv, slopes):
    n_pages = pt.shape[1]
    assert t == 8 and n_pages % PAGES_PER_STEP == 0
    nblk = n_pages * PAGE_SIZE // MOBA_BLOCK
    assert nblk == 64
    kmean = _skmean(pt, ck, bsz, n_pages)
    q = u[:, U_QB:U_QB + 512].reshape(bsz, t, B_KV_HEADS, 2, HEAD_DIM).transpose(0, 2, 3, 1, 4)
    qbd = jnp.einsum("bghtd,gk->bghtkd", q, jnp.eye(B_KV_HEADS, dtype=F32)).reshape(bsz, 64, 256)
    pad = ((0, 0), (0, LANES - t), (0, 0))
    knew = jnp.pad(k2d.reshape(bsz, t, 256), pad)
    vnew = jnp.pad(v2d.reshape(bsz, t, 256), pad)
    o = _sattn(pt, qbd, kmean, knew, vnew, slopes, ck, cv, bsz, n_pages, t)
    return o.reshape(bsz, B_HEADS, t, HEAD_DIM).transpose(0, 2, 1, 3).reshape(bsz * t, 512)


def kernel(x_prompt, x_sample, cache_k, cache_v, state_rglru_conv, state_rglru_h, state_mlstm_C, state_mlstm_n, state_mlstm_m, page_table, p_prompt, p_sample, norm_g, w_in, a_conv_w, a_conv_b, a_w_r, a_b_r, a_w_i, a_b_i, a_lambda, b_qnorm_g, b_knorm_g, c_b_i, c_b_f, c_onorm_g, w_out, ple_gate, ple_proj, ple_norm_g):
    weights = (norm_g, w_in, a_conv_w, a_conv_b, a_w_r, a_b_r, a_w_i, a_b_i, a_lambda, b_qnorm_g, b_knorm_g,
               c_b_i, c_b_f, c_onorm_g, w_out, ple_gate, ple_proj, ple_norm_g)
    depth = w_in.shape[0]
    bp, sp = x_prompt.shape[:2]
    bs, ts = x_sample.shape[:2]
    n_phys = cache_k.shape[1]
    ck = cache_k.reshape(depth * n_phys, PAGE_SIZE, 256)
    cv = cache_v.reshape(depth * n_phys, PAGE_SIZE, 256)
    slopes = jnp.asarray(np.repeat(2.0 ** -(np.arange(1, B_HEADS + 1)), ts)[:, None]
                         * np.ones((1, LANES)), F32)

    xp = x_prompt.reshape(bp * sp, D_MODEL)
    xs = x_sample.reshape(bs * ts, D_MODEL)
    zeros = lambda *s: jnp.zeros(s, F32)
    sp_all, ss_all = [], []
    for l in range(depth):
        lw = _prep_layer(l, weights)
        xp, st_p = _layer(xp, p_prompt[l].reshape(bp * sp, PLE_DIM), bp, sp,
                          zeros(bp, CONV_WIDTH - 1, A_WIDTH), zeros(bp, A_WIDTH),
                          zeros(bp, C_HEADS, HEAD_DIM, HEAD_DIM), zeros(bp, C_HEADS, HEAD_DIM), zeros(bp, C_HEADS),
                          lw, functools.partial(_attend_prompt, bsz=bp, s=sp))
        attend_s = functools.partial(_attend_sample, bsz=bs, t=ts, pt=page_table + l * n_phys,
                                     ck=ck, cv=cv, slopes=slopes)
        xs, st_s = _layer(xs, p_sample[l].reshape(bs * ts, PLE_DIM), bs, ts,
                          state_rglru_conv[l], state_rglru_h[l], state_mlstm_C[l], state_mlstm_n[l],
                          state_mlstm_m[l], lw, attend_s)
        sp_all.append(st_p)
        ss_all.append(st_s)

    stk = lambda states, j: jnp.stack([s_[j] for s_ in states])
    return (xp.reshape(bp, sp, D_MODEL), xs.reshape(bs, ts, D_MODEL),
            stk(sp_all, 0), stk(sp_all, 1), stk(ss_all, 0), stk(ss_all, 1),
            stk(sp_all, 2), stk(ss_all, 2), stk(sp_all, 3), stk(ss_all, 3),
            stk(sp_all, 4), stk(ss_all, 4), stk(sp_all, 5), stk(ss_all, 5),
            stk(sp_all, 6), stk(ss_all, 6))
```

```python
import functools

import numpy as np
import jax
import jax.numpy as jnp
from jax import lax
from jax.experimental import pallas as pl
from jax.experimental.pallas import tpu as pltpu

F32 = jnp.float32
BF16 = jnp.bfloat16
HIGHEST = lax.Precision.HIGHEST

LANES = 128
SUBLANES = 8

HEAD_DIM = 64
D_MODEL = 1024
A_WIDTH = 256
CONV_WIDTH = 4
RG_C = 8.0
B_HEADS = 8
B_KV_HEADS = 4
C_HEADS = 4
MOBA_BLOCK = 256
MOBA_TOPK = 3
PAGE_SIZE = 128
MLSTM_CHUNK = 64
PLE_DIM = 256
NORM_EPS = 1e-6
NB_MAX = 32
NEG = -1e30

U_XA, U_GA = 0, 256
U_QB, U_GB, U_KB, U_VB = 512, 1024, 1536, 1792
U_CQ, U_CK, U_CV, U_CO, U_CG, U_CIF = 2048, 2304, 2560, 2816, 3072, 3328
D_U = 3456
_REF_SPLITS = (256, 256, 512, 256, 256, 512, 256, 256, 256, 4, 4, 256, 256)
_REF_OFFS = np.concatenate([[0], np.cumsum(_REF_SPLITS)]).tolist()

VMEM_LIMIT = 48 * 1024 * 1024


def _cparams(sem):
    return pltpu.CompilerParams(dimension_semantics=sem, vmem_limit_bytes=VMEM_LIMIT)


def _lane_iota(shape):
    return lax.broadcasted_iota(jnp.int32, shape, len(shape) - 1)


def _row_iota(shape):
    return lax.broadcasted_iota(jnp.int32, shape, len(shape) - 2)


def _sigmoid(x):
    return 1.0 / (1.0 + jnp.exp(-x))


def _silu(x):
    return x * _sigmoid(x)


def _softplus(x):
    return jnp.maximum(x, 0.0) + jnp.log1p(jnp.exp(-jnp.abs(x)))


def _dot_nt(a, b, **kw):
    return lax.dot_general(a, b, (((1,), (1,)), ((), ())), preferred_element_type=F32, **kw)


def _inproj_kernel(x_ref, g_ref, w_ref, qg_ref, kg_ref, bd_ref, u_ref, k_ref, v_ref):
    x = x_ref[...]
    xn = x * lax.rsqrt(jnp.mean(x * x, axis=-1, keepdims=True) + NORM_EPS) * g_ref[...]
    u = jnp.dot(xn.astype(BF16), w_ref[...], preferred_element_type=F32)
    u_ref[...] = u
    bd = bd_ref[...]
    q = u[:, U_QB:U_QB + 512]
    qss = jnp.dot((q * q).astype(BF16), bd, preferred_element_type=F32) * (1.0 / HEAD_DIM)
    qn = q * lax.rsqrt(qss + NORM_EPS) * qg_ref[...] * (HEAD_DIM ** -0.5)
    u_ref[:, U_QB:U_QB + 512] = qn
    k = u[:, U_KB:U_KB + 256]
    kss = jnp.dot((k * k).astype(BF16), bd[:256, :256], preferred_element_type=F32) * (1.0 / HEAD_DIM)
    kn = k * lax.rsqrt(kss + NORM_EPS) * kg_ref[...]
    u_ref[:, U_KB:U_KB + 256] = kn
    k_ref[...] = kn
    v_ref[...] = u[:, U_VB:U_VB + 256]


def _inproj(x2d, norm_g, w_u, qg, kg, bd, tm):
    m = x2d.shape[0]
    const = lambda i: (0, 0)
    return pl.pallas_call(
        _inproj_kernel,
        grid=(m // tm,),
        in_specs=[
            pl.BlockSpec((tm, D_MODEL), lambda i: (i, 0)),
            pl.BlockSpec((1, D_MODEL), const),
            pl.BlockSpec((D_MODEL, D_U), const),
            pl.BlockSpec((1, 512), const),
            pl.BlockSpec((1, 256), const),
            pl.BlockSpec((512, 512), const),
        ],
        out_specs=[
            pl.BlockSpec((tm, D_U), lambda i: (i, 0)),
            pl.BlockSpec((tm, 256), lambda i: (i, 0)),
            pl.BlockSpec((tm, 256), lambda i: (i, 0)),
        ],
        out_shape=[
            jax.ShapeDtypeStruct((m, D_U), F32),
            jax.ShapeDtypeStruct((m, 256), F32),
            jax.ShapeDtypeStruct((m, 256), F32),
        ],
        compiler_params=_cparams(("parallel",)),
        name="inproj",
    )(x2d, norm_g, w_u, qg, kg, bd)


def _rglru_kernel(xa_ref, ga_ref, conv0_ref, h0_ref, cw_ref, cb_ref, wri_ref, bri_ref, lam_ref,
                  ya_ref, convn_ref, hl_ref, xbuf, abuf, bbuf, hbuf, hcar, *, tt):
    ti = pl.program_id(1)
    nt = pl.num_programs(1)

    @pl.when(ti == 0)
    def _():
        xbuf[5:8, :] = conv0_ref[0]
        hcar[...] = jnp.broadcast_to(h0_ref[0], (SUBLANES, A_WIDTH))

    xa = xa_ref[...]
    xbuf[8:8 + tt, :] = xa
    cw = cw_ref[...]
    xc = jnp.broadcast_to(cb_ref[...], (tt, A_WIDTH))
    for j in range(CONV_WIDTH):
        xc = xc + xbuf[5 + j:5 + j + tt, :] * cw[j:j + 1, :]
    xbuf[5:8, :] = xa[tt - 3:tt, :]

    gates = jnp.dot(xc.astype(BF16), wri_ref[...], preferred_element_type=F32) + bri_ref[...]
    r = _sigmoid(gates[:, :A_WIDTH])
    ig = _sigmoid(gates[:, A_WIDTH:])
    log_a = (-RG_C) * r * _softplus(-lam_ref[...])
    a = jnp.exp(log_a)
    abuf[...] = a
    bbuf[...] = jnp.sqrt(-jnp.tanh(log_a) * (a * a + 1.0)) * (ig * xc)

    rows = _row_iota((SUBLANES, A_WIDTH))

    def group(gi, carry):
        r0 = pl.multiple_of(gi * SUBLANES, SUBLANES)
        a = abuf[pl.ds(r0, SUBLANES), :]
        b = bbuf[pl.ds(r0, SUBLANES), :]
        for d in (1, 2, 4):
            a_sh = pltpu.roll(a, d, 0)
            b_sh = pltpu.roll(b, d, 0)
            keep = rows >= d
            b = jnp.where(keep, a * b_sh + b, b)
            a = jnp.where(keep, a * a_sh, a)
        h = a * hcar[...] + b
        hbuf[pl.ds(r0, SUBLANES), :] = h
        hcar[...] = jnp.broadcast_to(h[SUBLANES - 1:SUBLANES, :], (SUBLANES, A_WIDTH))
        return carry

    lax.fori_loop(0, tt // SUBLANES, group, 0)

    ya_ref[...] = hbuf[...] * _silu(ga_ref[...])

    @pl.when(ti == nt - 1)
    def _():
        convn_ref[0] = xa[tt - 3:tt, :]
        hl_ref[0] = hcar[0:1, :]


def _rglru(u, conv0, h0, cw, cb, wri, bri, lam, bsz, t, tt):
    nt = t // tt
    kern = functools.partial(_rglru_kernel, tt=tt)
    const = lambda b, i: (0, 0)
    return pl.pallas_call(
        kern,
        grid=(bsz, nt),
        in_specs=[
            pl.BlockSpec((tt, A_WIDTH), lambda b, i: (b * nt + i, U_XA // A_WIDTH)),
            pl.BlockSpec((tt, A_WIDTH), lambda b, i: (b * nt + i, U_GA // A_WIDTH)),
            pl.BlockSpec((1, 3, A_WIDTH), lambda b, i: (b, 0, 0)),
            pl.BlockSpec((1, 1, A_WIDTH), lambda b, i: (b, 0, 0)),
            pl.BlockSpec((CONV_WIDTH, A_WIDTH), const),
            pl.BlockSpec((1, A_WIDTH), const),
            pl.BlockSpec((A_WIDTH, 2 * A_WIDTH), const),
            pl.BlockSpec((1, 2 * A_WIDTH), const),
            pl.BlockSpec((1, A_WIDTH), const),
        ],
        out_specs=[
            pl.BlockSpec((tt, A_WIDTH), lambda b, i: (b * nt + i, 0)),
            pl.BlockSpec((1, 3, A_WIDTH), lambda b, i: (b, 0, 0)),
            pl.BlockSpec((1, 1, A_WIDTH), lambda b, i: (b, 0, 0)),
        ],
        out_shape=[
            jax.ShapeDtypeStruct((bsz * t, A_WIDTH), F32),
            jax.ShapeDtypeStruct((bsz, 3, A_WIDTH), F32),
            jax.ShapeDtypeStruct((bsz, 1, A_WIDTH), F32),
        ],
        scratch_shapes=[
            pltpu.VMEM((tt + 8, A_WIDTH), F32),
            pltpu.VMEM((tt, A_WIDTH), F32),
            pltpu.VMEM((tt, A_WIDTH), F32),
            pltpu.VMEM((tt, A_WIDTH), F32),
            pltpu.VMEM((SUBLANES, A_WIDTH), F32),
        ],
        compiler_params=_cparams(("parallel", "arbitrary")),
        name="rglru",
    )(u, u, conv0, h0, cw, cb, wri, bri, lam)


def _kmean_kernel(k_ref, o_ref, *, nblk):
    k = k_ref[...].reshape(nblk, MOBA_BLOCK, 256)
    o_ref[...] = jnp.sum(k, axis=1) * (1.0 / MOBA_BLOCK)


def _kmean(k2d, nblk):
    m = k2d.shape[0]
    rows = nblk * MOBA_BLOCK
    return pl.pallas_call(
        functools.partial(_kmean_kernel, nblk=nblk),
        grid=(m // rows,),
        in_specs=[pl.BlockSpec((rows, 256), lambda i: (i, 0))],
        out_specs=pl.BlockSpec((nblk, 256), lambda i: (i, 0)),
        out_shape=jax.ShapeDtypeStruct((m // MOBA_BLOCK, 256), F32),
        compiler_params=_cparams(("parallel",)),
        name="kmean",
    )(k2d)


def _select_kernel(q_ref, k_ref, v_ref, w_ref, shift_ref, qa_ref, ka_ref, va_ref, *, ts, rs):
    tile0 = pl.program_id(1) * ts
    w = w_ref[0]
    neg_shift = -shift_ref[...]

    def sub_tile(si, carry):
        r0 = pl.multiple_of(si * rs, rs)
        t0 = tile0 + r0
        q = q_ref[pl.ds(r0, rs), :]
        sc = jnp.dot(q, w, preferred_element_type=F32, precision=HIGHEST)
        shp = (rs, 256)
        pos = t0 + _row_iota(shp)
        j = _lane_iota(shp) & (NB_MAX - 1)
        n_full = pos >> 8
        valid = j < n_full
        scm = jnp.where(valid, sc, -jnp.inf)
        rank = jnp.zeros(shp, jnp.int32)
        for s in range(1, NB_MAX):
            a = pltpu.roll(scm, 256 - s, 1)
            b = pltpu.roll(scm, NB_MAX - s, 1)
            wrapped = j >= NB_MAX - s
            p = jnp.where(wrapped, b, a)
            rank = rank + jnp.where(wrapped, jnp.where(p >= scm, 1, 0), jnp.where(p > scm, 1, 0))
        allowed = (valid & (rank < MOBA_TOPK)) | (j == n_full)
        bias = jnp.where(allowed, 0.0, NEG)

        shp1 = (rs, LANES)
        lane = _lane_iota(shp1)
        pos1 = t0 + _row_iota(shp1)
        pos_hi = (pos1 >> 6).astype(F32)
        pos_lo = (pos1 & 63).astype(F32)
        blk_onehot = jnp.where((lane - 64) == (pos1 >> 8), 1.0, 0.0)
        kfeat = jnp.where(lane == 96, pos_hi, jnp.where(lane == 97, pos_lo,
                          jnp.where((lane >= 98) & (lane <= 100), 1.0, 0.0)))
        kextra = jnp.where(lane < 96, blk_onehot, kfeat)
        vextra = jnp.where(lane == 64, 1.0, 0.0)
        k = k_ref[pl.ds(r0, rs), :]
        v = v_ref[pl.ds(r0, rs), :]
        for g in range(B_KV_HEADS):
            kt = k[:, (g // 2) * LANES:(g // 2 + 1) * LANES]
            vt = v[:, (g // 2) * LANES:(g // 2 + 1) * LANES]
            if g % 2:
                kt = pltpu.roll(kt, 64, 1)
                vt = pltpu.roll(vt, 64, 1)
            ka_ref[0, g, pl.ds(r0, rs), :] = jnp.where(lane < 64, kt, kextra).astype(BF16)
            va_ref[0, g, pl.ds(r0, rs), :] = jnp.where(lane < 64, vt, vextra).astype(BF16)
            qt = q[:, g * LANES:(g + 1) * LANES]
            for h2 in range(2):
                h = 2 * g + h2
                base = qt if h2 == 0 else pltpu.roll(qt, 64, 1)
                bt = bias[:, (h // 4) * LANES:(h // 4 + 1) * LANES]
                sh = (64 - (h % 4) * NB_MAX) % LANES
                if sh:
                    bt = pltpu.roll(bt, sh, 1)
                slope = 2.0 ** (-(h + 1))
                qfeat = jnp.where(lane == 96, slope * 64.0, jnp.where(lane == 97, slope,
                                  jnp.where(lane == 98, (-slope * 64.0) * pos_hi,
                                            jnp.where(lane == 99, (-slope) * pos_lo,
                                                      jnp.where(lane == 100, neg_shift, 0.0)))))
                aug = jnp.where(lane < 64, base, jnp.where(lane < 96, bt, qfeat))
                qa_ref[0, h, pl.ds(r0, rs), :] = aug.astype(BF16)
        return carry

    lax.fori_loop(0, ts // rs, sub_tile, 0)


def _select(u, k2d, v2d, wsel, shift, bsz, s, ts):
    nt = s // ts
    return pl.pallas_call(
        functools.partial(_select_kernel, ts=ts, rs=min(ts, 64)),
        grid=(bsz, nt),
        in_specs=[
            pl.BlockSpec((ts, 512), lambda b, i: (b * nt + i, U_QB // 512)),
            pl.BlockSpec((ts, 256), lambda b, i: (b * nt + i, 0)),
            pl.BlockSpec((ts, 256), lambda b, i: (b * nt + i, 0)),
            pl.BlockSpec((1, 512, 256), lambda b, i: (b, 0, 0)),
            pl.BlockSpec((1, LANES), lambda b, i: (0, 0)),
        ],
        out_specs=[
            pl.BlockSpec((1, B_HEADS, ts, LANES), lambda b, i: (b, 0, i, 0)),
            pl.BlockSpec((1, B_KV_HEADS, ts, LANES), lambda b, i: (b, 0, i, 0)),
            pl.BlockSpec((1, B_KV_HEADS, ts, LANES), lambda b, i: (b, 0, i, 0)),
        ],
        out_shape=[
            jax.ShapeDtypeStruct((bsz, B_HEADS, s, LANES), BF16),
            jax.ShapeDtypeStruct((bsz, B_KV_HEADS, s, LANES), BF16),
            jax.ShapeDtypeStruct((bsz, B_KV_HEADS, s, LANES), BF16),
        ],
        compiler_params=_cparams(("parallel", "parallel")),
        name="moba_select",
    )(u, k2d, v2d, wsel, shift)


ATTN_KV_BLOCKS = 2
SHIFT_LIMIT = 30.0


def _attn_finish(o_ref, acc_sc, tq):
    acc = acc_sc[...]
    res = acc / acc[:, HEAD_DIM:HEAD_DIM + 1]
    lane = _lane_iota((tq, LANES))
    o_ref[...] = jnp.where(lane < 64, res[:tq], pltpu.roll(res[tq:], 64, 1))


def _attn_online(qa_ref, ka_ref, va_ref, o_ref, m_sc, acc_sc, tq):
    qi = pl.program_id(2)
    rows = 2 * tq
    q = qa_ref[0].reshape(rows, LANES)

    k0 = ka_ref[0, 0, pl.ds(pl.multiple_of(qi * tq, tq), tq), :]
    v0 = va_ref[0, 0, pl.ds(pl.multiple_of(qi * tq, tq), tq), :]
    s = _dot_nt(q, k0)
    shp = (rows, tq)
    qpos = _row_iota(shp) & (tq - 1)
    s = jnp.where(_lane_iota(shp) <= qpos, s, NEG)
    m0 = jnp.max(s, axis=-1, keepdims=True)
    p = jnp.exp(s - m0)
    m_sc[...] = jnp.broadcast_to(m0, (rows, LANES))
    acc_sc[...] = jnp.dot(p.astype(BF16), v0, preferred_element_type=F32)

    def body(jb, carry):
        r0 = pl.multiple_of(jb * tq, tq)
        kj = ka_ref[0, 0, pl.ds(r0, tq), :]
        vj = va_ref[0, 0, pl.ds(r0, tq), :]
        sj = _dot_nt(q, kj)
        m_prev = m_sc[...]
        m_new = jnp.maximum(m_prev, jnp.max(sj, axis=-1, keepdims=True))
        alpha = jnp.exp(m_prev - m_new)
        pj = jnp.exp(sj - m_new[:, 0:1])
        acc_sc[...] = alpha * acc_sc[...] + jnp.dot(pj.astype(BF16), vj, preferred_element_type=F32)
        m_sc[...] = m_new
        return carry

    lax.fori_loop(0, qi, body, 0)
    _attn_finish(o_ref, acc_sc, tq)


def _attn_shifted(qa_ref, ka_ref, va_ref, o_ref, acc_sc, tq):
    qi = pl.program_id(2)
    rows = 2 * tq
    keys = ATTN_KV_BLOCKS * tq
    q = qa_ref[0].reshape(rows, LANES)
    shp = (rows, keys)
    qpos = qi * tq + (_row_iota(shp) & (tq - 1))
    kofs = _lane_iota(shp)
    acc_sc[...] = jnp.zeros((rows, LANES), F32)

    def body(jp, carry):
        r0 = pl.multiple_of(jp * keys, keys)
        kj = ka_ref[0, 0, pl.ds(r0, keys), :]
        vj = va_ref[0, 0, pl.ds(r0, keys), :]
        s = _dot_nt(q, kj)
        p = jnp.exp(jnp.where(r0 + kofs <= qpos, s, NEG))
        acc_sc[...] += jnp.dot(p.astype(BF16), vj, preferred_element_type=F32)
        return carry

    lax.fori_loop(0, (qi + ATTN_KV_BLOCKS) // ATTN_KV_BLOCKS, body, 0)
    _attn_finish(o_ref, acc_sc, tq)


def _attn_kernel(mode_ref, qa_ref, ka_ref, va_ref, o_ref, m_sc, acc_sc, *, tq):
    @pl.when(mode_ref[0] == 1)
    def _():
        _attn_shifted(qa_ref, ka_ref, va_ref, o_ref, acc_sc, tq)

    @pl.when(mode_ref[0] != 1)
    def _():
        _attn_online(qa_ref, ka_ref, va_ref, o_ref, m_sc, acc_sc, tq)


def _attn(mode, qa, ka, va, bsz, s):
    tq = MOBA_BLOCK
    nq = s // tq
    assert nq % ATTN_KV_BLOCKS == 0
    grid_spec = pltpu.PrefetchScalarGridSpec(
        num_scalar_prefetch=1,
        grid=(bsz, B_KV_HEADS, nq),
        in_specs=[
            pl.BlockSpec((1, 2, tq, LANES), lambda b, g, i, mode: (b, g, i, 0)),
            pl.BlockSpec((1, 1, s, LANES), lambda b, g, i, mode: (b, g, 0, 0)),
            pl.BlockSpec((1, 1, s, LANES), lambda b, g, i, mode: (b, g, 0, 0)),
        ],
        out_specs=pl.BlockSpec((tq, LANES), lambda b, g, i, mode: (b * nq + i, g)),
        scratch_shapes=[
            pltpu.VMEM((2 * tq, LANES), F32),
            pltpu.VMEM((2 * tq, LANES), F32),
        ],
    )
    return pl.pallas_call(
        functools.partial(_attn_kernel, tq=tq),
        grid_spec=grid_spec,
        out_shape=jax.ShapeDtypeStruct((bsz * s, 512), F32),
        compiler_params=_cparams(("parallel", "parallel", "arbitrary")),
        name="moba_attn",
    )(mode, qa, ka, va)


PAGES_PER_STEP = 16


def _page_copies(pt_ref, src_hbm, buf, sem, b, c, slot):
    return [pltpu.make_async_copy(src_hbm.at[pt_ref[b, c * PAGES_PER_STEP + p]], buf.at[slot, p], sem.at[slot])
            for p in range(PAGES_PER_STEP)]


def _skmean_kernel(pt_ref, ck_hbm, o_ref, kbuf, sem, *, nch):
    b = pl.program_id(0)
    blocks = PAGES_PER_STEP * PAGE_SIZE // MOBA_BLOCK
    for cp in _page_copies(pt_ref, ck_hbm, kbuf, sem, b, 0, 0):
        cp.start()
    for c in range(nch):
        slot = c % 2
        if c + 1 < nch:
            for cp in _page_copies(pt_ref, ck_hbm, kbuf, sem, b, c + 1, 1 - slot):
                cp.start()
        for cp in _page_copies(pt_ref, ck_hbm, kbuf, sem, b, c, slot):
            cp.wait()
        page_sums = jnp.sum(kbuf[slot], axis=-1)
        ppb = MOBA_BLOCK // PAGE_SIZE
        o_ref[0, c * blocks:(c + 1) * blocks, :] = (
            jnp.sum(page_sums.reshape(blocks, ppb, 256), axis=1) * (1.0 / MOBA_BLOCK))


def _skmean(pt, ck, bsz, n_pages):
    nch = n_pages // PAGES_PER_STEP
    nblk = n_pages * PAGE_SIZE // MOBA_BLOCK
    grid_spec = pltpu.PrefetchScalarGridSpec(
        num_scalar_prefetch=1,
        grid=(bsz,),
        in_specs=[pl.BlockSpec(memory_space=pl.ANY)],
        out_specs=pl.BlockSpec((1, nblk, 256), lambda b, pt: (b, 0, 0)),
        scratch_shapes=[
            pltpu.VMEM((2, PAGES_PER_STEP, 256, PAGE_SIZE), F32),
            pltpu.SemaphoreType.DMA((2,)),
        ],
    )
    return pl.pallas_call(
        functools.partial(_skmean_kernel, nch=nch),
        grid_spec=grid_spec,
        out_shape=jax.ShapeDtypeStruct((bsz, nblk, 256), F32),
        compiler_params=_cparams(("arbitrary",)),
        name="sample_kmean",
    )(pt, ck)


def _sattn_kernel(pt_ref, q_ref, km_ref, kn_ref, vn_ref, slope_ref, ck_hbm, cv_hbm, o_ref,
                  kbuf, vbuf, ksem, vsem, *, nch, nblk, tnew, past):
    b = pl.program_id(0)
    rows = 64
    qf = q_ref[0]
    qb = qf.astype(BF16)
    slope = slope_ref[...]
    shp = (rows, LANES)
    lane = _lane_iota(shp)
    tok = _row_iota(shp) & (tnew - 1)

    def start(c, slot):
        for cp in _page_copies(pt_ref, ck_hbm, kbuf, ksem, b, c, slot):
            cp.start()
        for cp in _page_copies(pt_ref, cv_hbm, vbuf, vsem, b, c, slot):
            cp.start()

    def wait(c, slot):
        for cp in _page_copies(pt_ref, ck_hbm, kbuf, ksem, b, c, slot):
            cp.wait()
        for cp in _page_copies(pt_ref, cv_hbm, vbuf, vsem, b, c, slot):
            cp.wait()

    start(0, 0)

    km = km_ref[0]
    km2 = jnp.concatenate([km, km], axis=0)
    sc = _dot_nt(qf, km2, precision=HIGHEST)
    j = lane & (nblk - 1)
    rank = jnp.zeros(shp, jnp.int32)
    for s in range(1, nblk):
        a = pltpu.roll(sc, 2 * nblk - s, 1)
        wrapped = j >= nblk - s
        rank = rank + jnp.where(wrapped, jnp.where(a >= sc, 1, 0), jnp.where(a > sc, 1, 0))
    bias = jnp.where(rank < MOBA_TOPK, 0.0, NEG)

    s_own = _dot_nt(qb, kn_ref[0].astype(BF16))
    s_own = s_own - slope * (tok - lane).astype(F32)
    s_own = jnp.where((lane <= tok) & (lane < tnew), s_own, NEG)
    m = jnp.max(s_own, axis=-1, keepdims=True)
    p = jnp.exp(s_own - m)
    l = jnp.sum(p, axis=-1, keepdims=True)
    acc = jnp.dot(p.astype(BF16), vn_ref[0].astype(BF16), preferred_element_type=F32)

    keys = PAGES_PER_STEP * PAGE_SIZE
    blocks = keys // MOBA_BLOCK
    kshape = (rows, keys)
    klane = _lane_iota(kshape)
    qpos = past + (_row_iota(kshape) & (tnew - 1))
    slope_k = jnp.broadcast_to(slope[:, 0:1], kshape)
    for c in range(nch):
        slot = c % 2
        if c + 1 < nch:
            start(c + 1, 1 - slot)
        wait(c, slot)
        kc = kbuf[slot].astype(BF16)
        vc = vbuf[slot].astype(BF16)
        s_c = jnp.concatenate([jnp.dot(qb, kc[pg], preferred_element_type=F32)
                               for pg in range(PAGES_PER_STEP)], axis=1)
        s_c = s_c - slope_k * (qpos - (c * keys + klane)).astype(F32)
        sel = jnp.concatenate(
            [jnp.broadcast_to(bias[:, c * blocks + jj:c * blocks + jj + 1], (rows, MOBA_BLOCK))
             for jj in range(blocks)], axis=1)
        s_c = s_c + sel
        m_new = jnp.maximum(m, jnp.max(s_c, axis=-1, keepdims=True))
        alpha = jnp.exp(m - m_new)
        p = jnp.exp(s_c - m_new)
        l = alpha * l + jnp.sum(p, axis=-1, keepdims=True)
        pb = p.astype(BF16)
        pv = _dot_nt(pb[:, 0:PAGE_SIZE], vc[0])
        for pg in range(1, PAGES_PER_STEP):
            pv = pv + _dot_nt(pb[:, pg * PAGE_SIZE:(pg + 1) * PAGE_SIZE], vc[pg])
        acc = alpha * acc + pv
        m = m_new

    res = acc / l
    oshape = (rows, 256)
    own = (_lane_iota(oshape) >> 6) == (_row_iota(oshape) >> 4)
    res = jnp.where(own, res, 0.0)
    folded = res + pltpu.roll(res, 64, 1) + pltpu.roll(res, 128, 1) + pltpu.roll(res, 192, 1)
    o_ref[0] = folded[:, :HEAD_DIM]


def _sattn(pt, qbd, kmean, knew, vnew, slopes, ck, cv, bsz, n_pages, tnew):
    nch = n_pages // PAGES_PER_STEP
    nblk = n_pages * PAGE_SIZE // MOBA_BLOCK
    past = n_pages * PAGE_SIZE
    grid_spec = pltpu.PrefetchScalarGridSpec(
        num_scalar_prefetch=1,
        grid=(bsz,),
        in_specs=[
            pl.BlockSpec((1, 64, 256), lambda b, pt: (b, 0, 0)),
            pl.BlockSpec((1, nblk, 256), lambda b, pt: (b, 0, 0)),
            pl.BlockSpec((1, LANES, 256), lambda b, pt: (b, 0, 0)),
            pl.BlockSpec((1, LANES, 256), lambda b, pt: (b, 0, 0)),
            pl.BlockSpec((64, LANES), lambda b, pt: (0, 0)),
            pl.BlockSpec(memory_space=pl.ANY),
            pl.BlockSpec(memory_space=pl.ANY),
        ],
        out_specs=pl.BlockSpec((1, 64, HEAD_DIM), lambda b, pt: (b, 0, 0)),
        scratch_shapes=[
            pltpu.VMEM((2, PAGES_PER_STEP, 256, PAGE_SIZE), F32),
            pltpu.VMEM((2, PAGES_PER_STEP, 256, PAGE_SIZE), F32),
            pltpu.SemaphoreType.DMA((2,)),
            pltpu.SemaphoreType.DMA((2,)),
        ],
    )
    return pl.pallas_call(
        functools.partial(_sattn_kernel, nch=nch, nblk=nblk, tnew=tnew, past=past),
        grid_spec=grid_spec,
        out_shape=jax.ShapeDtypeStruct((bsz, 64, HEAD_DIM), F32),
        compiler_params=_cparams(("arbitrary",)),
        name="sample_attn",
    )(pt, qbd, kmean, knew, vnew, slopes, ck, cv)


def _mlstm_kernel(q_ref, k_ref, v_ref, if_ref, o_ref, g_ref, bias_ref, og_ref, c0_ref, m0_ref,
                  y_ref, cout_ref, mout_ref, c_sc, m_sc, *, tc, L):
    ti = pl.program_id(1)
    nt = pl.num_programs(1)

    @pl.when(ti == 0)
    def _():
        c_sc[...] = c0_ref[0]
        m_sc[...] = m0_ref[0]

    gates = if_ref[...] + bias_ref[...]
    lf = jnp.minimum(gates, 0.0) - jnp.log1p(jnp.exp(-jnp.abs(gates)))
    tri = _lane_iota((L, L)) <= _row_iota((L, L))
    tri_f = jnp.where(tri, 1.0, 0.0)
    lane = _lane_iota((L, LANES))
    og = og_ref[...]
    for c in range(tc // L):
        r0 = c * L
        fcum = jnp.dot(tri_f, lf[r0:r0 + L], preferred_element_type=F32, precision=HIGHEST)
        z = gates[r0:r0 + L] - pltpu.roll(fcum, LANES - 4, 1)
        qt = q_ref[r0:r0 + L, :]
        kt = k_ref[r0:r0 + L, :] * (HEAD_DIM ** -0.5)
        vt = v_ref[r0:r0 + L, :]
        ot = _sigmoid(o_ref[r0:r0 + L, :])
        for pair in range(C_HEADS // 2):
            ytile = jnp.zeros((L, LANES), F32)
            for h2 in range(2):
                h = 2 * pair + h2
                sl = slice(pair * LANES, (pair + 1) * LANES)
                qh, kh, vh, oh = qt[:, sl], kt[:, sl], vt[:, sl], ot[:, sl]
                if h2:
                    qh, kh, vh, oh = [pltpu.roll(a, 64, 1) for a in (qh, kh, vh, oh)]
                qh = jnp.where(lane < 64, qh, 0.0).astype(BF16)
                kh = jnp.where(lane < 64, kh, 0.0).astype(BF16)
                vaug = jnp.where(lane < 64, vh, jnp.where(lane == 64, 1.0, 0.0))
                m_prev = m_sc[h:h + 1, 0:1]
                f_col = fcum[:, 4 + h:5 + h]
                z_col = z[:, h:h + 1]
                pick = jnp.where(lane == h, 1.0, 0.0)
                z_row = _dot_nt(pick, z, precision=HIGHEST)
                d_log = jnp.where(tri, f_col + z_row, -jnp.inf)
                inter = f_col + m_prev
                m_t = jnp.maximum(jnp.max(d_log, axis=-1, keepdims=True), inter)
                w_intra = _dot_nt(qh, kh) * jnp.exp(d_log - m_t)
                w_inter = jnp.exp(inter - m_t)
                c_prev = c_sc[h]
                num = (w_inter * jnp.dot(qh, c_prev.astype(BF16), preferred_element_type=F32)
                       + jnp.dot(w_intra.astype(BF16), vaug.astype(BF16), preferred_element_type=F32))
                den = num[:, HEAD_DIM:HEAD_DIM + 1]
                hh = num / jnp.maximum(jnp.abs(den), jnp.exp(-m_t))
                f_tot = f_col[L - 1:L, :]
                w_log = f_tot + z_col
                m_new = jnp.maximum(f_tot + m_prev, jnp.max(w_log, axis=0, keepdims=True))
                decay = jnp.exp(f_tot + m_prev - m_new)
                w = jnp.exp(w_log - m_new)
                upd = lax.dot_general(kh, (w * vaug).astype(BF16), (((0,), (0,)), ((), ())),
                                      preferred_element_type=F32)
                c_sc[h] = decay * c_prev + upd
                m_sc[h:h + 1, :] = jnp.broadcast_to(m_new, (1, LANES))
                ho = jnp.where(lane < 64, hh * oh, 0.0)
                ms = jnp.sum(ho * ho, axis=-1, keepdims=True) * (1.0 / HEAD_DIM)
                yh = ho * lax.rsqrt(ms + NORM_EPS)
                ytile = ytile + (pltpu.roll(yh, 64, 1) if h2 else yh)
            sl = slice(pair * LANES, (pair + 1) * LANES)
            y_ref[r0:r0 + L, sl] = ytile * og[:, sl] * _silu(g_ref[r0:r0 + L, sl])

    @pl.when(ti == nt - 1)
    def _():
        cout_ref[0] = c_sc[...]
        mout_ref[0] = m_sc[...]


def _mlstm(u, gate_bias, onorm_g, c0aug, m0, bsz, t, tc, L):
    nt = t // tc
    const = lambda b, i: (0, 0)
    ucol = lambda off, w: (lambda b, i: (b * nt + i, off // w))
    return pl.pallas_call(
        functools.partial(_mlstm_kernel, tc=tc, L=L),
        grid=(bsz, nt),
        in_specs=[
            pl.BlockSpec((tc, 256), ucol(U_CQ, 256)),
            pl.BlockSpec((tc, 256), ucol(U_CK, 256)),
            pl.BlockSpec((tc, 256), ucol(U_CV, 256)),
            pl.BlockSpec((tc, LANES), ucol(U_CIF, LANES)),
            pl.BlockSpec((tc, 256), ucol(U_CO, 256)),
            pl.BlockSpec((tc, 256), ucol(U_CG, 256)),
            pl.BlockSpec((1, LANES), const),
            pl.BlockSpec((1, 256), const),
            pl.BlockSpec((1, C_HEADS, LANES, LANES), lambda b, i: (b, 0, 0, 0)),
            pl.BlockSpec((1, SUBLANES, LANES), lambda b, i: (b, 0, 0)),
        ],
        out_specs=[
            pl.BlockSpec((tc, 256), lambda b, i: (b * nt + i, 0)),
            pl.BlockSpec((1, C_HEADS, LANES, LANES), lambda b, i: (b, 0, 0, 0)),
            pl.BlockSpec((1, SUBLANES, LANES), lambda b, i: (b, 0, 0)),
        ],
        out_shape=[
            jax.ShapeDtypeStruct((bsz * t, 256), F32),
            jax.ShapeDtypeStruct((bsz, C_HEADS, LANES, LANES), F32),
            jax.ShapeDtypeStruct((bsz, SUBLANES, LANES), F32),
        ],
        scratch_shapes=[
            pltpu.VMEM((C_HEADS, LANES, LANES), F32),
            pltpu.VMEM((SUBLANES, LANES), F32),
        ],
        compiler_params=_cparams(("parallel", "arbitrary")),
        name="mlstm",
    )(u, u, u, u, u, u, gate_bias, onorm_g, c0aug, m0)


def _outproj_kernel(x_ref, ya_ref, yb_ref, gb_ref, yc_ref, p_ref, wa_ref, wb_ref, wc_ref,
                    pg_ref, pp_ref, png_ref, o_ref):
    yb = yb_ref[...] * _silu(gb_ref[...])
    x = (x_ref[...]
         + jnp.dot(ya_ref[...].astype(BF16), wa_ref[...], preferred_element_type=F32)
         + jnp.dot(yb.astype(BF16), wb_ref[...], preferred_element_type=F32)
         + jnp.dot(yc_ref[...].astype(BF16), wc_ref[...], preferred_element_type=F32))
    gate = _sigmoid(jnp.dot(x.astype(BF16), pg_ref[...], preferred_element_type=F32))
    e = jnp.dot(p_ref[...].astype(BF16), pp_ref[...], preferred_element_type=F32)
    e = e * lax.rsqrt(jnp.mean(e * e, axis=-1, keepdims=True) + NORM_EPS) * png_ref[...]
    o_ref[...] = x + gate * e


def _outproj(x2d, ya, yb, u, yc, p2d, wa, wb, wc, pg, pp, png, tm):
    m = x2d.shape[0]
    row = lambda i: (i, 0)
    const = lambda i: (0, 0)
    return pl.pallas_call(
        _outproj_kernel,
        grid=(m // tm,),
        in_specs=[
            pl.BlockSpec((tm, D_MODEL), row),
            pl.BlockSpec((tm, 256), row),
            pl.BlockSpec((tm, 512), row),
            pl.BlockSpec((tm, 512), lambda i: (i, U_GB // 512)),
            pl.BlockSpec((tm, 256), row),
            pl.BlockSpec((tm, PLE_DIM), row),
            pl.BlockSpec((256, D_MODEL), const),
            pl.BlockSpec((512, D_MODEL), const),
            pl.BlockSpec((256, D_MODEL), const),
            pl.BlockSpec((D_MODEL, D_MODEL), const),
            pl.BlockSpec((PLE_DIM, D_MODEL), const),
            pl.BlockSpec((1, D_MODEL), const),
        ],
        out_specs=pl.BlockSpec((tm, D_MODEL), row),
        out_shape=jax.ShapeDtypeStruct((m, D_MODEL), F32),
        compiler_params=_cparams(("parallel",)),
        name="outproj",
    )(x2d, ya, yb, u, yc, p2d, wa, wb, wc, pg, pp, png)


def _block_diag(w):
    n, a, b = w.shape
    eye = jnp.eye(n, dtype=w.dtype)
    return jnp.einsum("nab,nm->namb", w, eye).reshape(n * a, n * b)


def _prep_layer(l, w):
    (norm_g, w_in, a_conv_w, a_conv_b, a_w_r, a_b_r, a_w_i, a_b_i, a_lambda, b_qnorm_g, b_knorm_g,
     c_b_i, c_b_f, c_onorm_g, w_out, ple_gate, ple_proj, ple_norm_g) = [a[l] for a in w]
    col = lambda i: w_in[:, _REF_OFFS[i]:_REF_OFFS[i + 1]]
    w_u = jnp.concatenate(
        [col(0), col(1), col(2), col(5), col(3), col(4), col(6), col(7), col(8), col(11), col(12),
         col(9), col(10), jnp.zeros((D_MODEL, LANES - 2 * C_HEADS), F32)], axis=1).astype(BF16)
    gate_bias = jnp.concatenate([c_b_i, c_b_f, jnp.zeros((LANES - 2 * C_HEADS,), F32)])[None, :]
    shift = 1.02 * HEAD_DIM ** 0.5 * jnp.max(jnp.abs(b_qnorm_g)) * jnp.max(jnp.abs(b_knorm_g)) + 0.5
    return dict(
        shift=shift,
        norm_g=norm_g[None, :], w_u=w_u,
        qg=jnp.tile(b_qnorm_g, B_HEADS)[None, :], kg=jnp.tile(b_knorm_g, B_KV_HEADS)[None, :],
        bd=_block_diag(jnp.ones((B_HEADS, HEAD_DIM, HEAD_DIM), BF16)),
        cw=a_conv_w, cb=a_conv_b[None, :],
        wri=jnp.concatenate([_block_diag(a_w_r), _block_diag(a_w_i)], axis=1).astype(BF16),
        bri=jnp.concatenate([a_b_r, a_b_i])[None, :], lam=a_lambda[None, :],
        gate_bias=gate_bias, onorm_g=c_onorm_g[None, :],
        wa=w_out[:256].astype(BF16), wb=w_out[256:768].astype(BF16), wc=w_out[768:].astype(BF16),
        pg=ple_gate.astype(BF16), pp=ple_proj.astype(BF16), png=ple_norm_g[None, :],
    )


def _pick_tile(n, cap):
    t = min(n, cap)
    while n % t:
        t //= 2
    return t


def _layer(x2d, p2d, bsz, t, conv0, h0, c0, n0, m0, lw, attend):
    m = bsz * t
    tm = _pick_tile(m, 256)
    u, k2d, v2d = _inproj(x2d, lw["norm_g"], lw["w_u"], lw["qg"], lw["kg"], lw["bd"], tm)
    ya, conv_new, h_last = _rglru(u, conv0, h0[:, None, :], lw["cw"], lw["cb"], lw["wri"], lw["bri"],
                                  lw["lam"], bsz, t, _pick_tile(t, 512))
    yb = attend(u, k2d, v2d)
    c0aug = jnp.zeros((bsz, C_HEADS, LANES, LANES), F32)
    c0aug = c0aug.at[:, :, :HEAD_DIM, :HEAD_DIM].set(c0).at[:, :, :HEAD_DIM, HEAD_DIM].set(n0)
    m0b = jnp.zeros((bsz, SUBLANES, LANES), F32).at[:, :C_HEADS, :].set(
        jnp.broadcast_to(m0[:, :, None], (bsz, C_HEADS, LANES)))
    L = int(np.gcd(t, MLSTM_CHUNK))
    yc, caug, mout = _mlstm(u, lw["gate_bias"], lw["onorm_g"], c0aug, m0b, bsz, t, _pick_tile(t, 512), L)
    xo = _outproj(x2d, ya, yb, u, yc, p2d, lw["wa"], lw["wb"], lw["wc"], lw["pg"], lw["pp"], lw["png"], tm)
    state = (k2d.reshape(bsz, t, B_KV_HEADS, HEAD_DIM), v2d.reshape(bsz, t, B_KV_HEADS, HEAD_DIM),
             conv_new, h_last[:, 0, :],
             caug[:, :, :HEAD_DIM, :HEAD_DIM], caug[:, :, :HEAD_DIM, HEAD_DIM], mout[:, :C_HEADS, 0])
    return xo, state


def _attend_prompt(u, k2d, v2d, bsz, s, shift):
    nb = s // MOBA_BLOCK
    assert s % MOBA_BLOCK == 0 and nb <= NB_MAX
    kmean = _kmean(k2d, _pick_tile(bsz * nb, 8)).reshape(bsz, nb, B_KV_HEADS, HEAD_DIM)
    kmh = jnp.repeat(kmean, B_HEADS // B_KV_HEADS, axis=2).transpose(0, 2, 3, 1)
    kmh = jnp.pad(kmh, ((0, 0), (0, 0), (0, 0), (0, NB_MAX - nb)))
    wsel = jnp.einsum("bhdj,hg->bhdgj", kmh, jnp.eye(B_HEADS, dtype=F32)).reshape(bsz, 512, B_HEADS * NB_MAX)
    qa, ka, va = _select(u, k2d, v2d, wsel, jnp.full((1, LANES), shift, F32), bsz, s, _pick_tile(s, 512))
    mode = (shift <= SHIFT_LIMIT).astype(jnp.int32).reshape(1)
    return _attn(mode, qa, ka, va, bsz, s)


def _attend_sample(u, k2d, v2d, bsz, t, pt, ck, cv, slopes):
    n_pages = pt.shape[1]
    assert t == 8 and n_pages % PAGES_PER_STEP == 0
    nblk = n_pages * PAGE_SIZE // MOBA_BLOCK
    assert nblk == 64
    kmean = _skmean(pt, ck, bsz, n_pages)
    q = u[:, U_QB:U_QB + 512].reshape(bsz, t, B_KV_HEADS, 2, HEAD_DIM).transpose(0, 2, 3, 1, 4)
    qbd = jnp.einsum("bghtd,gk->bghtkd", q, jnp.eye(B_KV_HEADS, dtype=F32)).reshape(bsz, 64, 256)
    pad = ((0, 0), (0, LANES - t), (0, 0))
    knew = jnp.pad(k2d.reshape(bsz, t, 256), pad)
    vnew = jnp.pad(v2d.reshape(bsz, t, 256), pad)
    o = _sattn(pt, qbd, kmean, knew, vnew, slopes, ck, cv, bsz, n_pages, t)
    return o.reshape(bsz, B_HEADS, t, HEAD_DIM).transpose(0, 2, 1, 3).reshape(bsz * t, 512)


def kernel(x_prompt, x_sample, cache_k, cache_v, state_rglru_conv, state_rglru_h, state_mlstm_C, state_mlstm_n, state_mlstm_m, page_table, p_prompt, p_sample, norm_g, w_in, a_conv_w, a_conv_b, a_w_r, a_b_r, a_w_i, a_b_i, a_lambda, b_qnorm_g, b_knorm_g, c_b_i, c_b_f, c_onorm_g, w_out, ple_gate, ple_proj, ple_norm_g):
    weights = (norm_g, w_in, a_conv_w, a_conv_b, a_w_r, a_b_r, a_w_i, a_b_i, a_lambda, b_qnorm_g, b_knorm_g,
               c_b_i, c_b_f, c_onorm_g, w_out, ple_gate, ple_proj, ple_norm_g)
    depth = w_in.shape[0]
    bp, sp = x_prompt.shape[:2]
    bs, ts = x_sample.shape[:2]
    n_phys = cache_k.shape[1]
    ck = cache_k.transpose(0, 1, 3, 4, 2).reshape(depth * n_phys, 256, PAGE_SIZE)
    cv = cache_v.transpose(0, 1, 3, 4, 2).reshape(depth * n_phys, 256, PAGE_SIZE)
    slopes = jnp.asarray(np.repeat(2.0 ** -(np.arange(1, B_HEADS + 1)), ts)[:, None]
                         * np.ones((1, LANES)), F32)

    xp = x_prompt.reshape(bp * sp, D_MODEL)
    xs = x_sample.reshape(bs * ts, D_MODEL)
    zeros = lambda *s: jnp.zeros(s, F32)
    sp_all, ss_all = [], []
    for l in range(depth):
        lw = _prep_layer(l, weights)
        xp, st_p = _layer(xp, p_prompt[l].reshape(bp * sp, PLE_DIM), bp, sp,
                          zeros(bp, CONV_WIDTH - 1, A_WIDTH), zeros(bp, A_WIDTH),
                          zeros(bp, C_HEADS, HEAD_DIM, HEAD_DIM), zeros(bp, C_HEADS, HEAD_DIM), zeros(bp, C_HEADS),
                          lw, functools.partial(_attend_prompt, bsz=bp, s=sp, shift=lw["shift"]))
        attend_s = functools.partial(_attend_sample, bsz=bs, t=ts, pt=page_table + l * n_phys,
                                     ck=ck, cv=cv, slopes=slopes)
        xs, st_s = _layer(xs, p_sample[l].reshape(bs * ts, PLE_DIM), bs, ts,
                          state_rglru_conv[l], state_rglru_h[l], state_mlstm_C[l], state_mlstm_n[l],
                          state_mlstm_m[l], lw, attend_s)
        sp_all.append(st_p)
        ss_all.append(st_s)

    stk = lambda states, j: jnp.stack([s_[j] for s_ in states])
    return (xp.reshape(bp, sp, D_MODEL), xs.reshape(bs, ts, D_MODEL),
            stk(sp_all, 0), stk(sp_all, 1), stk(ss_all, 0), stk(ss_all, 1),
            stk(sp_all, 2), stk(ss_all, 2), stk(sp_all, 3), stk(ss_all, 3),
            stk(sp_all, 4), stk(ss_all, 4), stk(sp_all, 5), stk(ss_all, 5),
            stk(sp_all, 6), stk(ss_all, 6))
```

```python
import functools

import numpy as np
import jax
import jax.numpy as jnp
from jax import lax
from jax.experimental import pallas as pl
from jax.experimental.pallas import tpu as pltpu

F32 = jnp.float32
BF16 = jnp.bfloat16
HIGHEST = lax.Precision.HIGHEST

LANES = 128
SUBLANES = 8

HEAD_DIM = 64
D_MODEL = 1024
A_WIDTH = 256
CONV_WIDTH = 4
RG_C = 8.0
B_HEADS = 8
B_KV_HEADS = 4
C_HEADS = 4
MOBA_BLOCK = 256
MOBA_TOPK = 3
PAGE_SIZE = 128
MLSTM_CHUNK = 64
PLE_DIM = 256
NORM_EPS = 1e-6
NB_MAX = 32
NEG = -1e30

U_XA, U_GA = 0, 256
U_QB, U_GB, U_KB, U_VB = 512, 1024, 1536, 1792
U_CQ, U_CK, U_CV, U_CO, U_CG, U_CIF = 2048, 2304, 2560, 2816, 3072, 3328
D_U = 3456
_REF_SPLITS = (256, 256, 512, 256, 256, 512, 256, 256, 256, 4, 4, 256, 256)
_REF_OFFS = np.concatenate([[0], np.cumsum(_REF_SPLITS)]).tolist()

VMEM_LIMIT = 48 * 1024 * 1024


def _cparams(sem):
    return pltpu.CompilerParams(dimension_semantics=sem, vmem_limit_bytes=VMEM_LIMIT)


def _lane_iota(shape):
    return lax.broadcasted_iota(jnp.int32, shape, len(shape) - 1)


def _row_iota(shape):
    return lax.broadcasted_iota(jnp.int32, shape, len(shape) - 2)


def _sigmoid(x):
    return 1.0 / (1.0 + jnp.exp(-x))


def _silu(x):
    return x * _sigmoid(x)


def _softplus(x):
    return jnp.maximum(x, 0.0) + jnp.log1p(jnp.exp(-jnp.abs(x)))


def _dot_nt(a, b, **kw):
    return lax.dot_general(a, b, (((1,), (1,)), ((), ())), preferred_element_type=F32, **kw)


def _inproj_kernel(x_ref, g_ref, w_ref, qg_ref, kg_ref, bd_ref, u_ref, k_ref, v_ref):
    x = x_ref[...]
    xn = x * lax.rsqrt(jnp.mean(x * x, axis=-1, keepdims=True) + NORM_EPS) * g_ref[...]
    u = jnp.dot(xn.astype(BF16), w_ref[...], preferred_element_type=F32)
    u_ref[...] = u
    bd = bd_ref[...]
    q = u[:, U_QB:U_QB + 512]
    qss = jnp.dot((q * q).astype(BF16), bd, preferred_element_type=F32) * (1.0 / HEAD_DIM)
    qn = q * lax.rsqrt(qss + NORM_EPS) * qg_ref[...] * (HEAD_DIM ** -0.5)
    u_ref[:, U_QB:U_QB + 512] = qn
    k = u[:, U_KB:U_KB + 256]
    kss = jnp.dot((k * k).astype(BF16), bd[:256, :256], preferred_element_type=F32) * (1.0 / HEAD_DIM)
    kn = k * lax.rsqrt(kss + NORM_EPS) * kg_ref[...]
    u_ref[:, U_KB:U_KB + 256] = kn
    k_ref[...] = kn
    v_ref[...] = u[:, U_VB:U_VB + 256]


def _inproj(x2d, norm_g, w_u, qg, kg, bd, tm):
    m = x2d.shape[0]
    const = lambda i: (0, 0)
    return pl.pallas_call(
        _inproj_kernel,
        grid=(m // tm,),
        in_specs=[
            pl.BlockSpec((tm, D_MODEL), lambda i: (i, 0)),
            pl.BlockSpec((1, D_MODEL), const),
            pl.BlockSpec((D_MODEL, D_U), const),
            pl.BlockSpec((1, 512), const),
            pl.BlockSpec((1, 256), const),
            pl.BlockSpec((512, 512), const),
        ],
        out_specs=[
            pl.BlockSpec((tm, D_U), lambda i: (i, 0)),
            pl.BlockSpec((tm, 256), lambda i: (i, 0)),
            pl.BlockSpec((tm, 256), lambda i: (i, 0)),
        ],
        out_shape=[
            jax.ShapeDtypeStruct((m, D_U), F32),
            jax.ShapeDtypeStruct((m, 256), F32),
            jax.ShapeDtypeStruct((m, 256), F32),
        ],
        compiler_params=_cparams(("parallel",)),
        name="inproj",
    )(x2d, norm_g, w_u, qg, kg, bd)


def _rglru_kernel(xa_ref, ga_ref, conv0_ref, h0_ref, cw_ref, cb_ref, wri_ref, bri_ref, lam_ref,
                  ya_ref, convn_ref, hl_ref, xbuf, abuf, bbuf, hbuf, hcar, *, tt):
    ti = pl.program_id(1)
    nt = pl.num_programs(1)

    @pl.when(ti == 0)
    def _():
        xbuf[5:8, :] = conv0_ref[0]
        hcar[...] = jnp.broadcast_to(h0_ref[0], (SUBLANES, A_WIDTH))

    xa = xa_ref[...]
    xbuf[8:8 + tt, :] = xa
    cw = cw_ref[...]
    xc = jnp.broadcast_to(cb_ref[...], (tt, A_WIDTH))
    for j in range(CONV_WIDTH):
        xc = xc + xbuf[5 + j:5 + j + tt, :] * cw[j:j + 1, :]
    xbuf[5:8, :] = xa[tt - 3:tt, :]

    gates = jnp.dot(xc.astype(BF16), wri_ref[...], preferred_element_type=F32) + bri_ref[...]
    r = _sigmoid(gates[:, :A_WIDTH])
    ig = _sigmoid(gates[:, A_WIDTH:])
    log_a = (-RG_C) * r * _softplus(-lam_ref[...])
    a = jnp.exp(log_a)
    abuf[...] = a
    bbuf[...] = jnp.sqrt(-jnp.tanh(log_a) * (a * a + 1.0)) * (ig * xc)

    rows = _row_iota((SUBLANES, A_WIDTH))

    def group(gi, carry):
        r0 = pl.multiple_of(gi * SUBLANES, SUBLANES)
        a = abuf[pl.ds(r0, SUBLANES), :]
        b = bbuf[pl.ds(r0, SUBLANES), :]
        for d in (1, 2, 4):
            a_sh = pltpu.roll(a, d, 0)
            b_sh = pltpu.roll(b, d, 0)
            keep = rows >= d
            b = jnp.where(keep, a * b_sh + b, b)
            a = jnp.where(keep, a * a_sh, a)
        h = a * hcar[...] + b
        hbuf[pl.ds(r0, SUBLANES), :] = h
        hcar[...] = jnp.broadcast_to(h[SUBLANES - 1:SUBLANES, :], (SUBLANES, A_WIDTH))
        return carry

    lax.fori_loop(0, tt // SUBLANES, group, 0)

    ya_ref[...] = hbuf[...] * _silu(ga_ref[...])

    @pl.when(ti == nt - 1)
    def _():
        convn_ref[0] = xa[tt - 3:tt, :]
        hl_ref[0] = hcar[0:1, :]


def _rglru(u, conv0, h0, cw, cb, wri, bri, lam, bsz, t, tt):
    nt = t // tt
    kern = functools.partial(_rglru_kernel, tt=tt)
    const = lambda b, i: (0, 0)
    return pl.pallas_call(
        kern,
        grid=(bsz, nt),
        in_specs=[
            pl.BlockSpec((tt, A_WIDTH), lambda b, i: (b * nt + i, U_XA // A_WIDTH)),
            pl.BlockSpec((tt, A_WIDTH), lambda b, i: (b * nt + i, U_GA // A_WIDTH)),
            pl.BlockSpec((1, 3, A_WIDTH), lambda b, i: (b, 0, 0)),
            pl.BlockSpec((1, 1, A_WIDTH), lambda b, i: (b, 0, 0)),
            pl.BlockSpec((CONV_WIDTH, A_WIDTH), const),
            pl.BlockSpec((1, A_WIDTH), const),
            pl.BlockSpec((A_WIDTH, 2 * A_WIDTH), const),
            pl.BlockSpec((1, 2 * A_WIDTH), const),
            pl.BlockSpec((1, A_WIDTH), const),
        ],
        out_specs=[
            pl.BlockSpec((tt, A_WIDTH), lambda b, i: (b * nt + i, 0)),
            pl.BlockSpec((1, 3, A_WIDTH), lambda b, i: (b, 0, 0)),
            pl.BlockSpec((1, 1, A_WIDTH), lambda b, i: (b, 0, 0)),
        ],
        out_shape=[
            jax.ShapeDtypeStruct((bsz * t, A_WIDTH), F32),
            jax.ShapeDtypeStruct((bsz, 3, A_WIDTH), F32),
            jax.ShapeDtypeStruct((bsz, 1, A_WIDTH), F32),
        ],
        scratch_shapes=[
            pltpu.VMEM((tt + 8, A_WIDTH), F32),
            pltpu.VMEM((tt, A_WIDTH), F32),
            pltpu.VMEM((tt, A_WIDTH), F32),
            pltpu.VMEM((tt, A_WIDTH), F32),
            pltpu.VMEM((SUBLANES, A_WIDTH), F32),
        ],
        compiler_params=_cparams(("parallel", "arbitrary")),
        name="rglru",
    )(u, u, conv0, h0, cw, cb, wri, bri, lam)


def _kmean_kernel(k_ref, o_ref, *, nblk):
    k = k_ref[...].reshape(nblk, MOBA_BLOCK, 256)
    o_ref[...] = jnp.sum(k, axis=1) * (1.0 / MOBA_BLOCK)


def _kmean(k2d, nblk):
    m = k2d.shape[0]
    rows = nblk * MOBA_BLOCK
    return pl.pallas_call(
        functools.partial(_kmean_kernel, nblk=nblk),
        grid=(m // rows,),
        in_specs=[pl.BlockSpec((rows, 256), lambda i: (i, 0))],
        out_specs=pl.BlockSpec((nblk, 256), lambda i: (i, 0)),
        out_shape=jax.ShapeDtypeStruct((m // MOBA_BLOCK, 256), F32),
        compiler_params=_cparams(("parallel",)),
        name="kmean",
    )(k2d)


def _select_kernel(q_ref, k_ref, v_ref, w_ref, shift_ref, qa_ref, ka_ref, va_ref, *, ts, rs):
    tile0 = pl.program_id(1) * ts
    w = w_ref[0]
    neg_shift = -shift_ref[...]

    def sub_tile(si, carry):
        r0 = pl.multiple_of(si * rs, rs)
        t0 = tile0 + r0
        q = q_ref[pl.ds(r0, rs), :]
        sc = _dot_nt(w, q, precision=HIGHEST)
        shp = (B_HEADS * NB_MAX, rs)
        j = _row_iota(shp) & (NB_MAX - 1)
        n_full = (t0 + _lane_iota(shp)) >> 8
        valid = j < n_full
        scm = jnp.where(valid, sc, -jnp.inf)
        ranks = []
        for h in range(B_HEADS):
            grp = scm[h * NB_MAX:(h + 1) * NB_MAX]
            jg = j[h * NB_MAX:(h + 1) * NB_MAX]
            rank = jnp.zeros((NB_MAX, rs), jnp.int32)
            for jp in range(NB_MAX):
                p = jnp.broadcast_to(grp[jp:jp + 1, :], (NB_MAX, rs))
                rank = rank + jnp.where(jg > jp, jnp.where(p >= grp, 1, 0), jnp.where(p > grp, 1, 0))
            ranks.append(rank)
        rank = jnp.concatenate(ranks, axis=0)
        allowed = (valid & (rank < MOBA_TOPK)) | (j == n_full)
        bias = jnp.where(allowed, 0.0, NEG).T

        shp1 = (rs, LANES)
        lane = _lane_iota(shp1)
        pos1 = t0 + _row_iota(shp1)
        pos_hi = (pos1 >> 6).astype(F32)
        pos_lo = (pos1 & 63).astype(F32)
        blk_onehot = jnp.where((lane - 64) == (pos1 >> 8), 1.0, 0.0)
        kfeat = jnp.where(lane == 96, pos_hi, jnp.where(lane == 97, pos_lo,
                          jnp.where((lane >= 98) & (lane <= 100), 1.0, 0.0)))
        kextra = jnp.where(lane < 96, blk_onehot, kfeat)
        vextra = jnp.where(lane == 64, 1.0, 0.0)
        k = k_ref[pl.ds(r0, rs), :]
        v = v_ref[pl.ds(r0, rs), :]
        for g in range(B_KV_HEADS):
            kt = k[:, (g // 2) * LANES:(g // 2 + 1) * LANES]
            vt = v[:, (g // 2) * LANES:(g // 2 + 1) * LANES]
            if g % 2:
                kt = pltpu.roll(kt, 64, 1)
                vt = pltpu.roll(vt, 64, 1)
            ka_ref[0, g, pl.ds(r0, rs), :] = jnp.where(lane < 64, kt, kextra).astype(BF16)
            va_ref[0, g, pl.ds(r0, rs), :] = jnp.where(lane < 64, vt, vextra).astype(BF16)
            qt = q[:, g * LANES:(g + 1) * LANES]
            for h2 in range(2):
                h = 2 * g + h2
                base = qt if h2 == 0 else pltpu.roll(qt, 64, 1)
                bt = bias[:, (h // 4) * LANES:(h // 4 + 1) * LANES]
                sh = (64 - (h % 4) * NB_MAX) % LANES
                if sh:
                    bt = pltpu.roll(bt, sh, 1)
                slope = 2.0 ** (-(h + 1))
                qfeat = jnp.where(lane == 96, slope * 64.0, jnp.where(lane == 97, slope,
                                  jnp.where(lane == 98, (-slope * 64.0) * pos_hi,
                                            jnp.where(lane == 99, (-slope) * pos_lo,
                                                      jnp.where(lane == 100, neg_shift, 0.0)))))
                aug = jnp.where(lane < 64, base, jnp.where(lane < 96, bt, qfeat))
                qa_ref[0, h, pl.ds(r0, rs), :] = aug.astype(BF16)
        return carry

    lax.fori_loop(0, ts // rs, sub_tile, 0)


def _select(u, k2d, v2d, wsel, shift, bsz, s, ts):
    nt = s // ts
    return pl.pallas_call(
        functools.partial(_select_kernel, ts=ts, rs=min(ts, LANES)),
        grid=(bsz, nt),
        in_specs=[
            pl.BlockSpec((ts, 512), lambda b, i: (b * nt + i, U_QB // 512)),
            pl.BlockSpec((ts, 256), lambda b, i: (b * nt + i, 0)),
            pl.BlockSpec((ts, 256), lambda b, i: (b * nt + i, 0)),
            pl.BlockSpec((1, B_HEADS * NB_MAX, 512), lambda b, i: (b, 0, 0)),
            pl.BlockSpec((1, LANES), lambda b, i: (0, 0)),
        ],
        out_specs=[
            pl.BlockSpec((1, B_HEADS, ts, LANES), lambda b, i: (b, 0, i, 0)),
            pl.BlockSpec((1, B_KV_HEADS, ts, LANES), lambda b, i: (b, 0, i, 0)),
            pl.BlockSpec((1, B_KV_HEADS, ts, LANES), lambda b, i: (b, 0, i, 0)),
        ],
        out_shape=[
            jax.ShapeDtypeStruct((bsz, B_HEADS, s, LANES), BF16),
            jax.ShapeDtypeStruct((bsz, B_KV_HEADS, s, LANES), BF16),
            jax.ShapeDtypeStruct((bsz, B_KV_HEADS, s, LANES), BF16),
        ],
        compiler_params=_cparams(("parallel", "parallel")),
        name="moba_select",
    )(u, k2d, v2d, wsel, shift)


ATTN_KV_BLOCKS = 4
SHIFT_LIMIT = 30.0


def _attn_finish(o_ref, acc_sc, tq):
    acc = acc_sc[...]
    res = acc / acc[:, HEAD_DIM:HEAD_DIM + 1]
    lane = _lane_iota((tq, LANES))
    o_ref[...] = jnp.where(lane < 64, res[:tq], pltpu.roll(res[tq:], 64, 1))


def _attn_online(qa_ref, ka_ref, va_ref, o_ref, m_sc, acc_sc, tq):
    qi = pl.program_id(2)
    rows = 2 * tq
    q = qa_ref[0].reshape(rows, LANES)

    k0 = ka_ref[0, 0, pl.ds(pl.multiple_of(qi * tq, tq), tq), :]
    v0 = va_ref[0, 0, pl.ds(pl.multiple_of(qi * tq, tq), tq), :]
    s = _dot_nt(q, k0)
    shp = (rows, tq)
    qpos = _row_iota(shp) & (tq - 1)
    s = jnp.where(_lane_iota(shp) <= qpos, s, NEG)
    m0 = jnp.max(s, axis=-1, keepdims=True)
    p = jnp.exp(s - m0)
    m_sc[...] = jnp.broadcast_to(m0, (rows, LANES))
    acc_sc[...] = jnp.dot(p.astype(BF16), v0, preferred_element_type=F32)

    def body(jb, carry):
        r0 = pl.multiple_of(jb * tq, tq)
        kj = ka_ref[0, 0, pl.ds(r0, tq), :]
        vj = va_ref[0, 0, pl.ds(r0, tq), :]
        sj = _dot_nt(q, kj)
        m_prev = m_sc[...]
        m_new = jnp.maximum(m_prev, jnp.max(sj, axis=-1, keepdims=True))
        alpha = jnp.exp(m_prev - m_new)
        pj = jnp.exp(sj - m_new[:, 0:1])
        acc_sc[...] = alpha * acc_sc[...] + jnp.dot(pj.astype(BF16), vj, preferred_element_type=F32)
        m_sc[...] = m_new
        return carry

    lax.fori_loop(0, qi, body, 0)
    _attn_finish(o_ref, acc_sc, tq)


def _attn_shifted(qa_ref, ka_ref, va_ref, o_ref, acc_sc, tq):
    qi = pl.program_id(2)
    rows = 2 * tq
    keys = ATTN_KV_BLOCKS * tq
    q = qa_ref[0].reshape(rows, LANES)
    shp = (rows, keys)
    qpos = qi * tq + (_row_iota(shp) & (tq - 1))
    kofs = _lane_iota(shp)
    acc_sc[...] = jnp.zeros((rows, LANES), F32)

    def body(jp, carry):
        r0 = pl.multiple_of(jp * keys, keys)
        kj = ka_ref[0, 0, pl.ds(r0, keys), :]
        vj = va_ref[0, 0, pl.ds(r0, keys), :]
        s = _dot_nt(q, kj)
        p = jnp.exp(jnp.where(r0 + kofs <= qpos, s, NEG))
        acc_sc[...] += jnp.dot(p.astype(BF16), vj, preferred_element_type=F32)
        return carry

    lax.fori_loop(0, (qi + ATTN_KV_BLOCKS) // ATTN_KV_BLOCKS, body, 0)
    _attn_finish(o_ref, acc_sc, tq)


def _attn_kernel(mode_ref, qa_ref, ka_ref, va_ref, o_ref, m_sc, acc_sc, *, tq):
    @pl.when(mode_ref[0] == 1)
    def _():
        _attn_shifted(qa_ref, ka_ref, va_ref, o_ref, acc_sc, tq)

    @pl.when(mode_ref[0] != 1)
    def _():
        _attn_online(qa_ref, ka_ref, va_ref, o_ref, m_sc, acc_sc, tq)


def _attn(mode, qa, ka, va, bsz, s):
    tq = MOBA_BLOCK
    nq = s // tq
    assert nq % ATTN_KV_BLOCKS == 0
    grid_spec = pltpu.PrefetchScalarGridSpec(
        num_scalar_prefetch=1,
        grid=(bsz, B_KV_HEADS, nq),
        in_specs=[
            pl.BlockSpec((1, 2, tq, LANES), lambda b, g, i, mode: (b, g, i, 0)),
            pl.BlockSpec((1, 1, s, LANES), lambda b, g, i, mode: (b, g, 0, 0)),
            pl.BlockSpec((1, 1, s, LANES), lambda b, g, i, mode: (b, g, 0, 0)),
        ],
        out_specs=pl.BlockSpec((tq, LANES), lambda b, g, i, mode: (b * nq + i, g)),
        scratch_shapes=[
            pltpu.VMEM((2 * tq, LANES), F32),
            pltpu.VMEM((2 * tq, LANES), F32),
        ],
    )
    return pl.pallas_call(
        functools.partial(_attn_kernel, tq=tq),
        grid_spec=grid_spec,
        out_shape=jax.ShapeDtypeStruct((bsz * s, 512), F32),
        compiler_params=_cparams(("parallel", "parallel", "arbitrary")),
        name="moba_attn",
    )(mode, qa, ka, va)


PAGES_PER_STEP = 16


def _page_copies(pt_ref, src_hbm, buf, sem, b, c, slot):
    return [pltpu.make_async_copy(src_hbm.at[pt_ref[b, c * PAGES_PER_STEP + p]], buf.at[slot, p], sem.at[slot])
            for p in range(PAGES_PER_STEP)]


def _skmean_kernel(pt_ref, ck_hbm, o_ref, kbuf, sem, *, nch):
    b = pl.program_id(0)
    blocks = PAGES_PER_STEP * PAGE_SIZE // MOBA_BLOCK
    for cp in _page_copies(pt_ref, ck_hbm, kbuf, sem, b, 0, 0):
        cp.start()
    for c in range(nch):
        slot = c % 2
        if c + 1 < nch:
            for cp in _page_copies(pt_ref, ck_hbm, kbuf, sem, b, c + 1, 1 - slot):
                cp.start()
        for cp in _page_copies(pt_ref, ck_hbm, kbuf, sem, b, c, slot):
            cp.wait()
        page_sums = jnp.sum(kbuf[slot], axis=-1)
        ppb = MOBA_BLOCK // PAGE_SIZE
        o_ref[0, c * blocks:(c + 1) * blocks, :] = (
            jnp.sum(page_sums.reshape(blocks, ppb, 256), axis=1) * (1.0 / MOBA_BLOCK))


def _skmean(pt, ck, bsz, n_pages):
    nch = n_pages // PAGES_PER_STEP
    nblk = n_pages * PAGE_SIZE // MOBA_BLOCK
    grid_spec = pltpu.PrefetchScalarGridSpec(
        num_scalar_prefetch=1,
        grid=(bsz,),
        in_specs=[pl.BlockSpec(memory_space=pl.ANY)],
        out_specs=pl.BlockSpec((1, nblk, 256), lambda b, pt: (b, 0, 0)),
        scratch_shapes=[
            pltpu.VMEM((2, PAGES_PER_STEP, 256, PAGE_SIZE), F32),
            pltpu.SemaphoreType.DMA((2,)),
        ],
    )
    return pl.pallas_call(
        functools.partial(_skmean_kernel, nch=nch),
        grid_spec=grid_spec,
        out_shape=jax.ShapeDtypeStruct((bsz, nblk, 256), F32),
        compiler_params=_cparams(("arbitrary",)),
        name="sample_kmean",
    )(pt, ck)


def _sattn_kernel(pt_ref, q_ref, km_ref, kn_ref, vn_ref, slope_ref, ck_hbm, cv_hbm, o_ref,
                  kbuf, vbuf, ksem, vsem, *, nch, nblk, tnew, past):
    b = pl.program_id(0)
    rows = 64
    qf = q_ref[0]
    qb = qf.astype(BF16)
    slope = slope_ref[...]
    shp = (rows, LANES)
    lane = _lane_iota(shp)
    tok = _row_iota(shp) & (tnew - 1)

    def start(c, slot):
        for cp in _page_copies(pt_ref, ck_hbm, kbuf, ksem, b, c, slot):
            cp.start()
        for cp in _page_copies(pt_ref, cv_hbm, vbuf, vsem, b, c, slot):
            cp.start()

    def wait(c, slot):
        for cp in _page_copies(pt_ref, ck_hbm, kbuf, ksem, b, c, slot):
            cp.wait()
        for cp in _page_copies(pt_ref, cv_hbm, vbuf, vsem, b, c, slot):
            cp.wait()

    start(0, 0)

    km = km_ref[0]
    km2 = jnp.concatenate([km, km], axis=0)
    sc = _dot_nt(qf, km2, precision=HIGHEST)
    j = lane & (nblk - 1)
    rank = jnp.zeros(shp, jnp.int32)
    for s in range(1, nblk):
        a = pltpu.roll(sc, 2 * nblk - s, 1)
        wrapped = j >= nblk - s
        rank = rank + jnp.where(wrapped, jnp.where(a >= sc, 1, 0), jnp.where(a > sc, 1, 0))
    bias = jnp.where(rank < MOBA_TOPK, 0.0, NEG)

    s_own = _dot_nt(qb, kn_ref[0].astype(BF16))
    s_own = s_own - slope * (tok - lane).astype(F32)
    s_own = jnp.where((lane <= tok) & (lane < tnew), s_own, NEG)
    m = jnp.max(s_own, axis=-1, keepdims=True)
    p = jnp.exp(s_own - m)
    l = jnp.sum(p, axis=-1, keepdims=True)
    acc = jnp.dot(p.astype(BF16), vn_ref[0].astype(BF16), preferred_element_type=F32)

    keys = PAGES_PER_STEP * PAGE_SIZE
    blocks = keys // MOBA_BLOCK
    kshape = (rows, keys)
    klane = _lane_iota(kshape)
    qpos = past + (_row_iota(kshape) & (tnew - 1))
    slope_k = jnp.broadcast_to(slope[:, 0:1], kshape)
    for c in range(nch):
        slot = c % 2
        if c + 1 < nch:
            start(c + 1, 1 - slot)
        wait(c, slot)
        kc = kbuf[slot].astype(BF16)
        vc = vbuf[slot].astype(BF16)
        s_c = jnp.concatenate([jnp.dot(qb, kc[pg], preferred_element_type=F32)
                               for pg in range(PAGES_PER_STEP)], axis=1)
        s_c = s_c - slope_k * (qpos - (c * keys + klane)).astype(F32)
        sel = jnp.concatenate(
            [jnp.broadcast_to(bias[:, c * blocks + jj:c * blocks + jj + 1], (rows, MOBA_BLOCK))
             for jj in range(blocks)], axis=1)
        s_c = s_c + sel
        m_new = jnp.maximum(m, jnp.max(s_c, axis=-1, keepdims=True))
        alpha = jnp.exp(m - m_new)
        p = jnp.exp(s_c - m_new)
        l = alpha * l + jnp.sum(p, axis=-1, keepdims=True)
        pb = p.astype(BF16)
        pv = _dot_nt(pb[:, 0:PAGE_SIZE], vc[0])
        for pg in range(1, PAGES_PER_STEP):
            pv = pv + _dot_nt(pb[:, pg * PAGE_SIZE:(pg + 1) * PAGE_SIZE], vc[pg])
        acc = alpha * acc + pv
        m = m_new

    res = acc / l
    oshape = (rows, 256)
    own = (_lane_iota(oshape) >> 6) == (_row_iota(oshape) >> 4)
    res = jnp.where(own, res, 0.0)
    folded = res + pltpu.roll(res, 64, 1) + pltpu.roll(res, 128, 1) + pltpu.roll(res, 192, 1)
    o_ref[0] = folded[:, :HEAD_DIM]


def _sattn(pt, qbd, kmean, knew, vnew, slopes, ck, cv, bsz, n_pages, tnew):
    nch = n_pages // PAGES_PER_STEP
    nblk = n_pages * PAGE_SIZE // MOBA_BLOCK
    past = n_pages * PAGE_SIZE
    grid_spec = pltpu.PrefetchScalarGridSpec(
        num_scalar_prefetch=1,
        grid=(bsz,),
        in_specs=[
            pl.BlockSpec((1, 64, 256), lambda b, pt: (b, 0, 0)),
            pl.BlockSpec((1, nblk, 256), lambda b, pt: (b, 0, 0)),
            pl.BlockSpec((1, LANES, 256), lambda b, pt: (b, 0, 0)),
            pl.BlockSpec((1, LANES, 256), lambda b, pt: (b, 0, 0)),
            pl.BlockSpec((64, LANES), lambda b, pt: (0, 0)),
            pl.BlockSpec(memory_space=pl.ANY),
            pl.BlockSpec(memory_space=pl.ANY),
        ],
        out_specs=pl.BlockSpec((1, 64, HEAD_DIM), lambda b, pt: (b, 0, 0)),
        scratch_shapes=[
            pltpu.VMEM((2, PAGES_PER_STEP, 256, PAGE_SIZE), F32),
            pltpu.VMEM((2, PAGES_PER_STEP, 256, PAGE_SIZE), F32),
            pltpu.SemaphoreType.DMA((2,)),
            pltpu.SemaphoreType.DMA((2,)),
        ],
    )
    return pl.pallas_call(
        functools.partial(_sattn_kernel, nch=nch, nblk=nblk, tnew=tnew, past=past),
        grid_spec=grid_spec,
        out_shape=jax.ShapeDtypeStruct((bsz, 64, HEAD_DIM), F32),
        compiler_params=_cparams(("arbitrary",)),
        name="sample_attn",
    )(pt, qbd, kmean, knew, vnew, slopes, ck, cv)


def _mlstm_kernel(q_ref, k_ref, v_ref, if_ref, o_ref, g_ref, bias_ref, og_ref, c0_ref, m0_ref,
                  y_ref, cout_ref, mout_ref, c_sc, m_sc, *, tc, L):
    ti = pl.program_id(1)
    nt = pl.num_programs(1)

    @pl.when(ti == 0)
    def _():
        c_sc[...] = c0_ref[0]
        m_sc[...] = m0_ref[0]

    gates = if_ref[...] + bias_ref[...]
    lf = jnp.minimum(gates, 0.0) - jnp.log1p(jnp.exp(-jnp.abs(gates)))
    tri = _lane_iota((L, L)) <= _row_iota((L, L))
    tri_f = jnp.where(tri, 1.0, 0.0)
    lane = _lane_iota((L, LANES))
    og = og_ref[...]
    for c in range(tc // L):
        r0 = c * L
        fcum = jnp.dot(tri_f, lf[r0:r0 + L], preferred_element_type=F32, precision=HIGHEST)
        z = gates[r0:r0 + L] - pltpu.roll(fcum, LANES - 4, 1)
        qt = q_ref[r0:r0 + L, :]
        kt = k_ref[r0:r0 + L, :] * (HEAD_DIM ** -0.5)
        vt = v_ref[r0:r0 + L, :]
        ot = _sigmoid(o_ref[r0:r0 + L, :])
        for pair in range(C_HEADS // 2):
            ytile = jnp.zeros((L, LANES), F32)
            for h2 in range(2):
                h = 2 * pair + h2
                sl = slice(pair * LANES, (pair + 1) * LANES)
                qh, kh, vh, oh = qt[:, sl], kt[:, sl], vt[:, sl], ot[:, sl]
                if h2:
                    qh, kh, vh, oh = [pltpu.roll(a, 64, 1) for a in (qh, kh, vh, oh)]
                qh = jnp.where(lane < 64, qh, 0.0).astype(BF16)
                kh = jnp.where(lane < 64, kh, 0.0).astype(BF16)
                vaug = jnp.where(lane < 64, vh, jnp.where(lane == 64, 1.0, 0.0))
                m_prev = m_sc[h:h + 1, 0:1]
                f_col = fcum[:, 4 + h:5 + h]
                z_col = z[:, h:h + 1]
                pick = jnp.where(lane == h, 1.0, 0.0)
                z_row = _dot_nt(pick, z, precision=HIGHEST)
                d_log = jnp.where(tri, f_col + z_row, -jnp.inf)
                inter = f_col + m_prev
                m_t = jnp.maximum(jnp.max(d_log, axis=-1, keepdims=True), inter)
                w_intra = _dot_nt(qh, kh) * jnp.exp(d_log - m_t)
                w_inter = jnp.exp(inter - m_t)
                c_prev = c_sc[h]
                num = (w_inter * jnp.dot(qh, c_prev.astype(BF16), preferred_element_type=F32)
                       + jnp.dot(w_intra.astype(BF16), vaug.astype(BF16), preferred_element_type=F32))
                den = num[:, HEAD_DIM:HEAD_DIM + 1]
                hh = num / jnp.maximum(jnp.abs(den), jnp.exp(-m_t))
                f_tot = f_col[L - 1:L, :]
                w_log = f_tot + z_col
                m_new = jnp.maximum(f_tot + m_prev, jnp.max(w_log, axis=0, keepdims=True))
                decay = jnp.exp(f_tot + m_prev - m_new)
                w = jnp.exp(w_log - m_new)
                upd = lax.dot_general(kh, (w * vaug).astype(BF16), (((0,), (0,)), ((), ())),
                                      preferred_element_type=F32)
                c_sc[h] = decay * c_prev + upd
                m_sc[h:h + 1, :] = jnp.broadcast_to(m_new, (1, LANES))
                ho = jnp.where(lane < 64, hh * oh, 0.0)
                ms = jnp.sum(ho * ho, axis=-1, keepdims=True) * (1.0 / HEAD_DIM)
                yh = ho * lax.rsqrt(ms + NORM_EPS)
                ytile = ytile + (pltpu.roll(yh, 64, 1) if h2 else yh)
            sl = slice(pair * LANES, (pair + 1) * LANES)
            y_ref[r0:r0 + L, sl] = ytile * og[:, sl] * _silu(g_ref[r0:r0 + L, sl])

    @pl.when(ti == nt - 1)
    def _():
        cout_ref[0] = c_sc[...]
        mout_ref[0] = m_sc[...]


def _mlstm(u, gate_bias, onorm_g, c0aug, m0, bsz, t, tc, L, wide):
    nt = t // tc
    const = lambda b, i: (0, 0)
    ucol = lambda off, w: (lambda b, i: (b * nt + i, off // w))
    return pl.pallas_call(
        functools.partial(_mlstm_wide_kernel if wide else _mlstm_kernel, tc=tc, L=L),
        grid=(bsz, nt),
        in_specs=[
            pl.BlockSpec((tc, 256), ucol(U_CQ, 256)),
            pl.BlockSpec((tc, 256), ucol(U_CK, 256)),
            pl.BlockSpec((tc, 256), ucol(U_CV, 256)),
            pl.BlockSpec((tc, LANES), ucol(U_CIF, LANES)),
            pl.BlockSpec((tc, 256), ucol(U_CO, 256)),
            pl.BlockSpec((tc, 256), ucol(U_CG, 256)),
            pl.BlockSpec((1, LANES), const),
            pl.BlockSpec((1, 256), const),
            pl.BlockSpec((1, C_HEADS, LANES, LANES), lambda b, i: (b, 0, 0, 0)),
            pl.BlockSpec((1, SUBLANES, LANES), lambda b, i: (b, 0, 0)),
        ],
        out_specs=[
            pl.BlockSpec((tc, 256), lambda b, i: (b * nt + i, 0)),
            pl.BlockSpec((1, C_HEADS, LANES, LANES), lambda b, i: (b, 0, 0, 0)),
            pl.BlockSpec((1, SUBLANES, LANES), lambda b, i: (b, 0, 0)),
        ],
        out_shape=[
            jax.ShapeDtypeStruct((bsz * t, 256), F32),
            jax.ShapeDtypeStruct((bsz, C_HEADS, LANES, LANES), F32),
            jax.ShapeDtypeStruct((bsz, SUBLANES, LANES), F32),
        ],
        scratch_shapes=[
            pltpu.VMEM((C_HEADS, LANES, LANES), F32),
            pltpu.VMEM((SUBLANES, LANES), F32),
        ],
        compiler_params=_cparams(("parallel", "arbitrary")),
        name="mlstm",
    )(u, u, u, u, u, u, gate_bias, onorm_g, c0aug, m0)


def _mlstm_wide_kernel(q_ref, k_ref, v_ref, if_ref, o_ref, g_ref, bias_ref, og_ref, c0_ref, m0_ref,
                       y_ref, cout_ref, mout_ref, c_sc, m_sc, *, tc, L):
    ti = pl.program_id(1)
    nt = pl.num_programs(1)

    @pl.when(ti == 0)
    def _():
        c_sc[...] = c0_ref[0]
        m_sc[...] = m0_ref[0]

    gates = if_ref[...] + bias_ref[...]
    lf = jnp.minimum(gates, 0.0) - jnp.log1p(jnp.exp(-jnp.abs(gates)))
    in_chunk = _row_iota((tc, LANES)) & (L - 1)
    fcum = lf
    d = 1
    while d < L:
        fcum = fcum + jnp.where(in_chunk >= d, pltpu.roll(fcum, d, 0), 0.0)
        d *= 2
    z = gates - pltpu.roll(fcum, LANES - 4, 1)
    z_t = z.T
    k_t = (k_ref[...] * (HEAD_DIM ** -0.5)).T

    tri = _lane_iota((L, L)) <= _row_iota((L, L))
    lane = _lane_iota((L, LANES))
    srow = _row_iota((LANES, L))
    og = og_ref[...]
    for c in range(tc // L):
        r0 = c * L
        for pair in range(C_HEADS // 2):
            sl = slice(pair * LANES, (pair + 1) * LANES)
            qt = q_ref[r0:r0 + L, sl]
            vt = v_ref[r0:r0 + L, sl]
            ot = _sigmoid(o_ref[r0:r0 + L, sl])
            ktt = k_t[pair * LANES:(pair + 1) * LANES, r0:r0 + L]
            ytile = jnp.zeros((L, LANES), F32)
            for h2 in range(2):
                h = 2 * pair + h2
                mine = (lane >= 64) if h2 else (lane < 64)
                ones_col = 0 if h2 else HEAD_DIM
                qh = jnp.where(mine, qt, 0.0).astype(BF16)
                kth = jnp.where((srow >= 64) if h2 else (srow < 64), ktt, 0.0).astype(BF16)
                vaug = jnp.where(mine, vt, jnp.where(lane == ones_col, 1.0, 0.0))
                m_prev = m_sc[h:h + 1, 0:1]
                f_col = fcum[r0:r0 + L, 4 + h:5 + h]
                z_col = z[r0:r0 + L, h:h + 1]
                z_row = z_t[h:h + 1, r0:r0 + L]
                d_log = jnp.where(tri, f_col + z_row, -jnp.inf)
                inter = f_col + m_prev
                m_t = jnp.maximum(jnp.max(d_log, axis=-1, keepdims=True), inter)
                w_intra = jnp.dot(qh, kth, preferred_element_type=F32) * jnp.exp(d_log - m_t)
                w_inter = jnp.exp(inter - m_t)
                c_prev = c_sc[h]
                num = (w_inter * jnp.dot(qh, c_prev.astype(BF16), preferred_element_type=F32)
                       + jnp.dot(w_intra.astype(BF16), vaug.astype(BF16), preferred_element_type=F32))
                den = num[:, ones_col:ones_col + 1]
                hh = num / jnp.maximum(jnp.abs(den), jnp.exp(-m_t))
                f_tot = f_col[L - 1:L, :]
                w_log = f_tot + z_col
                m_new = jnp.maximum(f_tot + m_prev, jnp.max(w_log, axis=0, keepdims=True))
                decay = jnp.exp(f_tot + m_prev - m_new)
                w = jnp.exp(w_log - m_new)
                c_sc[h] = decay * c_prev + jnp.dot(kth, (w * vaug).astype(BF16), preferred_element_type=F32)
                m_sc[h:h + 1, :] = jnp.broadcast_to(m_new, (1, LANES))
                ho = jnp.where(mine, hh * ot, 0.0)
                ms = jnp.sum(ho * ho, axis=-1, keepdims=True) * (1.0 / HEAD_DIM)
                ytile = ytile + ho * lax.rsqrt(ms + NORM_EPS)
            y_ref[r0:r0 + L, sl] = ytile * og[:, sl] * _silu(g_ref[r0:r0 + L, sl])

    @pl.when(ti == nt - 1)
    def _():
        cout_ref[0] = c_sc[...]
        mout_ref[0] = m_sc[...]


def _outproj_kernel(x_ref, ya_ref, yb_ref, gb_ref, yc_ref, p_ref, wa_ref, wb_ref, wc_ref,
                    pg_ref, pp_ref, png_ref, o_ref):
    yb = yb_ref[...] * _silu(gb_ref[...])
    x = (x_ref[...]
         + jnp.dot(ya_ref[...].astype(BF16), wa_ref[...], preferred_element_type=F32)
         + jnp.dot(yb.astype(BF16), wb_ref[...], preferred_element_type=F32)
         + jnp.dot(yc_ref[...].astype(BF16), wc_ref[...], preferred_element_type=F32))
    gate = _sigmoid(jnp.dot(x.astype(BF16), pg_ref[...], preferred_element_type=F32))
    e = jnp.dot(p_ref[...].astype(BF16), pp_ref[...], preferred_element_type=F32)
    e = e * lax.rsqrt(jnp.mean(e * e, axis=-1, keepdims=True) + NORM_EPS) * png_ref[...]
    o_ref[...] = x + gate * e


def _outproj(x2d, ya, yb, u, yc, p2d, wa, wb, wc, pg, pp, png, tm):
    m = x2d.shape[0]
    row = lambda i: (i, 0)
    const = lambda i: (0, 0)
    return pl.pallas_call(
        _outproj_kernel,
        grid=(m // tm,),
        in_specs=[
            pl.BlockSpec((tm, D_MODEL), row),
            pl.BlockSpec((tm, 256), row),
            pl.BlockSpec((tm, 512), row),
            pl.BlockSpec((tm, 512), lambda i: (i, U_GB // 512)),
            pl.BlockSpec((tm, 256), row),
            pl.BlockSpec((tm, PLE_DIM), row),
            pl.BlockSpec((256, D_MODEL), const),
            pl.BlockSpec((512, D_MODEL), const),
            pl.BlockSpec((256, D_MODEL), const),
            pl.BlockSpec((D_MODEL, D_MODEL), const),
            pl.BlockSpec((PLE_DIM, D_MODEL), const),
            pl.BlockSpec((1, D_MODEL), const),
        ],
        out_specs=pl.BlockSpec((tm, D_MODEL), row),
        out_shape=jax.ShapeDtypeStruct((m, D_MODEL), F32),
        compiler_params=_cparams(("parallel",)),
        name="outproj",
    )(x2d, ya, yb, u, yc, p2d, wa, wb, wc, pg, pp, png)


def _block_diag(w):
    n, a, b = w.shape
    eye = jnp.eye(n, dtype=w.dtype)
    return jnp.einsum("nab,nm->namb", w, eye).reshape(n * a, n * b)


def _prep_layer(l, w):
    (norm_g, w_in, a_conv_w, a_conv_b, a_w_r, a_b_r, a_w_i, a_b_i, a_lambda, b_qnorm_g, b_knorm_g,
     c_b_i, c_b_f, c_onorm_g, w_out, ple_gate, ple_proj, ple_norm_g) = [a[l] for a in w]
    col = lambda i: w_in[:, _REF_OFFS[i]:_REF_OFFS[i + 1]]
    w_u = jnp.concatenate(
        [col(0), col(1), col(2), col(5), col(3), col(4), col(6), col(7), col(8), col(11), col(12),
         col(9), col(10), jnp.zeros((D_MODEL, LANES - 2 * C_HEADS), F32)], axis=1).astype(BF16)
    gate_bias = jnp.concatenate([c_b_i, c_b_f, jnp.zeros((LANES - 2 * C_HEADS,), F32)])[None, :]
    shift = 1.02 * HEAD_DIM ** 0.5 * jnp.max(jnp.abs(b_qnorm_g)) * jnp.max(jnp.abs(b_knorm_g)) + 0.5
    return dict(
        shift=shift,
        norm_g=norm_g[None, :], w_u=w_u,
        qg=jnp.tile(b_qnorm_g, B_HEADS)[None, :], kg=jnp.tile(b_knorm_g, B_KV_HEADS)[None, :],
        bd=_block_diag(jnp.ones((B_HEADS, HEAD_DIM, HEAD_DIM), BF16)),
        cw=a_conv_w, cb=a_conv_b[None, :],
        wri=jnp.concatenate([_block_diag(a_w_r), _block_diag(a_w_i)], axis=1).astype(BF16),
        bri=jnp.concatenate([a_b_r, a_b_i])[None, :], lam=a_lambda[None, :],
        gate_bias=gate_bias, onorm_g=c_onorm_g[None, :],
        wa=w_out[:256].astype(BF16), wb=w_out[256:768].astype(BF16), wc=w_out[768:].astype(BF16),
        pg=ple_gate.astype(BF16), pp=ple_proj.astype(BF16), png=ple_norm_g[None, :],
    )


def _pick_tile(n, cap):
    t = min(n, cap)
    while n % t:
        t //= 2
    return t


def _layer(x2d, p2d, bsz, t, conv0, h0, c0, n0, m0, lw, attend):
    m = bsz * t
    tm = _pick_tile(m, 256)
    u, k2d, v2d = _inproj(x2d, lw["norm_g"], lw["w_u"], lw["qg"], lw["kg"], lw["bd"], tm)
    ya, conv_new, h_last = _rglru(u, conv0, h0[:, None, :], lw["cw"], lw["cb"], lw["wri"], lw["bri"],
                                  lw["lam"], bsz, t, _pick_tile(t, 512))
    yb = attend(u, k2d, v2d)
    wide = t % LANES == 0
    L = LANES if wide else int(np.gcd(t, MLSTM_CHUNK))
    lo, hi = slice(0, HEAD_DIM), slice(HEAD_DIM, 2 * HEAD_DIM)
    c0aug = jnp.zeros((bsz, C_HEADS, LANES, LANES), F32)
    if wide:
        c0aug = c0aug.at[:, 0::2, lo, lo].set(c0[:, 0::2]).at[:, 0::2, lo, HEAD_DIM].set(n0[:, 0::2])
        c0aug = c0aug.at[:, 1::2, hi, hi].set(c0[:, 1::2]).at[:, 1::2, hi, 0].set(n0[:, 1::2])
    else:
        c0aug = c0aug.at[:, :, lo, lo].set(c0).at[:, :, lo, HEAD_DIM].set(n0)
    m0b = jnp.zeros((bsz, SUBLANES, LANES), F32).at[:, :C_HEADS, :].set(
        jnp.broadcast_to(m0[:, :, None], (bsz, C_HEADS, LANES)))
    yc, caug, mout = _mlstm(u, lw["gate_bias"], lw["onorm_g"], c0aug, m0b, bsz, t, _pick_tile(t, 512), L, wide)
    if wide:
        c_new = jnp.stack([caug[:, 0, lo, lo], caug[:, 1, hi, hi], caug[:, 2, lo, lo], caug[:, 3, hi, hi]], axis=1)
        n_new = jnp.stack([caug[:, 0, lo, HEAD_DIM], caug[:, 1, hi, 0],
                           caug[:, 2, lo, HEAD_DIM], caug[:, 3, hi, 0]], axis=1)
    else:
        c_new, n_new = caug[:, :, lo, lo], caug[:, :, lo, HEAD_DIM]
    xo = _outproj(x2d, ya, yb, u, yc, p2d, lw["wa"], lw["wb"], lw["wc"], lw["pg"], lw["pp"], lw["png"], tm)
    state = (k2d.reshape(bsz, t, B_KV_HEADS, HEAD_DIM), v2d.reshape(bsz, t, B_KV_HEADS, HEAD_DIM),
             conv_new, h_last[:, 0, :], c_new, n_new, mout[:, :C_HEADS, 0])
    return xo, state


def _attend_prompt(u, k2d, v2d, bsz, s, shift):
    nb = s // MOBA_BLOCK
    assert s % MOBA_BLOCK == 0 and nb <= NB_MAX
    kmean = _kmean(k2d, _pick_tile(bsz * nb, 8)).reshape(bsz, nb, B_KV_HEADS, HEAD_DIM)
    kmh = jnp.repeat(kmean, B_HEADS // B_KV_HEADS, axis=2).transpose(0, 2, 1, 3)
    kmh = jnp.pad(kmh, ((0, 0), (0, 0), (0, NB_MAX - nb), (0, 0)))
    wsel = jnp.einsum("bhjd,hg->bhjgd", kmh, jnp.eye(B_HEADS, dtype=F32)).reshape(bsz, B_HEADS * NB_MAX, 512)
    qa, ka, va = _select(u, k2d, v2d, wsel, jnp.full((1, LANES), shift, F32), bsz, s, _pick_tile(s, 512))
    mode = (shift <= SHIFT_LIMIT).astype(jnp.int32).reshape(1)
    return _attn(mode, qa, ka, va, bsz, s)


def _attend_sample(u, k2d, v2d, bsz, t, pt, ck, cv, slopes):
    n_pages = pt.shape[1]
    assert t == 8 and n_pages % PAGES_PER_STEP == 0
    nblk = n_pages * PAGE_SIZE // MOBA_BLOCK
    assert nblk == 64
    kmean = _skmean(pt, ck, bsz, n_pages)
    q = u[:, U_QB:U_QB + 512].reshape(bsz, t, B_KV_HEADS, 2, HEAD_DIM).transpose(0, 2, 3, 1, 4)
    qbd = jnp.einsum("bghtd,gk->bghtkd", q, jnp.eye(B_KV_HEADS, dtype=F32)).reshape(bsz, 64, 256)
    pad = ((0, 0), (0, LANES - t), (0, 0))
    knew = jnp.pad(k2d.reshape(bsz, t, 256), pad)
    vnew = jnp.pad(v2d.reshape(bsz, t, 256), pad)
    o = _sattn(pt, qbd, kmean, knew, vnew, slopes, ck, cv, bsz, n_pages, t)
    return o.reshape(bsz, B_HEADS, t, HEAD_DIM).transpose(0, 2, 1, 3).reshape(bsz * t, 512)


def kernel(x_prompt, x_sample, cache_k, cache_v, state_rglru_conv, state_rglru_h, state_mlstm_C, state_mlstm_n, state_mlstm_m, page_table, p_prompt, p_sample, norm_g, w_in, a_conv_w, a_conv_b, a_w_r, a_b_r, a_w_i, a_b_i, a_lambda, b_qnorm_g, b_knorm_g, c_b_i, c_b_f, c_onorm_g, w_out, ple_gate, ple_proj, ple_norm_g):
    weights = (norm_g, w_in, a_conv_w, a_conv_b, a_w_r, a_b_r, a_w_i, a_b_i, a_lambda, b_qnorm_g, b_knorm_g,
               c_b_i, c_b_f, c_onorm_g, w_out, ple_gate, ple_proj, ple_norm_g)
    depth = w_in.shape[0]
    bp, sp = x_prompt.shape[:2]
    bs, ts = x_sample.shape[:2]
    n_phys = cache_k.shape[1]
    ck = cache_k.transpose(0, 1, 3, 4, 2).reshape(depth * n_phys, 256, PAGE_SIZE)
    cv = cache_v.transpose(0, 1, 3, 4, 2).reshape(depth * n_phys, 256, PAGE_SIZE)
    slopes = jnp.asarray(np.repeat(2.0 ** -(np.arange(1, B_HEADS + 1)), ts)[:, None]
                         * np.ones((1, LANES)), F32)

    xp = x_prompt.reshape(bp * sp, D_MODEL)
    xs = x_sample.reshape(bs * ts, D_MODEL)
    zeros = lambda *s: jnp.zeros(s, F32)
    sp_all, ss_all = [], []
    for l in range(depth):
        lw = _prep_layer(l, weights)
        xp, st_p = _layer(xp, p_prompt[l].reshape(bp * sp, PLE_DIM), bp, sp,
                          zeros(bp, CONV_WIDTH - 1, A_WIDTH), zeros(bp, A_WIDTH),
                          zeros(bp, C_HEADS, HEAD_DIM, HEAD_DIM), zeros(bp, C_HEADS, HEAD_DIM), zeros(bp, C_HEADS),
                          lw, functools.partial(_attend_prompt, bsz=bp, s=sp, shift=lw["shift"]))
        attend_s = functools.partial(_attend_sample, bsz=bs, t=ts, pt=page_table + l * n_phys,
                                     ck=ck, cv=cv, slopes=slopes)
        xs, st_s = _layer(xs, p_sample[l].reshape(bs * ts, PLE_DIM), bs, ts,
                          state_rglru_conv[l], state_rglru_h[l], state_mlstm_C[l], state_mlstm_n[l],
                          state_mlstm_m[l], lw, attend_s)
        sp_all.append(st_p)
        ss_all.append(st_s)

    stk = lambda states, j: jnp.stack([s_[j] for s_ in states])
    return (xp.reshape(bp, sp, D_MODEL), xs.reshape(bs, ts, D_MODEL),
            stk(sp_all, 0), stk(sp_all, 1), stk(ss_all, 0), stk(ss_all, 1),
            stk(sp_all, 2), stk(ss_all, 2), stk(sp_all, 3), stk(ss_all, 3),
            stk(sp_all, 4), stk(ss_all, 4), stk(sp_all, 5), stk(ss_all, 5),
            stk(sp_all, 6), stk(ss_all, 6))
```

```python
import functools

import numpy as np
import jax
import jax.numpy as jnp
from jax import lax
from jax.experimental import pallas as pl
from jax.experimental.pallas import tpu as pltpu

F32 = jnp.float32
BF16 = jnp.bfloat16
HIGHEST = lax.Precision.HIGHEST

LANES = 128
SUBLANES = 8

HEAD_DIM = 64
D_MODEL = 1024
A_WIDTH = 256
CONV_WIDTH = 4
RG_C = 8.0
B_HEADS = 8
B_KV_HEADS = 4
C_HEADS = 4
MOBA_BLOCK = 256
MOBA_TOPK = 3
PAGE_SIZE = 128
MLSTM_CHUNK = 64
PLE_DIM = 256
NORM_EPS = 1e-6
NB_MAX = 32
NEG = -1e30

U_XA, U_GA = 0, 256
U_QB, U_GB, U_KB, U_VB = 512, 1024, 1536, 1792
U_CQ, U_CK, U_CV, U_CO, U_CG, U_CIF = 2048, 2304, 2560, 2816, 3072, 3328
D_U = 3456
_REF_SPLITS = (256, 256, 512, 256, 256, 512, 256, 256, 256, 4, 4, 256, 256)
_REF_OFFS = np.concatenate([[0], np.cumsum(_REF_SPLITS)]).tolist()

VMEM_LIMIT = 48 * 1024 * 1024


def _cparams(sem):
    return pltpu.CompilerParams(dimension_semantics=sem, vmem_limit_bytes=VMEM_LIMIT)


def _lane_iota(shape):
    return lax.broadcasted_iota(jnp.int32, shape, len(shape) - 1)


def _row_iota(shape):
    return lax.broadcasted_iota(jnp.int32, shape, len(shape) - 2)


def _sigmoid(x):
    return 1.0 / (1.0 + jnp.exp(-x))


def _silu(x):
    return x * _sigmoid(x)


def _softplus(x):
    return jnp.maximum(x, 0.0) + jnp.log1p(jnp.exp(-jnp.abs(x)))


def _dot_nt(a, b, **kw):
    return lax.dot_general(a, b, (((1,), (1,)), ((), ())), preferred_element_type=F32, **kw)


def _inproj_kernel(x_ref, g_ref, w_ref, qg_ref, kg_ref, bd_ref, u_ref, k_ref, v_ref):
    x = x_ref[...]
    xn = x * lax.rsqrt(jnp.mean(x * x, axis=-1, keepdims=True) + NORM_EPS) * g_ref[...]
    u = jnp.dot(xn.astype(BF16), w_ref[...], preferred_element_type=F32)
    u_ref[...] = u
    bd = bd_ref[...]
    q = u[:, U_QB:U_QB + 512]
    qss = jnp.dot((q * q).astype(BF16), bd, preferred_element_type=F32) * (1.0 / HEAD_DIM)
    qn = q * lax.rsqrt(qss + NORM_EPS) * qg_ref[...] * (HEAD_DIM ** -0.5)
    u_ref[:, U_QB:U_QB + 512] = qn
    k = u[:, U_KB:U_KB + 256]
    kss = jnp.dot((k * k).astype(BF16), bd[:256, :256], preferred_element_type=F32) * (1.0 / HEAD_DIM)
    kn = k * lax.rsqrt(kss + NORM_EPS) * kg_ref[...]
    u_ref[:, U_KB:U_KB + 256] = kn
    k_ref[...] = kn
    v_ref[...] = u[:, U_VB:U_VB + 256]


def _inproj(x2d, norm_g, w_u, qg, kg, bd, tm):
    m = x2d.shape[0]
    const = lambda i: (0, 0)
    return pl.pallas_call(
        _inproj_kernel,
        grid=(m // tm,),
        in_specs=[
            pl.BlockSpec((tm, D_MODEL), lambda i: (i, 0)),
            pl.BlockSpec((1, D_MODEL), const),
            pl.BlockSpec((D_MODEL, D_U), const),
            pl.BlockSpec((1, 512), const),
            pl.BlockSpec((1, 256), const),
            pl.BlockSpec((512, 512), const),
        ],
        out_specs=[
            pl.BlockSpec((tm, D_U), lambda i: (i, 0)),
            pl.BlockSpec((tm, 256), lambda i: (i, 0)),
            pl.BlockSpec((tm, 256), lambda i: (i, 0)),
        ],
        out_shape=[
            jax.ShapeDtypeStruct((m, D_U), F32),
            jax.ShapeDtypeStruct((m, 256), F32),
            jax.ShapeDtypeStruct((m, 256), F32),
        ],
        compiler_params=_cparams(("parallel",)),
        name="inproj",
    )(x2d, norm_g, w_u, qg, kg, bd)


def _rglru_kernel(xa_ref, ga_ref, conv0_ref, h0_ref, cw_ref, cb_ref, wri_ref, bri_ref, lam_ref,
                  ya_ref, convn_ref, hl_ref, xbuf, abuf, bbuf, hbuf, hcar, *, tt):
    ti = pl.program_id(1)
    nt = pl.num_programs(1)

    @pl.when(ti == 0)
    def _():
        xbuf[5:8, :] = conv0_ref[0]
        hcar[...] = jnp.broadcast_to(h0_ref[0], (SUBLANES, A_WIDTH))

    xa = xa_ref[...]
    xbuf[8:8 + tt, :] = xa
    cw = cw_ref[...]
    xc = jnp.broadcast_to(cb_ref[...], (tt, A_WIDTH))
    for j in range(CONV_WIDTH):
        xc = xc + xbuf[5 + j:5 + j + tt, :] * cw[j:j + 1, :]
    xbuf[5:8, :] = xa[tt - 3:tt, :]

    gates = jnp.dot(xc.astype(BF16), wri_ref[...], preferred_element_type=F32) + bri_ref[...]
    r = _sigmoid(gates[:, :A_WIDTH])
    ig = _sigmoid(gates[:, A_WIDTH:])
    log_a = (-RG_C) * r * _softplus(-lam_ref[...])
    a = jnp.exp(log_a)
    abuf[...] = a
    bbuf[...] = jnp.sqrt(-jnp.tanh(log_a) * (a * a + 1.0)) * (ig * xc)

    rows = _row_iota((SUBLANES, A_WIDTH))

    def group(gi, carry):
        r0 = pl.multiple_of(gi * SUBLANES, SUBLANES)
        a = abuf[pl.ds(r0, SUBLANES), :]
        b = bbuf[pl.ds(r0, SUBLANES), :]
        for d in (1, 2, 4):
            a_sh = pltpu.roll(a, d, 0)
            b_sh = pltpu.roll(b, d, 0)
            keep = rows >= d
            b = jnp.where(keep, a * b_sh + b, b)
            a = jnp.where(keep, a * a_sh, a)
        h = a * hcar[...] + b
        hbuf[pl.ds(r0, SUBLANES), :] = h
        hcar[...] = jnp.broadcast_to(h[SUBLANES - 1:SUBLANES, :], (SUBLANES, A_WIDTH))
        return carry

    lax.fori_loop(0, tt // SUBLANES, group, 0)

    ya_ref[...] = hbuf[...] * _silu(ga_ref[...])

    @pl.when(ti == nt - 1)
    def _():
        convn_ref[0] = xa[tt - 3:tt, :]
        hl_ref[0] = hcar[0:1, :]


def _rglru(u, conv0, h0, cw, cb, wri, bri, lam, bsz, t, tt):
    nt = t // tt
    kern = functools.partial(_rglru_kernel, tt=tt)
    const = lambda b, i: (0, 0)
    return pl.pallas_call(
        kern,
        grid=(bsz, nt),
        in_specs=[
            pl.BlockSpec((tt, A_WIDTH), lambda b, i: (b * nt + i, U_XA // A_WIDTH)),
            pl.BlockSpec((tt, A_WIDTH), lambda b, i: (b * nt + i, U_GA // A_WIDTH)),
            pl.BlockSpec((1, 3, A_WIDTH), lambda b, i: (b, 0, 0)),
            pl.BlockSpec((1, 1, A_WIDTH), lambda b, i: (b, 0, 0)),
            pl.BlockSpec((CONV_WIDTH, A_WIDTH), const),
            pl.BlockSpec((1, A_WIDTH), const),
            pl.BlockSpec((A_WIDTH, 2 * A_WIDTH), const),
            pl.BlockSpec((1, 2 * A_WIDTH), const),
            pl.BlockSpec((1, A_WIDTH), const),
        ],
        out_specs=[
            pl.BlockSpec((tt, A_WIDTH), lambda b, i: (b * nt + i, 0)),
            pl.BlockSpec((1, 3, A_WIDTH), lambda b, i: (b, 0, 0)),
            pl.BlockSpec((1, 1, A_WIDTH), lambda b, i: (b, 0, 0)),
        ],
        out_shape=[
            jax.ShapeDtypeStruct((bsz * t, A_WIDTH), F32),
            jax.ShapeDtypeStruct((bsz, 3, A_WIDTH), F32),
            jax.ShapeDtypeStruct((bsz, 1, A_WIDTH), F32),
        ],
        scratch_shapes=[
            pltpu.VMEM((tt + 8, A_WIDTH), F32),
            pltpu.VMEM((tt, A_WIDTH), F32),
            pltpu.VMEM((tt, A_WIDTH), F32),
            pltpu.VMEM((tt, A_WIDTH), F32),
            pltpu.VMEM((SUBLANES, A_WIDTH), F32),
        ],
        compiler_params=_cparams(("parallel", "arbitrary")),
        name="rglru",
    )(u, u, conv0, h0, cw, cb, wri, bri, lam)


def _kmean_kernel(k_ref, o_ref, *, nblk):
    k = k_ref[...].reshape(nblk, MOBA_BLOCK, 256)
    o_ref[...] = jnp.sum(k, axis=1) * (1.0 / MOBA_BLOCK)


def _kmean(k2d, nblk):
    m = k2d.shape[0]
    rows = nblk * MOBA_BLOCK
    return pl.pallas_call(
        functools.partial(_kmean_kernel, nblk=nblk),
        grid=(m // rows,),
        in_specs=[pl.BlockSpec((rows, 256), lambda i: (i, 0))],
        out_specs=pl.BlockSpec((nblk, 256), lambda i: (i, 0)),
        out_shape=jax.ShapeDtypeStruct((m // MOBA_BLOCK, 256), F32),
        compiler_params=_cparams(("parallel",)),
        name="kmean",
    )(k2d)


def _select_kernel(q_ref, k_ref, v_ref, w_ref, shift_ref, qa_ref, ka_ref, va_ref, *, ts, rs):
    tile0 = pl.program_id(1) * ts
    w = w_ref[0]
    neg_shift = -shift_ref[...]

    def sub_tile(si, carry):
        r0 = pl.multiple_of(si * rs, rs)
        t0 = tile0 + r0
        q = q_ref[pl.ds(r0, rs), :]
        sc = _dot_nt(w, q, precision=HIGHEST)
        shp = (B_HEADS * NB_MAX, rs)
        j = _row_iota(shp) & (NB_MAX - 1)
        n_full = (t0 + _lane_iota(shp)) >> 8
        valid = j < n_full
        scm = jnp.where(valid, sc, -jnp.inf)
        ranks = []
        for h in range(B_HEADS):
            grp = scm[h * NB_MAX:(h + 1) * NB_MAX]
            jg = j[h * NB_MAX:(h + 1) * NB_MAX]
            rank = jnp.zeros((NB_MAX, rs), jnp.int32)
            for jp in range(NB_MAX):
                p = jnp.broadcast_to(grp[jp:jp + 1, :], (NB_MAX, rs))
                rank = rank + jnp.where(jg > jp, jnp.where(p >= grp, 1, 0), jnp.where(p > grp, 1, 0))
            ranks.append(rank)
        rank = jnp.concatenate(ranks, axis=0)
        allowed = (valid & (rank < MOBA_TOPK)) | (j == n_full)
        bias = jnp.where(allowed, 0.0, NEG).T

        shp1 = (rs, LANES)
        lane = _lane_iota(shp1)
        pos1 = t0 + _row_iota(shp1)
        pos_hi = (pos1 >> 6).astype(F32)
        pos_lo = (pos1 & 63).astype(F32)
        blk_onehot = jnp.where((lane - 64) == (pos1 >> 8), 1.0, 0.0)
        kfeat = jnp.where(lane == 96, pos_hi, jnp.where(lane == 97, pos_lo,
                          jnp.where((lane >= 98) & (lane <= 100), 1.0, 0.0)))
        kextra = jnp.where(lane < 96, blk_onehot, kfeat)
        vextra = jnp.where(lane == 64, 1.0, 0.0)
        k = k_ref[pl.ds(r0, rs), :]
        v = v_ref[pl.ds(r0, rs), :]
        for g in range(B_KV_HEADS):
            kt = k[:, (g // 2) * LANES:(g // 2 + 1) * LANES]
            vt = v[:, (g // 2) * LANES:(g // 2 + 1) * LANES]
            if g % 2:
                kt = pltpu.roll(kt, 64, 1)
                vt = pltpu.roll(vt, 64, 1)
            ka_ref[0, g, pl.ds(r0, rs), :] = jnp.where(lane < 64, kt, kextra).astype(BF16)
            va_ref[0, g, pl.ds(r0, rs), :] = jnp.where(lane < 64, vt, vextra).astype(BF16)
            qt = q[:, g * LANES:(g + 1) * LANES]
            for h2 in range(2):
                h = 2 * g + h2
                base = qt if h2 == 0 else pltpu.roll(qt, 64, 1)
                bt = bias[:, (h // 4) * LANES:(h // 4 + 1) * LANES]
                sh = (64 - (h % 4) * NB_MAX) % LANES
                if sh:
                    bt = pltpu.roll(bt, sh, 1)
                slope = 2.0 ** (-(h + 1))
                qfeat = jnp.where(lane == 96, slope * 64.0, jnp.where(lane == 97, slope,
                                  jnp.where(lane == 98, (-slope * 64.0) * pos_hi,
                                            jnp.where(lane == 99, (-slope) * pos_lo,
                                                      jnp.where(lane == 100, neg_shift, 0.0)))))
                aug = jnp.where(lane < 64, base, jnp.where(lane < 96, bt, qfeat))
                qa_ref[0, h, pl.ds(r0, rs), :] = aug.astype(BF16)
        return carry

    lax.fori_loop(0, ts // rs, sub_tile, 0)


def _select(u, k2d, v2d, wsel, shift, bsz, s, ts):
    nt = s // ts
    return pl.pallas_call(
        functools.partial(_select_kernel, ts=ts, rs=min(ts, LANES)),
        grid=(bsz, nt),
        in_specs=[
            pl.BlockSpec((ts, 512), lambda b, i: (b * nt + i, U_QB // 512)),
            pl.BlockSpec((ts, 256), lambda b, i: (b * nt + i, 0)),
            pl.BlockSpec((ts, 256), lambda b, i: (b * nt + i, 0)),
            pl.BlockSpec((1, B_HEADS * NB_MAX, 512), lambda b, i: (b, 0, 0)),
            pl.BlockSpec((1, LANES), lambda b, i: (0, 0)),
        ],
        out_specs=[
            pl.BlockSpec((1, B_HEADS, ts, LANES), lambda b, i: (b, 0, i, 0)),
            pl.BlockSpec((1, B_KV_HEADS, ts, LANES), lambda b, i: (b, 0, i, 0)),
            pl.BlockSpec((1, B_KV_HEADS, ts, LANES), lambda b, i: (b, 0, i, 0)),
        ],
        out_shape=[
            jax.ShapeDtypeStruct((bsz, B_HEADS, s, LANES), BF16),
            jax.ShapeDtypeStruct((bsz, B_KV_HEADS, s, LANES), BF16),
            jax.ShapeDtypeStruct((bsz, B_KV_HEADS, s, LANES), BF16),
        ],
        compiler_params=_cparams(("parallel", "parallel")),
        name="moba_select",
    )(u, k2d, v2d, wsel, shift)


ATTN_KV_BLOCKS = 4
SHIFT_LIMIT = 30.0


def _attn_finish(o_ref, acc_sc, tq):
    acc = acc_sc[...]
    res = acc / acc[:, HEAD_DIM:HEAD_DIM + 1]
    lane = _lane_iota((tq, LANES))
    o_ref[...] = jnp.where(lane < 64, res[:tq], pltpu.roll(res[tq:], 64, 1))


def _attn_online(qa_ref, ka_ref, va_ref, o_ref, m_sc, acc_sc, tq):
    qi = pl.program_id(2)
    rows = 2 * tq
    q = qa_ref[0].reshape(rows, LANES)

    k0 = ka_ref[0, 0, pl.ds(pl.multiple_of(qi * tq, tq), tq), :]
    v0 = va_ref[0, 0, pl.ds(pl.multiple_of(qi * tq, tq), tq), :]
    s = _dot_nt(q, k0)
    shp = (rows, tq)
    qpos = _row_iota(shp) & (tq - 1)
    s = jnp.where(_lane_iota(shp) <= qpos, s, NEG)
    m0 = jnp.max(s, axis=-1, keepdims=True)
    p = jnp.exp(s - m0)
    m_sc[...] = jnp.broadcast_to(m0, (rows, LANES))
    acc_sc[...] = jnp.dot(p.astype(BF16), v0, preferred_element_type=F32)

    def body(jb, carry):
        r0 = pl.multiple_of(jb * tq, tq)
        kj = ka_ref[0, 0, pl.ds(r0, tq), :]
        vj = va_ref[0, 0, pl.ds(r0, tq), :]
        sj = _dot_nt(q, kj)
        m_prev = m_sc[...]
        m_new = jnp.maximum(m_prev, jnp.max(sj, axis=-1, keepdims=True))
        alpha = jnp.exp(m_prev - m_new)
        pj = jnp.exp(sj - m_new[:, 0:1])
        acc_sc[...] = alpha * acc_sc[...] + jnp.dot(pj.astype(BF16), vj, preferred_element_type=F32)
        m_sc[...] = m_new
        return carry

    lax.fori_loop(0, qi, body, 0)
    _attn_finish(o_ref, acc_sc, tq)


def _attn_shifted(qa_ref, ka_ref, va_ref, o_ref, acc_sc, tq):
    qi = pl.program_id(2)
    rows = 2 * tq
    keys = ATTN_KV_BLOCKS * tq
    q = qa_ref[0].reshape(rows, LANES)
    shp = (rows, keys)
    qpos = qi * tq + (_row_iota(shp) & (tq - 1))
    kofs = _lane_iota(shp)
    acc_sc[...] = jnp.zeros((rows, LANES), F32)

    def body(jp, carry):
        r0 = pl.multiple_of(jp * keys, keys)
        kj = ka_ref[0, 0, pl.ds(r0, keys), :]
        vj = va_ref[0, 0, pl.ds(r0, keys), :]
        s = _dot_nt(q, kj)
        p = jnp.exp(jnp.where(r0 + kofs <= qpos, s, NEG))
        acc_sc[...] += jnp.dot(p.astype(BF16), vj, preferred_element_type=F32)
        return carry

    lax.fori_loop(0, (qi + ATTN_KV_BLOCKS) // ATTN_KV_BLOCKS, body, 0)
    _attn_finish(o_ref, acc_sc, tq)


def _attn_kernel(mode_ref, qa_ref, ka_ref, va_ref, o_ref, m_sc, acc_sc, *, tq):
    @pl.when(mode_ref[0] == 1)
    def _():
        _attn_shifted(qa_ref, ka_ref, va_ref, o_ref, acc_sc, tq)

    @pl.when(mode_ref[0] != 1)
    def _():
        _attn_online(qa_ref, ka_ref, va_ref, o_ref, m_sc, acc_sc, tq)


def _attn(mode, qa, ka, va, bsz, s):
    tq = MOBA_BLOCK
    nq = s // tq
    assert nq % ATTN_KV_BLOCKS == 0
    grid_spec = pltpu.PrefetchScalarGridSpec(
        num_scalar_prefetch=1,
        grid=(bsz, B_KV_HEADS, nq),
        in_specs=[
            pl.BlockSpec((1, 2, tq, LANES), lambda b, g, i, mode: (b, g, i, 0)),
            pl.BlockSpec((1, 1, s, LANES), lambda b, g, i, mode: (b, g, 0, 0)),
            pl.BlockSpec((1, 1, s, LANES), lambda b, g, i, mode: (b, g, 0, 0)),
        ],
        out_specs=pl.BlockSpec((tq, LANES), lambda b, g, i, mode: (b * nq + i, g)),
        scratch_shapes=[
            pltpu.VMEM((2 * tq, LANES), F32),
            pltpu.VMEM((2 * tq, LANES), F32),
        ],
    )
    return pl.pallas_call(
        functools.partial(_attn_kernel, tq=tq),
        grid_spec=grid_spec,
        out_shape=jax.ShapeDtypeStruct((bsz * s, 512), F32),
        compiler_params=_cparams(("parallel", "parallel", "arbitrary")),
        name="moba_attn",
    )(mode, qa, ka, va)


PAGES_PER_STEP = 16


def _page_copies(pt_ref, src_hbm, buf, sem, b, c, slot):
    return [pltpu.make_async_copy(src_hbm.at[pt_ref[b, c * PAGES_PER_STEP + p]], buf.at[slot, p], sem.at[slot])
            for p in range(PAGES_PER_STEP)]


def _sample_moba_kernel(pt_ref, q_ref, kn_ref, vn_ref, slope_ref, ck_hbm, cv_hbm, o_ref,
                        kst, vst, kres, kmt, ksem, vsem, *, nseq, nch, nblk, tnew, past):
    b = pl.program_id(0)
    cur = b % 2
    nxt = 1 - cur
    has_next = b + 1 < nseq
    rows = 64
    blocks = PAGES_PER_STEP * PAGE_SIZE // MOBA_BLOCK
    ppb = MOBA_BLOCK // PAGE_SIZE
    km_lane = _lane_iota((256, LANES)) & (nblk - 1)

    def k_copies(seq, c, st):
        return _page_copies(pt_ref, ck_hbm, kst, ksem, seq, c, st)

    def v_copies(c, st):
        return _page_copies(pt_ref, cv_hbm, vst, vsem, b, c, st)

    def stage_k_chunk(res_slot, c, st):
        kres[res_slot, pl.ds(c * PAGES_PER_STEP, PAGES_PER_STEP)] = kst[st].astype(BF16)
        tile = jnp.zeros((256, LANES), F32)
        for jj in range(blocks):
            x = kst[st, ppb * jj]
            for pp in range(1, ppb):
                x = x + kst[st, ppb * jj + pp]
            mean = jnp.sum(x, axis=-1, keepdims=True) * (1.0 / MOBA_BLOCK)
            tile = jnp.where(km_lane == c * blocks + jj, mean, tile)
        if c == 0:
            kmt[res_slot] = tile
        else:
            kmt[res_slot] = kmt[res_slot] + tile

    @pl.when(b == 0)
    def _():
        for cp in k_copies(0, 0, 0):
            cp.start()
        for c in range(nch):
            st = c % 2
            if c + 1 < nch:
                for cp in k_copies(0, c + 1, 1 - st):
                    cp.start()
            for cp in k_copies(0, c, st):
                cp.wait()
            stage_k_chunk(0, c, st)

    @pl.when(has_next)
    def _():
        for cp in k_copies(b + 1, 0, 0):
            cp.start()

    for cp in v_copies(0, 0):
        cp.start()

    qf = q_ref[0]
    qb = qf.astype(BF16)
    slope = slope_ref[...]
    shp = (rows, LANES)
    lane = _lane_iota(shp)
    tok = _row_iota(shp) & (tnew - 1)

    sc = jnp.dot(qf, kmt[cur], preferred_element_type=F32, precision=HIGHEST)
    j = lane & (nblk - 1)
    rank = jnp.zeros(shp, jnp.int32)
    for s in range(1, nblk):
        a = pltpu.roll(sc, 2 * nblk - s, 1)
        wrapped = j >= nblk - s
        rank = rank + jnp.where(wrapped, jnp.where(a >= sc, 1, 0), jnp.where(a > sc, 1, 0))
    bias = jnp.where(rank < MOBA_TOPK, 0.0, NEG)

    s_own = _dot_nt(qb, kn_ref[0].astype(BF16))
    s_own = s_own - slope * (tok - lane).astype(F32)
    s_own = jnp.where((lane <= tok) & (lane < tnew), s_own, NEG)
    m = jnp.max(s_own, axis=-1, keepdims=True)
    p = jnp.exp(s_own - m)
    l = jnp.sum(p, axis=-1, keepdims=True)
    acc = jnp.dot(p.astype(BF16), vn_ref[0].astype(BF16), preferred_element_type=F32)

    keys = PAGES_PER_STEP * PAGE_SIZE
    blocks = keys // MOBA_BLOCK
    kshape = (rows, keys)
    klane = _lane_iota(kshape)
    qpos = past + (_row_iota(kshape) & (tnew - 1))
    slope_k = jnp.broadcast_to(slope[:, 0:1], kshape)
    for c in range(nch):
        st = c % 2
        if c + 1 < nch:
            for cp in v_copies(c + 1, 1 - st):
                cp.start()

            @pl.when(has_next)
            def _():
                for cp in k_copies(b + 1, c + 1, 1 - st):
                    cp.start()

        for cp in v_copies(c, st):
            cp.wait()
        vc = vst[st].astype(BF16)
        s_c = jnp.concatenate(
            [jnp.dot(qb, kres[cur, c * PAGES_PER_STEP + pg], preferred_element_type=F32)
             for pg in range(PAGES_PER_STEP)], axis=1)
        s_c = s_c - slope_k * (qpos - (c * keys + klane)).astype(F32)
        sel = jnp.concatenate(
            [jnp.broadcast_to(bias[:, c * blocks + jj:c * blocks + jj + 1], (rows, MOBA_BLOCK))
             for jj in range(blocks)], axis=1)
        s_c = s_c + sel
        m_new = jnp.maximum(m, jnp.max(s_c, axis=-1, keepdims=True))
        alpha = jnp.exp(m - m_new)
        p = jnp.exp(s_c - m_new)
        l = alpha * l + jnp.sum(p, axis=-1, keepdims=True)
        pb = p.astype(BF16)
        pv = _dot_nt(pb[:, 0:PAGE_SIZE], vc[0])
        for pg in range(1, PAGES_PER_STEP):
            pv = pv + _dot_nt(pb[:, pg * PAGE_SIZE:(pg + 1) * PAGE_SIZE], vc[pg])
        acc = alpha * acc + pv
        m = m_new

        @pl.when(has_next)
        def _():
            for cp in k_copies(b + 1, c, st):
                cp.wait()
            stage_k_chunk(nxt, c, st)

    res = acc / l
    oshape = (rows, 256)
    own = (_lane_iota(oshape) >> 6) == (_row_iota(oshape) >> 4)
    res = jnp.where(own, res, 0.0)
    folded = res + pltpu.roll(res, 64, 1) + pltpu.roll(res, 128, 1) + pltpu.roll(res, 192, 1)
    o_ref[0] = folded[:, :HEAD_DIM]


def _sample_moba(pt, qbd, knew, vnew, slopes, ck, cv, bsz, n_pages, tnew):
    nch = n_pages // PAGES_PER_STEP
    nblk = n_pages * PAGE_SIZE // MOBA_BLOCK
    past = n_pages * PAGE_SIZE
    grid_spec = pltpu.PrefetchScalarGridSpec(
        num_scalar_prefetch=1,
        grid=(bsz,),
        in_specs=[
            pl.BlockSpec((1, 64, 256), lambda b, pt: (b, 0, 0)),
            pl.BlockSpec((1, LANES, 256), lambda b, pt: (b, 0, 0)),
            pl.BlockSpec((1, LANES, 256), lambda b, pt: (b, 0, 0)),
            pl.BlockSpec((64, LANES), lambda b, pt: (0, 0)),
            pl.BlockSpec(memory_space=pl.ANY),
            pl.BlockSpec(memory_space=pl.ANY),
        ],
        out_specs=pl.BlockSpec((1, 64, HEAD_DIM), lambda b, pt: (b, 0, 0)),
        scratch_shapes=[
            pltpu.VMEM((2, PAGES_PER_STEP, 256, PAGE_SIZE), F32),
            pltpu.VMEM((2, PAGES_PER_STEP, 256, PAGE_SIZE), F32),
            pltpu.VMEM((2, n_pages, 256, PAGE_SIZE), BF16),
            pltpu.VMEM((2, 256, LANES), F32),
            pltpu.SemaphoreType.DMA((2,)),
            pltpu.SemaphoreType.DMA((2,)),
        ],
    )
    return pl.pallas_call(
        functools.partial(_sample_moba_kernel, nseq=bsz, nch=nch, nblk=nblk, tnew=tnew, past=past),
        grid_spec=grid_spec,
        out_shape=jax.ShapeDtypeStruct((bsz, 64, HEAD_DIM), F32),
        compiler_params=_cparams(("arbitrary",)),
        name="sample_moba",
    )(pt, qbd, knew, vnew, slopes, ck, cv)


def _mlstm_kernel(q_ref, k_ref, v_ref, if_ref, o_ref, g_ref, bias_ref, og_ref, c0_ref, m0_ref,
                  y_ref, cout_ref, mout_ref, c_sc, m_sc, *, tc, L):
    ti = pl.program_id(1)
    nt = pl.num_programs(1)

    @pl.when(ti == 0)
    def _():
        c_sc[...] = c0_ref[0]
        m_sc[...] = m0_ref[0]

    gates = if_ref[...] + bias_ref[...]
    lf = jnp.minimum(gates, 0.0) - jnp.log1p(jnp.exp(-jnp.abs(gates)))
    tri = _lane_iota((L, L)) <= _row_iota((L, L))
    tri_f = jnp.where(tri, 1.0, 0.0)
    lane = _lane_iota((L, LANES))
    og = og_ref[...]
    for c in range(tc // L):
        r0 = c * L
        fcum = jnp.dot(tri_f, lf[r0:r0 + L], preferred_element_type=F32, precision=HIGHEST)
        z = gates[r0:r0 + L] - pltpu.roll(fcum, LANES - 4, 1)
        qt = q_ref[r0:r0 + L, :]
        kt = k_ref[r0:r0 + L, :] * (HEAD_DIM ** -0.5)
        vt = v_ref[r0:r0 + L, :]
        ot = _sigmoid(o_ref[r0:r0 + L, :])
        for pair in range(C_HEADS // 2):
            ytile = jnp.zeros((L, LANES), F32)
            for h2 in range(2):
                h = 2 * pair + h2
                sl = slice(pair * LANES, (pair + 1) * LANES)
                qh, kh, vh, oh = qt[:, sl], kt[:, sl], vt[:, sl], ot[:, sl]
                if h2:
                    qh, kh, vh, oh = [pltpu.roll(a, 64, 1) for a in (qh, kh, vh, oh)]
                qh = jnp.where(lane < 64, qh, 0.0).astype(BF16)
                kh = jnp.where(lane < 64, kh, 0.0).astype(BF16)
                vaug = jnp.where(lane < 64, vh, jnp.where(lane == 64, 1.0, 0.0))
                m_prev = m_sc[h:h + 1, 0:1]
                f_col = fcum[:, 4 + h:5 + h]
                z_col = z[:, h:h + 1]
                pick = jnp.where(lane == h, 1.0, 0.0)
                z_row = _dot_nt(pick, z, precision=HIGHEST)
                d_log = jnp.where(tri, f_col + z_row, -jnp.inf)
                inter = f_col + m_prev
                m_t = jnp.maximum(jnp.max(d_log, axis=-1, keepdims=True), inter)
                w_intra = _dot_nt(qh, kh) * jnp.exp(d_log - m_t)
                w_inter = jnp.exp(inter - m_t)
                c_prev = c_sc[h]
                num = (w_inter * jnp.dot(qh, c_prev.astype(BF16), preferred_element_type=F32)
                       + jnp.dot(w_intra.astype(BF16), vaug.astype(BF16), preferred_element_type=F32))
                den = num[:, HEAD_DIM:HEAD_DIM + 1]
                hh = num / jnp.maximum(jnp.abs(den), jnp.exp(-m_t))
                f_tot = f_col[L - 1:L, :]
                w_log = f_tot + z_col
                m_new = jnp.maximum(f_tot + m_prev, jnp.max(w_log, axis=0, keepdims=True))
                decay = jnp.exp(f_tot + m_prev - m_new)
                w = jnp.exp(w_log - m_new)
                upd = lax.dot_general(kh, (w * vaug).astype(BF16), (((0,), (0,)), ((), ())),
                                      preferred_element_type=F32)
                c_sc[h] = decay * c_prev + upd
                m_sc[h:h + 1, :] = jnp.broadcast_to(m_new, (1, LANES))
                ho = jnp.where(lane < 64, hh * oh, 0.0)
                ms = jnp.sum(ho * ho, axis=-1, keepdims=True) * (1.0 / HEAD_DIM)
                yh = ho * lax.rsqrt(ms + NORM_EPS)
                ytile = ytile + (pltpu.roll(yh, 64, 1) if h2 else yh)
            sl = slice(pair * LANES, (pair + 1) * LANES)
            y_ref[r0:r0 + L, sl] = ytile * og[:, sl] * _silu(g_ref[r0:r0 + L, sl])

    @pl.when(ti == nt - 1)
    def _():
        cout_ref[0] = c_sc[...]
        mout_ref[0] = m_sc[...]


def _mlstm(u, gate_bias, onorm_g, c0aug, m0, bsz, t, tc, L, wide):
    nt = t // tc
    const = lambda b, i: (0, 0)
    ucol = lambda off, w: (lambda b, i: (b * nt + i, off // w))
    return pl.pallas_call(
        functools.partial(_mlstm_wide_kernel if wide else _mlstm_kernel, tc=tc, L=L),
        grid=(bsz, nt),
        in_specs=[
            pl.BlockSpec((tc, 256), ucol(U_CQ, 256)),
            pl.BlockSpec((tc, 256), ucol(U_CK, 256)),
            pl.BlockSpec((tc, 256), ucol(U_CV, 256)),
            pl.BlockSpec((tc, LANES), ucol(U_CIF, LANES)),
            pl.BlockSpec((tc, 256), ucol(U_CO, 256)),
            pl.BlockSpec((tc, 256), ucol(U_CG, 256)),
            pl.BlockSpec((1, LANES), const),
            pl.BlockSpec((1, 256), const),
            pl.BlockSpec((1, C_HEADS, LANES, LANES), lambda b, i: (b, 0, 0, 0)),
            pl.BlockSpec((1, SUBLANES, LANES), lambda b, i: (b, 0, 0)),
        ],
        out_specs=[
            pl.BlockSpec((tc, 256), lambda b, i: (b * nt + i, 0)),
            pl.BlockSpec((1, C_HEADS, LANES, LANES), lambda b, i: (b, 0, 0, 0)),
            pl.BlockSpec((1, SUBLANES, LANES), lambda b, i: (b, 0, 0)),
        ],
        out_shape=[
            jax.ShapeDtypeStruct((bsz * t, 256), F32),
            jax.ShapeDtypeStruct((bsz, C_HEADS, LANES, LANES), F32),
            jax.ShapeDtypeStruct((bsz, SUBLANES, LANES), F32),
        ],
        scratch_shapes=[
            pltpu.VMEM((C_HEADS, LANES, LANES), F32),
            pltpu.VMEM((SUBLANES, LANES), F32),
        ],
        compiler_params=_cparams(("parallel", "arbitrary")),
        name="mlstm",
    )(u, u, u, u, u, u, gate_bias, onorm_g, c0aug, m0)


def _mlstm_wide_kernel(q_ref, k_ref, v_ref, if_ref, o_ref, g_ref, bias_ref, og_ref, c0_ref, m0_ref,
                       y_ref, cout_ref, mout_ref, c_sc, m_sc, *, tc, L):
    ti = pl.program_id(1)
    nt = pl.num_programs(1)

    @pl.when(ti == 0)
    def _():
        c_sc[...] = c0_ref[0]
        m_sc[...] = m0_ref[0]

    gates = if_ref[...] + bias_ref[...]
    lf = jnp.minimum(gates, 0.0) - jnp.log1p(jnp.exp(-jnp.abs(gates)))
    in_chunk = _row_iota((tc, LANES)) & (L - 1)
    fcum = lf
    d = 1
    while d < L:
        fcum = fcum + jnp.where(in_chunk >= d, pltpu.roll(fcum, d, 0), 0.0)
        d *= 2
    z = gates - pltpu.roll(fcum, LANES - 4, 1)
    z_t = z.T
    k_t = (k_ref[...] * (HEAD_DIM ** -0.5)).T

    tri = _lane_iota((L, L)) <= _row_iota((L, L))
    lane = _lane_iota((L, LANES))
    srow = _row_iota((LANES, L))
    og = og_ref[...]
    for c in range(tc // L):
        r0 = c * L
        for pair in range(C_HEADS // 2):
            sl = slice(pair * LANES, (pair + 1) * LANES)
            qt = q_ref[r0:r0 + L, sl]
            vt = v_ref[r0:r0 + L, sl]
            ot = _sigmoid(o_ref[r0:r0 + L, sl])
            ktt = k_t[pair * LANES:(pair + 1) * LANES, r0:r0 + L]
            ytile = jnp.zeros((L, LANES), F32)
            for h2 in range(2):
                h = 2 * pair + h2
                mine = (lane >= 64) if h2 else (lane < 64)
                ones_col = 0 if h2 else HEAD_DIM
                qh = jnp.where(mine, qt, 0.0).astype(BF16)
                kth = jnp.where((srow >= 64) if h2 else (srow < 64), ktt, 0.0).astype(BF16)
                vaug = jnp.where(mine, vt, jnp.where(lane == ones_col, 1.0, 0.0))
                m_prev = m_sc[h:h + 1, 0:1]
                f_col = fcum[r0:r0 + L, 4 + h:5 + h]
                z_col = z[r0:r0 + L, h:h + 1]
                z_row = z_t[h:h + 1, r0:r0 + L]
                d_log = jnp.where(tri, f_col + z_row, -jnp.inf)
                inter = f_col + m_prev
                m_t = jnp.maximum(jnp.max(d_log, axis=-1, keepdims=True), inter)
                w_intra = jnp.dot(qh, kth, preferred_element_type=F32) * jnp.exp(d_log - m_t)
                w_inter = jnp.exp(inter - m_t)
                c_prev = c_sc[h]
                num = (w_inter * jnp.dot(qh, c_prev.astype(BF16), preferred_element_type=F32)
                       + jnp.dot(w_intra.astype(BF16), vaug.astype(BF16), preferred_element_type=F32))
                den = num[:, ones_col:ones_col + 1]
                hh = num / jnp.maximum(jnp.abs(den), jnp.exp(-m_t))
                f_tot = f_col[L - 1:L, :]
                w_log = f_tot + z_col
                m_new = jnp.maximum(f_tot + m_prev, jnp.max(w_log, axis=0, keepdims=True))
                decay = jnp.exp(f_tot + m_prev - m_new)
                w = jnp.exp(w_log - m_new)
                c_sc[h] = decay * c_prev + jnp.dot(kth, (w * vaug).astype(BF16), preferred_element_type=F32)
                m_sc[h:h + 1, :] = jnp.broadcast_to(m_new, (1, LANES))
                ho = jnp.where(mine, hh * ot, 0.0)
                ms = jnp.sum(ho * ho, axis=-1, keepdims=True) * (1.0 / HEAD_DIM)
                ytile = ytile + ho * lax.rsqrt(ms + NORM_EPS)
            y_ref[r0:r0 + L, sl] = ytile * og[:, sl] * _silu(g_ref[r0:r0 + L, sl])

    @pl.when(ti == nt - 1)
    def _():
        cout_ref[0] = c_sc[...]
        mout_ref[0] = m_sc[...]


def _outproj_kernel(x_ref, ya_ref, yb_ref, gb_ref, yc_ref, p_ref, wa_ref, wb_ref, wc_ref,
                    pg_ref, pp_ref, png_ref, o_ref):
    yb = yb_ref[...] * _silu(gb_ref[...])
    x = (x_ref[...]
         + jnp.dot(ya_ref[...].astype(BF16), wa_ref[...], preferred_element_type=F32)
         + jnp.dot(yb.astype(BF16), wb_ref[...], preferred_element_type=F32)
         + jnp.dot(yc_ref[...].astype(BF16), wc_ref[...], preferred_element_type=F32))
    gate = _sigmoid(jnp.dot(x.astype(BF16), pg_ref[...], preferred_element_type=F32))
    e = jnp.dot(p_ref[...].astype(BF16), pp_ref[...], preferred_element_type=F32)
    e = e * lax.rsqrt(jnp.mean(e * e, axis=-1, keepdims=True) + NORM_EPS) * png_ref[...]
    o_ref[...] = x + gate * e


def _outproj(x2d, ya, yb, u, yc, p2d, wa, wb, wc, pg, pp, png, tm):
    m = x2d.shape[0]
    row = lambda i: (i, 0)
    const = lambda i: (0, 0)
    return pl.pallas_call(
        _outproj_kernel,
        grid=(m // tm,),
        in_specs=[
            pl.BlockSpec((tm, D_MODEL), row),
            pl.BlockSpec((tm, 256), row),
            pl.BlockSpec((tm, 512), row),
            pl.BlockSpec((tm, 512), lambda i: (i, U_GB // 512)),
            pl.BlockSpec((tm, 256), row),
            pl.BlockSpec((tm, PLE_DIM), row),
            pl.BlockSpec((256, D_MODEL), const),
            pl.BlockSpec((512, D_MODEL), const),
            pl.BlockSpec((256, D_MODEL), const),
            pl.BlockSpec((D_MODEL, D_MODEL), const),
            pl.BlockSpec((PLE_DIM, D_MODEL), const),
            pl.BlockSpec((1, D_MODEL), const),
        ],
        out_specs=pl.BlockSpec((tm, D_MODEL), row),
        out_shape=jax.ShapeDtypeStruct((m, D_MODEL), F32),
        compiler_params=_cparams(("parallel",)),
        name="outproj",
    )(x2d, ya, yb, u, yc, p2d, wa, wb, wc, pg, pp, png)


def _block_diag(w):
    n, a, b = w.shape
    eye = jnp.eye(n, dtype=w.dtype)
    return jnp.einsum("nab,nm->namb", w, eye).reshape(n * a, n * b)


def _prep_layer(l, w):
    (norm_g, w_in, a_conv_w, a_conv_b, a_w_r, a_b_r, a_w_i, a_b_i, a_lambda, b_qnorm_g, b_knorm_g,
     c_b_i, c_b_f, c_onorm_g, w_out, ple_gate, ple_proj, ple_norm_g) = [a[l] for a in w]
    col = lambda i: w_in[:, _REF_OFFS[i]:_REF_OFFS[i + 1]]
    w_u = jnp.concatenate(
        [col(0), col(1), col(2), col(5), col(3), col(4), col(6), col(7), col(8), col(11), col(12),
         col(9), col(10), jnp.zeros((D_MODEL, LANES - 2 * C_HEADS), F32)], axis=1).astype(BF16)
    gate_bias = jnp.concatenate([c_b_i, c_b_f, jnp.zeros((LANES - 2 * C_HEADS,), F32)])[None, :]
    shift = 1.02 * HEAD_DIM ** 0.5 * jnp.max(jnp.abs(b_qnorm_g)) * jnp.max(jnp.abs(b_knorm_g)) + 0.5
    return dict(
        shift=shift,
        norm_g=norm_g[None, :], w_u=w_u,
        qg=jnp.tile(b_qnorm_g, B_HEADS)[None, :], kg=jnp.tile(b_knorm_g, B_KV_HEADS)[None, :],
        bd=_block_diag(jnp.ones((B_HEADS, HEAD_DIM, HEAD_DIM), BF16)),
        cw=a_conv_w, cb=a_conv_b[None, :],
        wri=jnp.concatenate([_block_diag(a_w_r), _block_diag(a_w_i)], axis=1).astype(BF16),
        bri=jnp.concatenate([a_b_r, a_b_i])[None, :], lam=a_lambda[None, :],
        gate_bias=gate_bias, onorm_g=c_onorm_g[None, :],
        wa=w_out[:256].astype(BF16), wb=w_out[256:768].astype(BF16), wc=w_out[768:].astype(BF16),
        pg=ple_gate.astype(BF16), pp=ple_proj.astype(BF16), png=ple_norm_g[None, :],
    )


def _pick_tile(n, cap):
    t = min(n, cap)
    while n % t:
        t //= 2
    return t


def _layer(x2d, p2d, bsz, t, conv0, h0, c0, n0, m0, lw, attend):
    m = bsz * t
    tm = _pick_tile(m, 256)
    u, k2d, v2d = _inproj(x2d, lw["norm_g"], lw["w_u"], lw["qg"], lw["kg"], lw["bd"], tm)
    ya, conv_new, h_last = _rglru(u, conv0, h0[:, None, :], lw["cw"], lw["cb"], lw["wri"], lw["bri"],
                                  lw["lam"], bsz, t, _pick_tile(t, 512))
    yb = attend(u, k2d, v2d)
    wide = t % LANES == 0
    L = LANES if wide else int(np.gcd(t, MLSTM_CHUNK))
    lo, hi = slice(0, HEAD_DIM), slice(HEAD_DIM, 2 * HEAD_DIM)
    c0aug = jnp.zeros((bsz, C_HEADS, LANES, LANES), F32)
    if wide:
        c0aug = c0aug.at[:, 0::2, lo, lo].set(c0[:, 0::2]).at[:, 0::2, lo, HEAD_DIM].set(n0[:, 0::2])
        c0aug = c0aug.at[:, 1::2, hi, hi].set(c0[:, 1::2]).at[:, 1::2, hi, 0].set(n0[:, 1::2])
    else:
        c0aug = c0aug.at[:, :, lo, lo].set(c0).at[:, :, lo, HEAD_DIM].set(n0)
    m0b = jnp.zeros((bsz, SUBLANES, LANES), F32).at[:, :C_HEADS, :].set(
        jnp.broadcast_to(m0[:, :, None], (bsz, C_HEADS, LANES)))
    yc, caug, mout = _mlstm(u, lw["gate_bias"], lw["onorm_g"], c0aug, m0b, bsz, t, _pick_tile(t, 512), L, wide)
    if wide:
        c_new = jnp.stack([caug[:, 0, lo, lo], caug[:, 1, hi, hi], caug[:, 2, lo, lo], caug[:, 3, hi, hi]], axis=1)
        n_new = jnp.stack([caug[:, 0, lo, HEAD_DIM], caug[:, 1, hi, 0],
                           caug[:, 2, lo, HEAD_DIM], caug[:, 3, hi, 0]], axis=1)
    else:
        c_new, n_new = caug[:, :, lo, lo], caug[:, :, lo, HEAD_DIM]
    xo = _outproj(x2d, ya, yb, u, yc, p2d, lw["wa"], lw["wb"], lw["wc"], lw["pg"], lw["pp"], lw["png"], tm)
    state = (k2d.reshape(bsz, t, B_KV_HEADS, HEAD_DIM), v2d.reshape(bsz, t, B_KV_HEADS, HEAD_DIM),
             conv_new, h_last[:, 0, :], c_new, n_new, mout[:, :C_HEADS, 0])
    return xo, state


def _attend_prompt(u, k2d, v2d, bsz, s, shift):
    nb = s // MOBA_BLOCK
    assert s % MOBA_BLOCK == 0 and nb <= NB_MAX
    kmean = _kmean(k2d, _pick_tile(bsz * nb, 8)).reshape(bsz, nb, B_KV_HEADS, HEAD_DIM)
    kmh = jnp.repeat(kmean, B_HEADS // B_KV_HEADS, axis=2).transpose(0, 2, 1, 3)
    kmh = jnp.pad(kmh, ((0, 0), (0, 0), (0, NB_MAX - nb), (0, 0)))
    wsel = jnp.einsum("bhjd,hg->bhjgd", kmh, jnp.eye(B_HEADS, dtype=F32)).reshape(bsz, B_HEADS * NB_MAX, 512)
    qa, ka, va = _select(u, k2d, v2d, wsel, jnp.full((1, LANES), shift, F32), bsz, s, _pick_tile(s, 512))
    mode = (shift <= SHIFT_LIMIT).astype(jnp.int32).reshape(1)
    return _attn(mode, qa, ka, va, bsz, s)


def _attend_sample(u, k2d, v2d, bsz, t, pt, ck, cv, slopes):
    n_pages = pt.shape[1]
    assert t == 8 and n_pages % PAGES_PER_STEP == 0
    nblk = n_pages * PAGE_SIZE // MOBA_BLOCK
    assert nblk == 64
    q = u[:, U_QB:U_QB + 512].reshape(bsz, t, B_KV_HEADS, 2, HEAD_DIM).transpose(0, 2, 3, 1, 4)
    qbd = jnp.einsum("bghtd,gk->bghtkd", q, jnp.eye(B_KV_HEADS, dtype=F32)).reshape(bsz, 64, 256)
    pad = ((0, 0), (0, LANES - t), (0, 0))
    knew = jnp.pad(k2d.reshape(bsz, t, 256), pad)
    vnew = jnp.pad(v2d.reshape(bsz, t, 256), pad)
    o = _sample_moba(pt, qbd, knew, vnew, slopes, ck, cv, bsz, n_pages, t)
    return o.reshape(bsz, B_HEADS, t, HEAD_DIM).transpose(0, 2, 1, 3).reshape(bsz * t, 512)


def kernel(x_prompt, x_sample, cache_k, cache_v, state_rglru_conv, state_rglru_h, state_mlstm_C, state_mlstm_n, state_mlstm_m, page_table, p_prompt, p_sample, norm_g, w_in, a_conv_w, a_conv_b, a_w_r, a_b_r, a_w_i, a_b_i, a_lambda, b_qnorm_g, b_knorm_g, c_b_i, c_b_f, c_onorm_g, w_out, ple_gate, ple_proj, ple_norm_g):
    weights = (norm_g, w_in, a_conv_w, a_conv_b, a_w_r, a_b_r, a_w_i, a_b_i, a_lambda, b_qnorm_g, b_knorm_g,
               c_b_i, c_b_f, c_onorm_g, w_out, ple_gate, ple_proj, ple_norm_g)
    depth = w_in.shape[0]
    bp, sp = x_prompt.shape[:2]
    bs, ts = x_sample.shape[:2]
    n_phys = cache_k.shape[1]
    ck = cache_k.transpose(0, 1, 3, 4, 2).reshape(depth * n_phys, 256, PAGE_SIZE)
    cv = cache_v.transpose(0, 1, 3, 4, 2).reshape(depth * n_phys, 256, PAGE_SIZE)
    slopes = jnp.asarray(np.repeat(2.0 ** -(np.arange(1, B_HEADS + 1)), ts)[:, None]
                         * np.ones((1, LANES)), F32)

    xp = x_prompt.reshape(bp * sp, D_MODEL)
    xs = x_sample.reshape(bs * ts, D_MODEL)
    zeros = lambda *s: jnp.zeros(s, F32)
    sp_all, ss_all = [], []
    for l in range(depth):
        lw = _prep_layer(l, weights)
        xp, st_p = _layer(xp, p_prompt[l].reshape(bp * sp, PLE_DIM), bp, sp,
                          zeros(bp, CONV_WIDTH - 1, A_WIDTH), zeros(bp, A_WIDTH),
                          zeros(bp, C_HEADS, HEAD_DIM, HEAD_DIM), zeros(bp, C_HEADS, HEAD_DIM), zeros(bp, C_HEADS),
                          lw, functools.partial(_attend_prompt, bsz=bp, s=sp, shift=lw["shift"]))
        attend_s = functools.partial(_attend_sample, bsz=bs, t=ts, pt=page_table + l * n_phys,
                                     ck=ck, cv=cv, slopes=slopes)
        xs, st_s = _layer(xs, p_sample[l].reshape(bs * ts, PLE_DIM), bs, ts,
                          state_rglru_conv[l], state_rglru_h[l], state_mlstm_C[l], state_mlstm_n[l],
                          state_mlstm_m[l], lw, attend_s)
        sp_all.append(st_p)
        ss_all.append(st_s)

    stk = lambda states, j: jnp.stack([s_[j] for s_ in states])
    return (xp.reshape(bp, sp, D_MODEL), xs.reshape(bs, ts, D_MODEL),
            stk(sp_all, 0), stk(sp_all, 1), stk(ss_all, 0), stk(ss_all, 1),
            stk(sp_all, 2), stk(ss_all, 2), stk(sp_all, 3), stk(ss_all, 3),
            stk(sp_all, 4), stk(ss_all, 4), stk(sp_all, 5), stk(ss_all, 5),
            stk(sp_all, 6), stk(ss_all, 6))
```

```python
import functools

import numpy as np
import jax
import jax.numpy as jnp
from jax import lax
from jax.experimental import pallas as pl
from jax.experimental.pallas import tpu as pltpu

F32 = jnp.float32
BF16 = jnp.bfloat16
HIGHEST = lax.Precision.HIGHEST

LANES = 128
SUBLANES = 8

HEAD_DIM = 64
D_MODEL = 1024
A_WIDTH = 256
CONV_WIDTH = 4
RG_C = 8.0
B_HEADS = 8
B_KV_HEADS = 4
C_HEADS = 4
MOBA_BLOCK = 256
MOBA_TOPK = 3
PAGE_SIZE = 128
MLSTM_CHUNK = 64
PLE_DIM = 256
NORM_EPS = 1e-6
NB_MAX = 32
NEG = -1e30

U_XA, U_GA = 0, 256
U_QB, U_GB, U_KB, U_VB = 512, 1024, 1536, 1792
U_CQ, U_CK, U_CV, U_CO, U_CG, U_CIF = 2048, 2304, 2560, 2816, 3072, 3328
D_U = 3456
_REF_SPLITS = (256, 256, 512, 256, 256, 512, 256, 256, 256, 4, 4, 256, 256)
_REF_OFFS = np.concatenate([[0], np.cumsum(_REF_SPLITS)]).tolist()

VMEM_LIMIT = 48 * 1024 * 1024


def _cparams(sem):
    return pltpu.CompilerParams(dimension_semantics=sem, vmem_limit_bytes=VMEM_LIMIT)


def _lane_iota(shape):
    return lax.broadcasted_iota(jnp.int32, shape, len(shape) - 1)


def _row_iota(shape):
    return lax.broadcasted_iota(jnp.int32, shape, len(shape) - 2)


def _sigmoid(x):
    return 1.0 / (1.0 + jnp.exp(-x))


def _silu(x):
    return x * _sigmoid(x)


def _softplus(x):
    return jnp.maximum(x, 0.0) + jnp.log1p(jnp.exp(-jnp.abs(x)))


def _dot_nt(a, b, **kw):
    return lax.dot_general(a, b, (((1,), (1,)), ((), ())), preferred_element_type=F32, **kw)


def _inproj_kernel(x_ref, g_ref, w_ref, qg_ref, kg_ref, bd_ref, u_ref, k_ref, v_ref, *t_refs):
    x = x_ref[...]
    xn = x * lax.rsqrt(jnp.mean(x * x, axis=-1, keepdims=True) + NORM_EPS) * g_ref[...]
    u = jnp.dot(xn.astype(BF16), w_ref[...], preferred_element_type=F32)
    u_ref[...] = u
    bd = bd_ref[...]
    q = u[:, U_QB:U_QB + 512]
    qss = jnp.dot((q * q).astype(BF16), bd, preferred_element_type=F32) * (1.0 / HEAD_DIM)
    qn = q * lax.rsqrt(qss + NORM_EPS) * qg_ref[...] * (HEAD_DIM ** -0.5)
    u_ref[:, U_QB:U_QB + 512] = qn
    k = u[:, U_KB:U_KB + 256]
    kss = jnp.dot((k * k).astype(BF16), bd[:256, :256], preferred_element_type=F32) * (1.0 / HEAD_DIM)
    kn = k * lax.rsqrt(kss + NORM_EPS) * kg_ref[...]
    u_ref[:, U_KB:U_KB + 256] = kn
    k_ref[...] = kn
    v = u[:, U_VB:U_VB + 256]
    v_ref[...] = v
    if t_refs:
        kt_ref, vt_ref = t_refs
        kt_ref[0] = kn.T
        vt_ref[0] = v.T


def _inproj(x2d, norm_g, w_u, qg, kg, bd, tm, bsz, t):
    m = x2d.shape[0]
    const = lambda i: (0, 0)
    transposed = t % tm == 0
    nt = t // tm if transposed else 1
    t_specs = [pl.BlockSpec((1, 256, tm), lambda i: (i // nt, 0, i % nt))] * 2 if transposed else []
    t_shapes = [jax.ShapeDtypeStruct((bsz, 256, t), F32)] * 2 if transposed else []
    return pl.pallas_call(
        _inproj_kernel,
        grid=(m // tm,),
        in_specs=[
            pl.BlockSpec((tm, D_MODEL), lambda i: (i, 0)),
            pl.BlockSpec((1, D_MODEL), const),
            pl.BlockSpec((D_MODEL, D_U), const),
            pl.BlockSpec((1, 512), const),
            pl.BlockSpec((1, 256), const),
            pl.BlockSpec((512, 512), const),
        ],
        out_specs=[
            pl.BlockSpec((tm, D_U), lambda i: (i, 0)),
            pl.BlockSpec((tm, 256), lambda i: (i, 0)),
            pl.BlockSpec((tm, 256), lambda i: (i, 0)),
        ] + t_specs,
        out_shape=[
            jax.ShapeDtypeStruct((m, D_U), F32),
            jax.ShapeDtypeStruct((m, 256), F32),
            jax.ShapeDtypeStruct((m, 256), F32),
        ] + t_shapes,
        compiler_params=_cparams(("parallel",)),
        name="inproj",
    )(x2d, norm_g, w_u, qg, kg, bd)


def _rglru_kernel(xa_ref, ga_ref, conv0_ref, h0_ref, cw_ref, cb_ref, wri_ref, bri_ref, lam_ref,
                  ya_ref, convn_ref, hl_ref, xbuf, abuf, bbuf, hbuf, hcar, *, tt):
    ti = pl.program_id(1)
    nt = pl.num_programs(1)

    @pl.when(ti == 0)
    def _():
        xbuf[5:8, :] = conv0_ref[0]
        hcar[...] = jnp.broadcast_to(h0_ref[0], (SUBLANES, A_WIDTH))

    xa = xa_ref[...]
    xbuf[8:8 + tt, :] = xa
    cw = cw_ref[...]
    xc = jnp.broadcast_to(cb_ref[...], (tt, A_WIDTH))
    for j in range(CONV_WIDTH):
        xc = xc + xbuf[5 + j:5 + j + tt, :] * cw[j:j + 1, :]
    xbuf[5:8, :] = xa[tt - 3:tt, :]

    gates = jnp.dot(xc.astype(BF16), wri_ref[...], preferred_element_type=F32) + bri_ref[...]
    r = _sigmoid(gates[:, :A_WIDTH])
    ig = _sigmoid(gates[:, A_WIDTH:])
    log_a = (-RG_C) * r * _softplus(-lam_ref[...])
    a = jnp.exp(log_a)
    abuf[...] = a
    bbuf[...] = jnp.sqrt(-jnp.tanh(log_a) * (a * a + 1.0)) * (ig * xc)

    rows = _row_iota((SUBLANES, A_WIDTH))

    def group(gi, carry):
        r0 = pl.multiple_of(gi * SUBLANES, SUBLANES)
        a = abuf[pl.ds(r0, SUBLANES), :]
        b = bbuf[pl.ds(r0, SUBLANES), :]
        for d in (1, 2, 4):
            a_sh = pltpu.roll(a, d, 0)
            b_sh = pltpu.roll(b, d, 0)
            keep = rows >= d
            b = jnp.where(keep, a * b_sh + b, b)
            a = jnp.where(keep, a * a_sh, a)
        h = a * hcar[...] + b
        hbuf[pl.ds(r0, SUBLANES), :] = h
        hcar[...] = jnp.broadcast_to(h[SUBLANES - 1:SUBLANES, :], (SUBLANES, A_WIDTH))
        return carry

    lax.fori_loop(0, tt // SUBLANES, group, 0)

    ya_ref[...] = hbuf[...] * _silu(ga_ref[...])

    @pl.when(ti == nt - 1)
    def _():
        convn_ref[0] = xa[tt - 3:tt, :]
        hl_ref[0] = hcar[0:1, :]


def _rglru(u, conv0, h0, cw, cb, wri, bri, lam, bsz, t, tt):
    nt = t // tt
    kern = functools.partial(_rglru_kernel, tt=tt)
    const = lambda b, i: (0, 0)
    return pl.pallas_call(
        kern,
        grid=(bsz, nt),
        in_specs=[
            pl.BlockSpec((tt, A_WIDTH), lambda b, i: (b * nt + i, U_XA // A_WIDTH)),
            pl.BlockSpec((tt, A_WIDTH), lambda b, i: (b * nt + i, U_GA // A_WIDTH)),
            pl.BlockSpec((1, 3, A_WIDTH), lambda b, i: (b, 0, 0)),
            pl.BlockSpec((1, 1, A_WIDTH), lambda b, i: (b, 0, 0)),
            pl.BlockSpec((CONV_WIDTH, A_WIDTH), const),
            pl.BlockSpec((1, A_WIDTH), const),
            pl.BlockSpec((A_WIDTH, 2 * A_WIDTH), const),
            pl.BlockSpec((1, 2 * A_WIDTH), const),
            pl.BlockSpec((1, A_WIDTH), const),
        ],
        out_specs=[
            pl.BlockSpec((tt, A_WIDTH), lambda b, i: (b * nt + i, 0)),
            pl.BlockSpec((1, 3, A_WIDTH), lambda b, i: (b, 0, 0)),
            pl.BlockSpec((1, 1, A_WIDTH), lambda b, i: (b, 0, 0)),
        ],
        out_shape=[
            jax.ShapeDtypeStruct((bsz * t, A_WIDTH), F32),
            jax.ShapeDtypeStruct((bsz, 3, A_WIDTH), F32),
            jax.ShapeDtypeStruct((bsz, 1, A_WIDTH), F32),
        ],
        scratch_shapes=[
            pltpu.VMEM((tt + 8, A_WIDTH), F32),
            pltpu.VMEM((tt, A_WIDTH), F32),
            pltpu.VMEM((tt, A_WIDTH), F32),
            pltpu.VMEM((tt, A_WIDTH), F32),
            pltpu.VMEM((SUBLANES, A_WIDTH), F32),
        ],
        compiler_params=_cparams(("parallel", "arbitrary")),
        name="rglru",
    )(u, u, conv0, h0, cw, cb, wri, bri, lam)


def _kmean_kernel(k_ref, o_ref, *, nblk):
    k = k_ref[...].reshape(nblk, MOBA_BLOCK, 256)
    o_ref[...] = jnp.sum(k, axis=1) * (1.0 / MOBA_BLOCK)


def _kmean(k2d, nblk):
    m = k2d.shape[0]
    rows = nblk * MOBA_BLOCK
    return pl.pallas_call(
        functools.partial(_kmean_kernel, nblk=nblk),
        grid=(m // rows,),
        in_specs=[pl.BlockSpec((rows, 256), lambda i: (i, 0))],
        out_specs=pl.BlockSpec((nblk, 256), lambda i: (i, 0)),
        out_shape=jax.ShapeDtypeStruct((m // MOBA_BLOCK, 256), F32),
        compiler_params=_cparams(("parallel",)),
        name="kmean",
    )(k2d)


def _select_kernel(q_ref, k_ref, v_ref, w_ref, shift_ref, qa_ref, ka_ref, va_ref, *, ts, rs):
    tile0 = pl.program_id(1) * ts
    w = w_ref[0]
    neg_shift = -shift_ref[...]

    def sub_tile(si, carry):
        r0 = pl.multiple_of(si * rs, rs)
        t0 = tile0 + r0
        q = q_ref[pl.ds(r0, rs), :]
        sc = _dot_nt(w, q, precision=HIGHEST)
        shp = (B_HEADS * NB_MAX, rs)
        j = _row_iota(shp) & (NB_MAX - 1)
        n_full = (t0 + _lane_iota(shp)) >> 8
        valid = j < n_full
        scm = jnp.where(valid, sc, -jnp.inf)
        ranks = []
        for h in range(B_HEADS):
            grp = scm[h * NB_MAX:(h + 1) * NB_MAX]
            jg = j[h * NB_MAX:(h + 1) * NB_MAX]
            rank = jnp.zeros((NB_MAX, rs), jnp.int32)
            for jp in range(NB_MAX):
                p = jnp.broadcast_to(grp[jp:jp + 1, :], (NB_MAX, rs))
                rank = rank + jnp.where(jg > jp, jnp.where(p >= grp, 1, 0), jnp.where(p > grp, 1, 0))
            ranks.append(rank)
        rank = jnp.concatenate(ranks, axis=0)
        allowed = (valid & (rank < MOBA_TOPK)) | (j == n_full)
        bias = jnp.where(allowed, 0.0, NEG).T

        shp1 = (rs, LANES)
        lane = _lane_iota(shp1)
        pos1 = t0 + _row_iota(shp1)
        pos_hi = (pos1 >> 6).astype(F32)
        pos_lo = (pos1 & 63).astype(F32)
        blk_onehot = jnp.where((lane - 64) == (pos1 >> 8), 1.0, 0.0)
        kfeat = jnp.where(lane == 96, pos_hi, jnp.where(lane == 97, pos_lo,
                          jnp.where((lane >= 98) & (lane <= 100), 1.0, 0.0)))
        kextra = jnp.where(lane < 96, blk_onehot, kfeat)
        vextra = jnp.where(lane == 64, 1.0, 0.0)
        k = k_ref[pl.ds(r0, rs), :]
        v = v_ref[pl.ds(r0, rs), :]
        for g in range(B_KV_HEADS):
            kt = k[:, (g // 2) * LANES:(g // 2 + 1) * LANES]
            vt = v[:, (g // 2) * LANES:(g // 2 + 1) * LANES]
            if g % 2:
                kt = pltpu.roll(kt, 64, 1)
                vt = pltpu.roll(vt, 64, 1)
            ka_ref[0, g, pl.ds(r0, rs), :] = jnp.where(lane < 64, kt, kextra).astype(BF16)
            va_ref[0, g, pl.ds(r0, rs), :] = jnp.where(lane < 64, vt, vextra).astype(BF16)
            qt = q[:, g * LANES:(g + 1) * LANES]
            for h2 in range(2):
                h = 2 * g + h2
                base = qt if h2 == 0 else pltpu.roll(qt, 64, 1)
                bt = bias[:, (h // 4) * LANES:(h // 4 + 1) * LANES]
                sh = (64 - (h % 4) * NB_MAX) % LANES
                if sh:
                    bt = pltpu.roll(bt, sh, 1)
                slope = 2.0 ** (-(h + 1))
                qfeat = jnp.where(lane == 96, slope * 64.0, jnp.where(lane == 97, slope,
                                  jnp.where(lane == 98, (-slope * 64.0) * pos_hi,
                                            jnp.where(lane == 99, (-slope) * pos_lo,
                                                      jnp.where(lane == 100, neg_shift, 0.0)))))
                aug = jnp.where(lane < 64, base, jnp.where(lane < 96, bt, qfeat))
                qa_ref[0, h, pl.ds(r0, rs), :] = aug.astype(BF16)
        return carry

    lax.fori_loop(0, ts // rs, sub_tile, 0)


def _select(u, k2d, v2d, wsel, shift, bsz, s, ts):
    nt = s // ts
    return pl.pallas_call(
        functools.partial(_select_kernel, ts=ts, rs=min(ts, LANES)),
        grid=(bsz, nt),
        in_specs=[
            pl.BlockSpec((ts, 512), lambda b, i: (b * nt + i, U_QB // 512)),
            pl.BlockSpec((ts, 256), lambda b, i: (b * nt + i, 0)),
            pl.BlockSpec((ts, 256), lambda b, i: (b * nt + i, 0)),
            pl.BlockSpec((1, B_HEADS * NB_MAX, 512), lambda b, i: (b, 0, 0)),
            pl.BlockSpec((1, LANES), lambda b, i: (0, 0)),
        ],
        out_specs=[
            pl.BlockSpec((1, B_HEADS, ts, LANES), lambda b, i: (b, 0, i, 0)),
            pl.BlockSpec((1, B_KV_HEADS, ts, LANES), lambda b, i: (b, 0, i, 0)),
            pl.BlockSpec((1, B_KV_HEADS, ts, LANES), lambda b, i: (b, 0, i, 0)),
        ],
        out_shape=[
            jax.ShapeDtypeStruct((bsz, B_HEADS, s, LANES), BF16),
            jax.ShapeDtypeStruct((bsz, B_KV_HEADS, s, LANES), BF16),
            jax.ShapeDtypeStruct((bsz, B_KV_HEADS, s, LANES), BF16),
        ],
        compiler_params=_cparams(("parallel", "parallel")),
        name="moba_select",
    )(u, k2d, v2d, wsel, shift)


ATTN_KV_BLOCKS = 4
ATTN_GROUPS = 2
SHIFT_LIMIT = 30.0


def _attn_finish(o_ref, acc, gi, tq):
    res = acc / acc[:, HEAD_DIM:HEAD_DIM + 1]
    lane = _lane_iota((tq, LANES))
    o_ref[:, gi * LANES:(gi + 1) * LANES] = jnp.where(lane < 64, res[:tq], pltpu.roll(res[tq:], 64, 1))


def _attn_online(qa_ref, ka_ref, va_ref, o_ref, m_sc, acc_sc, tq):
    qi = pl.program_id(2)
    rows = 2 * tq
    for gi in range(ATTN_GROUPS):
        q = qa_ref[0, 2 * gi:2 * gi + 2].reshape(rows, LANES)

        k0 = ka_ref[0, gi, pl.ds(pl.multiple_of(qi * tq, tq), tq), :]
        v0 = va_ref[0, gi, pl.ds(pl.multiple_of(qi * tq, tq), tq), :]
        s = _dot_nt(q, k0)
        shp = (rows, tq)
        qpos = _row_iota(shp) & (tq - 1)
        s = jnp.where(_lane_iota(shp) <= qpos, s, NEG)
        m0 = jnp.max(s, axis=-1, keepdims=True)
        p = jnp.exp(s - m0)
        m_sc[gi] = jnp.broadcast_to(m0, (rows, LANES))
        acc_sc[gi] = jnp.dot(p.astype(BF16), v0, preferred_element_type=F32)

        def body(jb, carry, gi=gi, q=q):
            r0 = pl.multiple_of(jb * tq, tq)
            kj = ka_ref[0, gi, pl.ds(r0, tq), :]
            vj = va_ref[0, gi, pl.ds(r0, tq), :]
            sj = _dot_nt(q, kj)
            m_prev = m_sc[gi]
            m_new = jnp.maximum(m_prev, jnp.max(sj, axis=-1, keepdims=True))
            alpha = jnp.exp(m_prev - m_new)
            pj = jnp.exp(sj - m_new[:, 0:1])
            acc_sc[gi] = alpha * acc_sc[gi] + jnp.dot(pj.astype(BF16), vj, preferred_element_type=F32)
            m_sc[gi] = m_new
            return carry

        lax.fori_loop(0, qi, body, 0)
        _attn_finish(o_ref, acc_sc[gi], gi, tq)


def _attn_shifted(qa_ref, ka_ref, va_ref, o_ref, acc_sc, tq):
    qi = pl.program_id(2)
    rows = 2 * tq
    keys = ATTN_KV_BLOCKS * tq
    qs = [qa_ref[0, 2 * gi:2 * gi + 2].reshape(rows, LANES) for gi in range(ATTN_GROUPS)]
    shp = (rows, keys)
    qpos = qi * tq + (_row_iota(shp) & (tq - 1))
    kofs = _lane_iota(shp)
    acc_sc[...] = jnp.zeros((ATTN_GROUPS, rows, LANES), F32)

    def body(jp, carry):
        r0 = pl.multiple_of(jp * keys, keys)
        causal = r0 + kofs <= qpos
        for gi in range(ATTN_GROUPS):
            kj = ka_ref[0, gi, pl.ds(r0, keys), :]
            vj = va_ref[0, gi, pl.ds(r0, keys), :]
            p = jnp.exp(jnp.where(causal, _dot_nt(qs[gi], kj), NEG))
            acc_sc[gi] += jnp.dot(p.astype(BF16), vj, preferred_element_type=F32)
        return carry

    lax.fori_loop(0, (qi + ATTN_KV_BLOCKS) // ATTN_KV_BLOCKS, body, 0)
    for gi in range(ATTN_GROUPS):
        _attn_finish(o_ref, acc_sc[gi], gi, tq)


def _attn_kernel(mode_ref, qa_ref, ka_ref, va_ref, o_ref, m_sc, acc_sc, *, tq):
    @pl.when(mode_ref[0] == 1)
    def _():
        _attn_shifted(qa_ref, ka_ref, va_ref, o_ref, acc_sc, tq)

    @pl.when(mode_ref[0] != 1)
    def _():
        _attn_online(qa_ref, ka_ref, va_ref, o_ref, m_sc, acc_sc, tq)


def _attn(mode, qa, ka, va, bsz, s):
    tq = MOBA_BLOCK
    nq = s // tq
    ng = ATTN_GROUPS
    assert nq % ATTN_KV_BLOCKS == 0 and B_KV_HEADS % ng == 0
    grid_spec = pltpu.PrefetchScalarGridSpec(
        num_scalar_prefetch=1,
        grid=(bsz, B_KV_HEADS // ng, nq),
        in_specs=[
            pl.BlockSpec((1, 2 * ng, tq, LANES), lambda b, g, i, mode: (b, g, i, 0)),
            pl.BlockSpec((1, ng, s, LANES), lambda b, g, i, mode: (b, g, 0, 0)),
            pl.BlockSpec((1, ng, s, LANES), lambda b, g, i, mode: (b, g, 0, 0)),
        ],
        out_specs=pl.BlockSpec((tq, ng * LANES), lambda b, g, i, mode: (b * nq + i, g)),
        scratch_shapes=[
            pltpu.VMEM((ng, 2 * tq, LANES), F32),
            pltpu.VMEM((ng, 2 * tq, LANES), F32),
        ],
    )
    return pl.pallas_call(
        functools.partial(_attn_kernel, tq=tq),
        grid_spec=grid_spec,
        out_shape=jax.ShapeDtypeStruct((bsz * s, 512), F32),
        compiler_params=_cparams(("parallel", "parallel", "arbitrary")),
        name="moba_attn",
    )(mode, qa, ka, va)


PAGES_PER_STEP = 16


def _page_copies(pt_ref, src_hbm, buf, sem, b, c, slot):
    return [pltpu.make_async_copy(src_hbm.at[pt_ref[b, c * PAGES_PER_STEP + p]], buf.at[slot, p], sem.at[slot])
            for p in range(PAGES_PER_STEP)]


def _sample_moba_kernel(pt_ref, q_ref, kn_ref, vn_ref, slope_ref, ck_hbm, cv_hbm, o_ref,
                        kst, vst, kres, kmt, ksem, vsem, *, nseq, nch, nblk, tnew, past):
    b = pl.program_id(0)
    cur = b % 2
    nxt = 1 - cur
    has_next = b + 1 < nseq
    rows = 64
    blocks = PAGES_PER_STEP * PAGE_SIZE // MOBA_BLOCK
    ppb = MOBA_BLOCK // PAGE_SIZE
    km_lane = _lane_iota((256, LANES)) & (nblk - 1)

    def k_copies(seq, c, st):
        return _page_copies(pt_ref, ck_hbm, kst, ksem, seq, c, st)

    def v_copies(c, st):
        return _page_copies(pt_ref, cv_hbm, vst, vsem, b, c, st)

    def stage_k_chunk(res_slot, c, st):
        kres[res_slot, pl.ds(c * PAGES_PER_STEP, PAGES_PER_STEP)] = kst[st].astype(BF16)
        tile = jnp.zeros((256, LANES), F32)
        for jj in range(blocks):
            x = kst[st, ppb * jj]
            for pp in range(1, ppb):
                x = x + kst[st, ppb * jj + pp]
            mean = jnp.sum(x, axis=-1, keepdims=True) * (1.0 / MOBA_BLOCK)
            tile = jnp.where(km_lane == c * blocks + jj, mean, tile)
        if c == 0:
            kmt[res_slot] = tile
        else:
            kmt[res_slot] = kmt[res_slot] + tile

    @pl.when(b == 0)
    def _():
        for cp in k_copies(0, 0, 0):
            cp.start()
        for c in range(nch):
            st = c % 2
            if c + 1 < nch:
                for cp in k_copies(0, c + 1, 1 - st):
                    cp.start()
            for cp in k_copies(0, c, st):
                cp.wait()
            stage_k_chunk(0, c, st)

    @pl.when(has_next)
    def _():
        for cp in k_copies(b + 1, 0, 0):
            cp.start()

    for cp in v_copies(0, 0):
        cp.start()

    qf = q_ref[0]
    qb = qf.astype(BF16)
    slope = slope_ref[...]
    shp = (rows, LANES)
    lane = _lane_iota(shp)
    tok = _row_iota(shp) & (tnew - 1)

    sc = jnp.dot(qf, kmt[cur], preferred_element_type=F32, precision=HIGHEST)
    j = lane & (nblk - 1)
    rank = jnp.zeros(shp, jnp.int32)
    for s in range(1, nblk):
        a = pltpu.roll(sc, 2 * nblk - s, 1)
        wrapped = j >= nblk - s
        rank = rank + jnp.where(wrapped, jnp.where(a >= sc, 1, 0), jnp.where(a > sc, 1, 0))
    bias = jnp.where(rank < MOBA_TOPK, 0.0, NEG)

    s_own = _dot_nt(qb, kn_ref[0].astype(BF16))
    s_own = s_own - slope * (tok - lane).astype(F32)
    s_own = jnp.where((lane <= tok) & (lane < tnew), s_own, NEG)
    m = jnp.max(s_own, axis=-1, keepdims=True)
    p = jnp.exp(s_own - m)
    l = jnp.sum(p, axis=-1, keepdims=True)
    acc = jnp.dot(p.astype(BF16), vn_ref[0].astype(BF16), preferred_element_type=F32)

    keys = PAGES_PER_STEP * PAGE_SIZE
    blocks = keys // MOBA_BLOCK
    kshape = (rows, keys)
    klane = _lane_iota(kshape)
    qpos = past + (_row_iota(kshape) & (tnew - 1))
    slope_k = jnp.broadcast_to(slope[:, 0:1], kshape)
    for c in range(nch):
        st = c % 2
        if c + 1 < nch:
            for cp in v_copies(c + 1, 1 - st):
                cp.start()

            @pl.when(has_next)
            def _():
                for cp in k_copies(b + 1, c + 1, 1 - st):
                    cp.start()

        for cp in v_copies(c, st):
            cp.wait()
        vc = vst[st].astype(BF16)
        s_c = jnp.concatenate(
            [jnp.dot(qb, kres[cur, c * PAGES_PER_STEP + pg], preferred_element_type=F32)
             for pg in range(PAGES_PER_STEP)], axis=1)
        s_c = s_c - slope_k * (qpos - (c * keys + klane)).astype(F32)
        sel = jnp.concatenate(
            [jnp.broadcast_to(bias[:, c * blocks + jj:c * blocks + jj + 1], (rows, MOBA_BLOCK))
             for jj in range(blocks)], axis=1)
        s_c = s_c + sel
        m_new = jnp.maximum(m, jnp.max(s_c, axis=-1, keepdims=True))
        alpha = jnp.exp(m - m_new)
        p = jnp.exp(s_c - m_new)
        l = alpha * l + jnp.sum(p, axis=-1, keepdims=True)
        pb = p.astype(BF16)
        pv = _dot_nt(pb[:, 0:PAGE_SIZE], vc[0])
        for pg in range(1, PAGES_PER_STEP):
            pv = pv + _dot_nt(pb[:, pg * PAGE_SIZE:(pg + 1) * PAGE_SIZE], vc[pg])
        acc = alpha * acc + pv
        m = m_new

        @pl.when(has_next)
        def _():
            for cp in k_copies(b + 1, c, st):
                cp.wait()
            stage_k_chunk(nxt, c, st)

    res = acc / l
    oshape = (rows, 256)
    own = (_lane_iota(oshape) >> 6) == (_row_iota(oshape) >> 4)
    res = jnp.where(own, res, 0.0)
    folded = res + pltpu.roll(res, 64, 1) + pltpu.roll(res, 128, 1) + pltpu.roll(res, 192, 1)
    o_ref[0] = folded[:, :HEAD_DIM]


def _sample_moba(pt, qbd, knew, vnew, slopes, ck, cv, bsz, n_pages, tnew):
    nch = n_pages // PAGES_PER_STEP
    nblk = n_pages * PAGE_SIZE // MOBA_BLOCK
    past = n_pages * PAGE_SIZE
    grid_spec = pltpu.PrefetchScalarGridSpec(
        num_scalar_prefetch=1,
        grid=(bsz,),
        in_specs=[
            pl.BlockSpec((1, 64, 256), lambda b, pt: (b, 0, 0)),
            pl.BlockSpec((1, LANES, 256), lambda b, pt: (b, 0, 0)),
            pl.BlockSpec((1, LANES, 256), lambda b, pt: (b, 0, 0)),
            pl.BlockSpec((64, LANES), lambda b, pt: (0, 0)),
            pl.BlockSpec(memory_space=pl.ANY),
            pl.BlockSpec(memory_space=pl.ANY),
        ],
        out_specs=pl.BlockSpec((1, 64, HEAD_DIM), lambda b, pt: (b, 0, 0)),
        scratch_shapes=[
            pltpu.VMEM((2, PAGES_PER_STEP, 256, PAGE_SIZE), F32),
            pltpu.VMEM((2, PAGES_PER_STEP, 256, PAGE_SIZE), F32),
            pltpu.VMEM((2, n_pages, 256, PAGE_SIZE), BF16),
            pltpu.VMEM((2, 256, LANES), F32),
            pltpu.SemaphoreType.DMA((2,)),
            pltpu.SemaphoreType.DMA((2,)),
        ],
    )
    return pl.pallas_call(
        functools.partial(_sample_moba_kernel, nseq=bsz, nch=nch, nblk=nblk, tnew=tnew, past=past),
        grid_spec=grid_spec,
        out_shape=jax.ShapeDtypeStruct((bsz, 64, HEAD_DIM), F32),
        compiler_params=_cparams(("arbitrary",)),
        name="sample_moba",
    )(pt, qbd, knew, vnew, slopes, ck, cv)


def _mlstm_kernel(q_ref, k_ref, v_ref, if_ref, o_ref, g_ref, bias_ref, og_ref, c0_ref, m0_ref,
                  y_ref, cout_ref, mout_ref, c_sc, m_sc, *, tc, L):
    ti = pl.program_id(1)
    nt = pl.num_programs(1)

    @pl.when(ti == 0)
    def _():
        c_sc[...] = c0_ref[0]
        m_sc[...] = m0_ref[0]

    gates = if_ref[...] + bias_ref[...]
    lf = jnp.minimum(gates, 0.0) - jnp.log1p(jnp.exp(-jnp.abs(gates)))
    tri = _lane_iota((L, L)) <= _row_iota((L, L))
    tri_f = jnp.where(tri, 1.0, 0.0)
    lane = _lane_iota((L, LANES))
    og = og_ref[...]
    for c in range(tc // L):
        r0 = c * L
        fcum = jnp.dot(tri_f, lf[r0:r0 + L], preferred_element_type=F32, precision=HIGHEST)
        z = gates[r0:r0 + L] - pltpu.roll(fcum, LANES - 4, 1)
        qt = q_ref[r0:r0 + L, :]
        kt = k_ref[r0:r0 + L, :] * (HEAD_DIM ** -0.5)
        vt = v_ref[r0:r0 + L, :]
        ot = _sigmoid(o_ref[r0:r0 + L, :])
        for pair in range(C_HEADS // 2):
            ytile = jnp.zeros((L, LANES), F32)
            for h2 in range(2):
                h = 2 * pair + h2
                sl = slice(pair * LANES, (pair + 1) * LANES)
                qh, kh, vh, oh = qt[:, sl], kt[:, sl], vt[:, sl], ot[:, sl]
                if h2:
                    qh, kh, vh, oh = [pltpu.roll(a, 64, 1) for a in (qh, kh, vh, oh)]
                qh = jnp.where(lane < 64, qh, 0.0).astype(BF16)
                kh = jnp.where(lane < 64, kh, 0.0).astype(BF16)
                vaug = jnp.where(lane < 64, vh, jnp.where(lane == 64, 1.0, 0.0))
                m_prev = m_sc[h:h + 1, 0:1]
                f_col = fcum[:, 4 + h:5 + h]
                z_col = z[:, h:h + 1]
                pick = jnp.where(lane == h, 1.0, 0.0)
                z_row = _dot_nt(pick, z, precision=HIGHEST)
                d_log = jnp.where(tri, f_col + z_row, -jnp.inf)
                inter = f_col + m_prev
                m_t = jnp.maximum(jnp.max(d_log, axis=-1, keepdims=True), inter)
                w_intra = _dot_nt(qh, kh) * jnp.exp(d_log - m_t)
                w_inter = jnp.exp(inter - m_t)
                c_prev = c_sc[h]
                num = (w_inter * jnp.dot(qh, c_prev.astype(BF16), preferred_element_type=F32)
                       + jnp.dot(w_intra.astype(BF16), vaug.astype(BF16), preferred_element_type=F32))
                den = num[:, HEAD_DIM:HEAD_DIM + 1]
                hh = num / jnp.maximum(jnp.abs(den), jnp.exp(-m_t))
                f_tot = f_col[L - 1:L, :]
                w_log = f_tot + z_col
                m_new = jnp.maximum(f_tot + m_prev, jnp.max(w_log, axis=0, keepdims=True))
                decay = jnp.exp(f_tot + m_prev - m_new)
                w = jnp.exp(w_log - m_new)
                upd = lax.dot_general(kh, (w * vaug).astype(BF16), (((0,), (0,)), ((), ())),
                                      preferred_element_type=F32)
                c_sc[h] = decay * c_prev + upd
                m_sc[h:h + 1, :] = jnp.broadcast_to(m_new, (1, LANES))
                ho = jnp.where(lane < 64, hh * oh, 0.0)
                ms = jnp.sum(ho * ho, axis=-1, keepdims=True) * (1.0 / HEAD_DIM)
                yh = ho * lax.rsqrt(ms + NORM_EPS)
                ytile = ytile + (pltpu.roll(yh, 64, 1) if h2 else yh)
            sl = slice(pair * LANES, (pair + 1) * LANES)
            y_ref[r0:r0 + L, sl] = ytile * og[:, sl] * _silu(g_ref[r0:r0 + L, sl])

    @pl.when(ti == nt - 1)
    def _():
        cout_ref[0] = c_sc[...]
        mout_ref[0] = m_sc[...]


def _mlstm(u, gate_bias, onorm_g, c0aug, m0, bsz, t, tc, L, wide):
    nt = t // tc
    const = lambda b, i: (0, 0)
    ucol = lambda off, w: (lambda b, i: (b * nt + i, off // w))
    return pl.pallas_call(
        functools.partial(_mlstm_wide_kernel if wide else _mlstm_kernel, tc=tc, L=L),
        grid=(bsz, nt),
        in_specs=[
            pl.BlockSpec((tc, 256), ucol(U_CQ, 256)),
            pl.BlockSpec((tc, 256), ucol(U_CK, 256)),
            pl.BlockSpec((tc, 256), ucol(U_CV, 256)),
            pl.BlockSpec((tc, LANES), ucol(U_CIF, LANES)),
            pl.BlockSpec((tc, 256), ucol(U_CO, 256)),
            pl.BlockSpec((tc, 256), ucol(U_CG, 256)),
            pl.BlockSpec((1, LANES), const),
            pl.BlockSpec((1, 256), const),
            pl.BlockSpec((1, C_HEADS, LANES, LANES), lambda b, i: (b, 0, 0, 0)),
            pl.BlockSpec((1, SUBLANES, LANES), lambda b, i: (b, 0, 0)),
        ],
        out_specs=[
            pl.BlockSpec((tc, 256), lambda b, i: (b * nt + i, 0)),
            pl.BlockSpec((1, C_HEADS, LANES, LANES), lambda b, i: (b, 0, 0, 0)),
            pl.BlockSpec((1, SUBLANES, LANES), lambda b, i: (b, 0, 0)),
        ],
        out_shape=[
            jax.ShapeDtypeStruct((bsz * t, 256), F32),
            jax.ShapeDtypeStruct((bsz, C_HEADS, LANES, LANES), F32),
            jax.ShapeDtypeStruct((bsz, SUBLANES, LANES), F32),
        ],
        scratch_shapes=[
            pltpu.VMEM((C_HEADS, LANES, LANES), F32),
            pltpu.VMEM((SUBLANES, LANES), F32),
        ],
        compiler_params=_cparams(("parallel", "arbitrary")),
        name="mlstm",
    )(u, u, u, u, u, u, gate_bias, onorm_g, c0aug, m0)


def _mlstm_wide_kernel(q_ref, k_ref, v_ref, if_ref, o_ref, g_ref, bias_ref, og_ref, c0_ref, m0_ref,
                       y_ref, cout_ref, mout_ref, c_sc, m_sc, *, tc, L):
    ti = pl.program_id(1)
    nt = pl.num_programs(1)

    @pl.when(ti == 0)
    def _():
        c_sc[...] = c0_ref[0]
        m_sc[...] = m0_ref[0]

    gates = if_ref[...] + bias_ref[...]
    lf = jnp.minimum(gates, 0.0) - jnp.log1p(jnp.exp(-jnp.abs(gates)))
    in_chunk = _row_iota((tc, LANES)) & (L - 1)
    fcum = lf
    d = 1
    while d < L:
        fcum = fcum + jnp.where(in_chunk >= d, pltpu.roll(fcum, d, 0), 0.0)
        d *= 2
    z = gates - pltpu.roll(fcum, LANES - 4, 1)
    z_t = z.T
    k_t = (k_ref[...] * (HEAD_DIM ** -0.5)).T

    tri = _lane_iota((L, L)) <= _row_iota((L, L))
    lane = _lane_iota((L, LANES))
    srow = _row_iota((LANES, L))
    og = og_ref[...]
    for c in range(tc // L):
        r0 = c * L
        for pair in range(C_HEADS // 2):
            sl = slice(pair * LANES, (pair + 1) * LANES)
            qt = q_ref[r0:r0 + L, sl]
            vt = v_ref[r0:r0 + L, sl]
            ot = _sigmoid(o_ref[r0:r0 + L, sl])
            ktt = k_t[pair * LANES:(pair + 1) * LANES, r0:r0 + L]
            ytile = jnp.zeros((L, LANES), F32)
            for h2 in range(2):
                h = 2 * pair + h2
                mine = (lane >= 64) if h2 else (lane < 64)
                ones_col = 0 if h2 else HEAD_DIM
                qh = jnp.where(mine, qt, 0.0).astype(BF16)
                kth = jnp.where((srow >= 64) if h2 else (srow < 64), ktt, 0.0).astype(BF16)
                vaug = jnp.where(mine, vt, jnp.where(lane == ones_col, 1.0, 0.0))
                m_prev = m_sc[h:h + 1, 0:1]
                f_col = fcum[r0:r0 + L, 4 + h:5 + h]
                z_col = z[r0:r0 + L, h:h + 1]
                z_row = z_t[h:h + 1, r0:r0 + L]
                d_log = jnp.where(tri, f_col + z_row, -jnp.inf)
                inter = f_col + m_prev
                m_t = jnp.maximum(jnp.max(d_log, axis=-1, keepdims=True), inter)
                w_intra = jnp.dot(qh, kth, preferred_element_type=F32) * jnp.exp(d_log - m_t)
                w_inter = jnp.exp(inter - m_t)
                c_prev = c_sc[h]
                num = (w_inter * jnp.dot(qh, c_prev.astype(BF16), preferred_element_type=F32)
                       + jnp.dot(w_intra.astype(BF16), vaug.astype(BF16), preferred_element_type=F32))
                den = num[:, ones_col:ones_col + 1]
                hh = num / jnp.maximum(jnp.abs(den), jnp.exp(-m_t))
                f_tot = f_col[L - 1:L, :]
                w_log = f_tot + z_col
                m_new = jnp.maximum(f_tot + m_prev, jnp.max(w_log, axis=0, keepdims=True))
                decay = jnp.exp(f_tot + m_prev - m_new)
                w = jnp.exp(w_log - m_new)
                c_sc[h] = decay * c_prev + jnp.dot(kth, (w * vaug).astype(BF16), preferred_element_type=F32)
                m_sc[h:h + 1, :] = jnp.broadcast_to(m_new, (1, LANES))
                ho = jnp.where(mine, hh * ot, 0.0)
                ms = jnp.sum(ho * ho, axis=-1, keepdims=True) * (1.0 / HEAD_DIM)
                ytile = ytile + ho * lax.rsqrt(ms + NORM_EPS)
            y_ref[r0:r0 + L, sl] = ytile * og[:, sl] * _silu(g_ref[r0:r0 + L, sl])

    @pl.when(ti == nt - 1)
    def _():
        cout_ref[0] = c_sc[...]
        mout_ref[0] = m_sc[...]


def _outproj_kernel(x_ref, ya_ref, yb_ref, gb_ref, yc_ref, p_ref, wa_ref, wb_ref, wc_ref,
                    pg_ref, pp_ref, png_ref, o_ref):
    yb = yb_ref[...] * _silu(gb_ref[...])
    x = (x_ref[...]
         + jnp.dot(ya_ref[...].astype(BF16), wa_ref[...], preferred_element_type=F32)
         + jnp.dot(yb.astype(BF16), wb_ref[...], preferred_element_type=F32)
         + jnp.dot(yc_ref[...].astype(BF16), wc_ref[...], preferred_element_type=F32))
    gate = _sigmoid(jnp.dot(x.astype(BF16), pg_ref[...], preferred_element_type=F32))
    e = jnp.dot(p_ref[...].astype(BF16), pp_ref[...], preferred_element_type=F32)
    e = e * lax.rsqrt(jnp.mean(e * e, axis=-1, keepdims=True) + NORM_EPS) * png_ref[...]
    o_ref[...] = x + gate * e


def _outproj(x2d, ya, yb, u, yc, p_all, layer, wa, wb, wc, pg, pp, png, tm):
    m = x2d.shape[0]
    row = lambda i: (i, 0)
    p_row = lambda i: (layer * (m // tm) + i, 0)
    const = lambda i: (0, 0)
    return pl.pallas_call(
        _outproj_kernel,
        grid=(m // tm,),
        in_specs=[
            pl.BlockSpec((tm, D_MODEL), row),
            pl.BlockSpec((tm, 256), row),
            pl.BlockSpec((tm, 512), row),
            pl.BlockSpec((tm, 512), lambda i: (i, U_GB // 512)),
            pl.BlockSpec((tm, 256), row),
            pl.BlockSpec((tm, PLE_DIM), p_row),
            pl.BlockSpec((256, D_MODEL), const),
            pl.BlockSpec((512, D_MODEL), const),
            pl.BlockSpec((256, D_MODEL), const),
            pl.BlockSpec((D_MODEL, D_MODEL), const),
            pl.BlockSpec((PLE_DIM, D_MODEL), const),
            pl.BlockSpec((1, D_MODEL), const),
        ],
        out_specs=pl.BlockSpec((tm, D_MODEL), row),
        out_shape=jax.ShapeDtypeStruct((m, D_MODEL), F32),
        compiler_params=_cparams(("parallel",)),
        name="outproj",
    )(x2d, ya, yb, u, yc, p_all, wa, wb, wc, pg, pp, png)


def _block_diag(w):
    n, a, b = w.shape
    eye = jnp.eye(n, dtype=w.dtype)
    return jnp.einsum("nab,nm->namb", w, eye).reshape(n * a, n * b)


def _prep_layer(l, w):
    (norm_g, w_in, a_conv_w, a_conv_b, a_w_r, a_b_r, a_w_i, a_b_i, a_lambda, b_qnorm_g, b_knorm_g,
     c_b_i, c_b_f, c_onorm_g, w_out, ple_gate, ple_proj, ple_norm_g) = [a[l] for a in w]
    col = lambda i: w_in[:, _REF_OFFS[i]:_REF_OFFS[i + 1]]
    w_u = jnp.concatenate(
        [col(0), col(1), col(2), col(5), col(3), col(4), col(6), col(7), col(8), col(11), col(12),
         col(9), col(10), jnp.zeros((D_MODEL, LANES - 2 * C_HEADS), F32)], axis=1).astype(BF16)
    gate_bias = jnp.concatenate([c_b_i, c_b_f, jnp.zeros((LANES - 2 * C_HEADS,), F32)])[None, :]
    shift = 1.02 * HEAD_DIM ** 0.5 * jnp.max(jnp.abs(b_qnorm_g)) * jnp.max(jnp.abs(b_knorm_g)) + 0.5
    return dict(
        shift=shift,
        norm_g=norm_g[None, :], w_u=w_u,
        qg=jnp.tile(b_qnorm_g, B_HEADS)[None, :], kg=jnp.tile(b_knorm_g, B_KV_HEADS)[None, :],
        bd=_block_diag(jnp.ones((B_HEADS, HEAD_DIM, HEAD_DIM), BF16)),
        cw=a_conv_w, cb=a_conv_b[None, :],
        wri=jnp.concatenate([_block_diag(a_w_r), _block_diag(a_w_i)], axis=1).astype(BF16),
        bri=jnp.concatenate([a_b_r, a_b_i])[None, :], lam=a_lambda[None, :],
        gate_bias=gate_bias, onorm_g=c_onorm_g[None, :],
        wa=w_out[:256].astype(BF16), wb=w_out[256:768].astype(BF16), wc=w_out[768:].astype(BF16),
        pg=ple_gate.astype(BF16), pp=ple_proj.astype(BF16), png=ple_norm_g[None, :],
    )


def _pick_tile(n, cap):
    t = min(n, cap)
    while n % t:
        t //= 2
    return t


def _layer(x2d, p_all, layer, bsz, t, conv0, h0, c0, n0, m0, lw, attend):
    m = bsz * t
    tm = _pick_tile(m, 256)
    proj = _inproj(x2d, lw["norm_g"], lw["w_u"], lw["qg"], lw["kg"], lw["bd"], tm, bsz, t)
    u, k2d, v2d = proj[:3]
    if len(proj) > 3:
        k_state, v_state = [a.reshape(bsz, B_KV_HEADS, HEAD_DIM, t).transpose(0, 3, 1, 2) for a in proj[3:]]
    else:
        k_state, v_state = [a.reshape(bsz, t, B_KV_HEADS, HEAD_DIM) for a in (k2d, v2d)]
    ya, conv_new, h_last = _rglru(u, conv0, h0[:, None, :], lw["cw"], lw["cb"], lw["wri"], lw["bri"],
                                  lw["lam"], bsz, t, _pick_tile(t, 512))
    yb = attend(u, k2d, v2d)
    wide = t % LANES == 0
    L = LANES if wide else int(np.gcd(t, MLSTM_CHUNK))
    lo, hi = slice(0, HEAD_DIM), slice(HEAD_DIM, 2 * HEAD_DIM)
    c0aug = jnp.zeros((bsz, C_HEADS, LANES, LANES), F32)
    if wide:
        c0aug = c0aug.at[:, 0::2, lo, lo].set(c0[:, 0::2]).at[:, 0::2, lo, HEAD_DIM].set(n0[:, 0::2])
        c0aug = c0aug.at[:, 1::2, hi, hi].set(c0[:, 1::2]).at[:, 1::2, hi, 0].set(n0[:, 1::2])
    else:
        c0aug = c0aug.at[:, :, lo, lo].set(c0).at[:, :, lo, HEAD_DIM].set(n0)
    m0b = jnp.zeros((bsz, SUBLANES, LANES), F32).at[:, :C_HEADS, :].set(
        jnp.broadcast_to(m0[:, :, None], (bsz, C_HEADS, LANES)))
    yc, caug, mout = _mlstm(u, lw["gate_bias"], lw["onorm_g"], c0aug, m0b, bsz, t, _pick_tile(t, 512), L, wide)
    if wide:
        c_new = jnp.stack([caug[:, 0, lo, lo], caug[:, 1, hi, hi], caug[:, 2, lo, lo], caug[:, 3, hi, hi]], axis=1)
        n_new = jnp.stack([caug[:, 0, lo, HEAD_DIM], caug[:, 1, hi, 0],
                           caug[:, 2, lo, HEAD_DIM], caug[:, 3, hi, 0]], axis=1)
    else:
        c_new, n_new = caug[:, :, lo, lo], caug[:, :, lo, HEAD_DIM]
    xo = _outproj(x2d, ya, yb, u, yc, p_all, layer, lw["wa"], lw["wb"], lw["wc"], lw["pg"], lw["pp"], lw["png"], tm)
    state = (k_state, v_state, conv_new, h_last[:, 0, :], c_new, n_new, mout[:, :C_HEADS, 0])
    return xo, state


def _attend_prompt(u, k2d, v2d, bsz, s, shift):
    nb = s // MOBA_BLOCK
    assert s % MOBA_BLOCK == 0 and nb <= NB_MAX
    kmean = _kmean(k2d, _pick_tile(bsz * nb, 8)).reshape(bsz, nb, B_KV_HEADS, HEAD_DIM)
    kmh = jnp.repeat(kmean, B_HEADS // B_KV_HEADS, axis=2).transpose(0, 2, 1, 3)
    kmh = jnp.pad(kmh, ((0, 0), (0, 0), (0, NB_MAX - nb), (0, 0)))
    wsel = jnp.einsum("bhjd,hg->bhjgd", kmh, jnp.eye(B_HEADS, dtype=F32)).reshape(bsz, B_HEADS * NB_MAX, 512)
    qa, ka, va = _select(u, k2d, v2d, wsel, jnp.full((1, LANES), shift, F32), bsz, s, _pick_tile(s, 512))
    mode = (shift <= SHIFT_LIMIT).astype(jnp.int32).reshape(1)
    return _attn(mode, qa, ka, va, bsz, s)


def _attend_sample(u, k2d, v2d, bsz, t, pt, ck, cv, slopes):
    n_pages = pt.shape[1]
    assert t == 8 and n_pages % PAGES_PER_STEP == 0
    nblk = n_pages * PAGE_SIZE // MOBA_BLOCK
    assert nblk == 64
    q = u[:, U_QB:U_QB + 512].reshape(bsz, t, B_KV_HEADS, 2, HEAD_DIM).transpose(0, 2, 3, 1, 4)
    qbd = jnp.einsum("bghtd,gk->bghtkd", q, jnp.eye(B_KV_HEADS, dtype=F32)).reshape(bsz, 64, 256)
    pad = ((0, 0), (0, LANES - t), (0, 0))
    knew = jnp.pad(k2d.reshape(bsz, t, 256), pad)
    vnew = jnp.pad(v2d.reshape(bsz, t, 256), pad)
    o = _sample_moba(pt, qbd, knew, vnew, slopes, ck, cv, bsz, n_pages, t)
    return o.reshape(bsz, B_HEADS, t, HEAD_DIM).transpose(0, 2, 1, 3).reshape(bsz * t, 512)


def kernel(x_prompt, x_sample, cache_k, cache_v, state_rglru_conv, state_rglru_h, state_mlstm_C, state_mlstm_n, state_mlstm_m, page_table, p_prompt, p_sample, norm_g, w_in, a_conv_w, a_conv_b, a_w_r, a_b_r, a_w_i, a_b_i, a_lambda, b_qnorm_g, b_knorm_g, c_b_i, c_b_f, c_onorm_g, w_out, ple_gate, ple_proj, ple_norm_g):
    weights = (norm_g, w_in, a_conv_w, a_conv_b, a_w_r, a_b_r, a_w_i, a_b_i, a_lambda, b_qnorm_g, b_knorm_g,
               c_b_i, c_b_f, c_onorm_g, w_out, ple_gate, ple_proj, ple_norm_g)
    depth = w_in.shape[0]
    bp, sp = x_prompt.shape[:2]
    bs, ts = x_sample.shape[:2]
    n_phys = cache_k.shape[1]
    ck = cache_k.transpose(0, 1, 3, 4, 2).reshape(depth * n_phys, 256, PAGE_SIZE)
    cv = cache_v.transpose(0, 1, 3, 4, 2).reshape(depth * n_phys, 256, PAGE_SIZE)
    slopes = jnp.asarray(np.repeat(2.0 ** -(np.arange(1, B_HEADS + 1)), ts)[:, None]
                         * np.ones((1, LANES)), F32)

    xp = x_prompt.reshape(bp * sp, D_MODEL)
    pp_all = p_prompt.reshape(depth * bp * sp, PLE_DIM)
    ps_all = p_sample.reshape(depth * bs * ts, PLE_DIM)
    xs = x_sample.reshape(bs * ts, D_MODEL)
    zeros = lambda *s: jnp.zeros(s, F32)
    sp_all, ss_all = [], []
    for l in range(depth):
        lw = _prep_layer(l, weights)
        xp, st_p = _layer(xp, pp_all, l, bp, sp,
                          zeros(bp, CONV_WIDTH - 1, A_WIDTH), zeros(bp, A_WIDTH),
                          zeros(bp, C_HEADS, HEAD_DIM, HEAD_DIM), zeros(bp, C_HEADS, HEAD_DIM), zeros(bp, C_HEADS),
                          lw, functools.partial(_attend_prompt, bsz=bp, s=sp, shift=lw["shift"]))
        attend_s = functools.partial(_attend_sample, bsz=bs, t=ts, pt=page_table + l * n_phys,
                                     ck=ck, cv=cv, slopes=slopes)
        xs, st_s = _layer(xs, ps_all, l, bs, ts,
                          state_rglru_conv[l], state_rglru_h[l], state_mlstm_C[l], state_mlstm_n[l],
                          state_mlstm_m[l], lw, attend_s)
        sp_all.append(st_p)
        ss_all.append(st_s)

    stk = lambda states, j: jnp.stack([s_[j] for s_ in states])
    return (xp.reshape(bp, sp, D_MODEL), xs.reshape(bs, ts, D_MODEL),
            stk(sp_all, 0), stk(sp_all, 1), stk(ss_all, 0), stk(ss_all, 1),
            stk(sp_all, 2), stk(ss_all, 2), stk(sp_all, 3), stk(ss_all, 3),
            stk(sp_all, 4), stk(ss_all, 4), stk(sp_all, 5), stk(ss_all, 5),
            stk(sp_all, 6), stk(ss_all, 6))
```

```python
import functools

import numpy as np
import jax
import jax.numpy as jnp
from jax import lax
from jax.experimental import pallas as pl
from jax.experimental.pallas import tpu as pltpu

F32 = jnp.float32
BF16 = jnp.bfloat16
HIGHEST = lax.Precision.HIGHEST

LANES = 128
SUBLANES = 8

HEAD_DIM = 64
D_MODEL = 1024
A_WIDTH = 256
CONV_WIDTH = 4
RG_C = 8.0
B_HEADS = 8
B_KV_HEADS = 4
C_HEADS = 4
MOBA_BLOCK = 256
MOBA_TOPK = 3
PAGE_SIZE = 128
MLSTM_CHUNK = 64
PLE_DIM = 256
NORM_EPS = 1e-6
NB_MAX = 32
NEG = -1e30

U_XA, U_GA = 0, 256
U_QB, U_GB, U_KB, U_VB = 512, 1024, 1536, 1792
U_CQ, U_CK, U_CV, U_CO, U_CG, U_CIF = 2048, 2304, 2560, 2816, 3072, 3328
D_U = 3456
_REF_SPLITS = (256, 256, 512, 256, 256, 512, 256, 256, 256, 4, 4, 256, 256)
_REF_OFFS = np.concatenate([[0], np.cumsum(_REF_SPLITS)]).tolist()

VMEM_LIMIT = 48 * 1024 * 1024


def _cparams(sem):
    return pltpu.CompilerParams(dimension_semantics=sem, vmem_limit_bytes=VMEM_LIMIT)


def _lane_iota(shape):
    return lax.broadcasted_iota(jnp.int32, shape, len(shape) - 1)


def _row_iota(shape):
    return lax.broadcasted_iota(jnp.int32, shape, len(shape) - 2)


def _sigmoid(x):
    return 1.0 / (1.0 + jnp.exp(-x))


def _silu(x):
    return x * _sigmoid(x)


def _softplus(x):
    return jnp.maximum(x, 0.0) + jnp.log1p(jnp.exp(-jnp.abs(x)))


def _dot_nt(a, b, **kw):
    return lax.dot_general(a, b, (((1,), (1,)), ((), ())), preferred_element_type=F32, **kw)


def _inproj_kernel(x_ref, g_ref, w_ref, qg_ref, kg_ref, bd_ref, u_ref, k_ref, v_ref, *t_refs):
    x = x_ref[...]
    xn = x * lax.rsqrt(jnp.mean(x * x, axis=-1, keepdims=True) + NORM_EPS) * g_ref[...]
    u = jnp.dot(xn.astype(BF16), w_ref[...], preferred_element_type=F32)
    u_ref[...] = u
    bd = bd_ref[...]
    q = u[:, U_QB:U_QB + 512]
    qss = jnp.dot((q * q).astype(BF16), bd, preferred_element_type=F32) * (1.0 / HEAD_DIM)
    qn = q * lax.rsqrt(qss + NORM_EPS) * qg_ref[...] * (HEAD_DIM ** -0.5)
    u_ref[:, U_QB:U_QB + 512] = qn
    k = u[:, U_KB:U_KB + 256]
    kss = jnp.dot((k * k).astype(BF16), bd[:256, :256], preferred_element_type=F32) * (1.0 / HEAD_DIM)
    kn = k * lax.rsqrt(kss + NORM_EPS) * kg_ref[...]
    u_ref[:, U_KB:U_KB + 256] = kn
    k_ref[...] = kn
    v = u[:, U_VB:U_VB + 256]
    v_ref[...] = v
    if t_refs:
        kt_ref, vt_ref = t_refs
        kt_ref[0] = kn.T
        vt_ref[0] = v.T


def _inproj(x2d, norm_g, w_u, qg, kg, bd, tm, bsz, t):
    m = x2d.shape[0]
    const = lambda i: (0, 0)
    transposed = t % tm == 0
    nt = t // tm if transposed else 1
    t_specs = [pl.BlockSpec((1, 256, tm), lambda i: (i // nt, 0, i % nt))] * 2 if transposed else []
    t_shapes = [jax.ShapeDtypeStruct((bsz, 256, t), F32)] * 2 if transposed else []
    return pl.pallas_call(
        _inproj_kernel,
        grid=(m // tm,),
        in_specs=[
            pl.BlockSpec((tm, D_MODEL), lambda i: (i, 0)),
            pl.BlockSpec((1, D_MODEL), const),
            pl.BlockSpec((D_MODEL, D_U), const),
            pl.BlockSpec((1, 512), const),
            pl.BlockSpec((1, 256), const),
            pl.BlockSpec((512, 512), const),
        ],
        out_specs=[
            pl.BlockSpec((tm, D_U), lambda i: (i, 0)),
            pl.BlockSpec((tm, 256), lambda i: (i, 0)),
            pl.BlockSpec((tm, 256), lambda i: (i, 0)),
        ] + t_specs,
        out_shape=[
            jax.ShapeDtypeStruct((m, D_U), F32),
            jax.ShapeDtypeStruct((m, 256), F32),
            jax.ShapeDtypeStruct((m, 256), F32),
        ] + t_shapes,
        compiler_params=_cparams(("parallel",)),
        name="inproj",
    )(x2d, norm_g, w_u, qg, kg, bd)


def _rglru_kernel(xa_ref, ga_ref, conv0_ref, h0_ref, cw_ref, cb_ref, wri_ref, bri_ref, lam_ref,
                  ya_ref, convn_ref, hl_ref, xbuf, abuf, bbuf, hbuf, hcar, *, tt):
    ti = pl.program_id(1)
    nt = pl.num_programs(1)

    @pl.when(ti == 0)
    def _():
        xbuf[5:8, :] = conv0_ref[0]
        hcar[...] = jnp.broadcast_to(h0_ref[0], (SUBLANES, A_WIDTH))

    xa = xa_ref[...]
    xbuf[8:8 + tt, :] = xa
    cw = cw_ref[...]
    xc = jnp.broadcast_to(cb_ref[...], (tt, A_WIDTH))
    for j in range(CONV_WIDTH):
        xc = xc + xbuf[5 + j:5 + j + tt, :] * cw[j:j + 1, :]
    xbuf[5:8, :] = xa[tt - 3:tt, :]

    gates = jnp.dot(xc.astype(BF16), wri_ref[...], preferred_element_type=F32) + bri_ref[...]
    r = _sigmoid(gates[:, :A_WIDTH])
    ig = _sigmoid(gates[:, A_WIDTH:])
    log_a = (-RG_C) * r * _softplus(-lam_ref[...])
    a = jnp.exp(log_a)
    abuf[...] = a
    bbuf[...] = jnp.sqrt(-jnp.tanh(log_a) * (a * a + 1.0)) * (ig * xc)

    rows = _row_iota((SUBLANES, A_WIDTH))

    def group(gi, carry):
        r0 = pl.multiple_of(gi * SUBLANES, SUBLANES)
        a = abuf[pl.ds(r0, SUBLANES), :]
        b = bbuf[pl.ds(r0, SUBLANES), :]
        for d in (1, 2, 4):
            a_sh = pltpu.roll(a, d, 0)
            b_sh = pltpu.roll(b, d, 0)
            keep = rows >= d
            b = jnp.where(keep, a * b_sh + b, b)
            a = jnp.where(keep, a * a_sh, a)
        h = a * hcar[...] + b
        hbuf[pl.ds(r0, SUBLANES), :] = h
        hcar[...] = jnp.broadcast_to(h[SUBLANES - 1:SUBLANES, :], (SUBLANES, A_WIDTH))
        return carry

    lax.fori_loop(0, tt // SUBLANES, group, 0)

    ya_ref[...] = hbuf[...] * _silu(ga_ref[...])

    @pl.when(ti == nt - 1)
    def _():
        convn_ref[0] = xa[tt - 3:tt, :]
        hl_ref[0] = hcar[0:1, :]


def _rglru(u, conv0, h0, cw, cb, wri, bri, lam, bsz, t, tt):
    nt = t // tt
    kern = functools.partial(_rglru_kernel, tt=tt)
    const = lambda b, i: (0, 0)
    return pl.pallas_call(
        kern,
        grid=(bsz, nt),
        in_specs=[
            pl.BlockSpec((tt, A_WIDTH), lambda b, i: (b * nt + i, U_XA // A_WIDTH)),
            pl.BlockSpec((tt, A_WIDTH), lambda b, i: (b * nt + i, U_GA // A_WIDTH)),
            pl.BlockSpec((1, 3, A_WIDTH), lambda b, i: (b, 0, 0)),
            pl.BlockSpec((1, 1, A_WIDTH), lambda b, i: (b, 0, 0)),
            pl.BlockSpec((CONV_WIDTH, A_WIDTH), const),
            pl.BlockSpec((1, A_WIDTH), const),
            pl.BlockSpec((A_WIDTH, 2 * A_WIDTH), const),
            pl.BlockSpec((1, 2 * A_WIDTH), const),
            pl.BlockSpec((1, A_WIDTH), const),
        ],
        out_specs=[
            pl.BlockSpec((tt, A_WIDTH), lambda b, i: (b * nt + i, 0)),
            pl.BlockSpec((1, 3, A_WIDTH), lambda b, i: (b, 0, 0)),
            pl.BlockSpec((1, 1, A_WIDTH), lambda b, i: (b, 0, 0)),
        ],
        out_shape=[
            jax.ShapeDtypeStruct((bsz * t, A_WIDTH), F32),
            jax.ShapeDtypeStruct((bsz, 3, A_WIDTH), F32),
            jax.ShapeDtypeStruct((bsz, 1, A_WIDTH), F32),
        ],
        scratch_shapes=[
            pltpu.VMEM((tt + 8, A_WIDTH), F32),
            pltpu.VMEM((tt, A_WIDTH), F32),
            pltpu.VMEM((tt, A_WIDTH), F32),
            pltpu.VMEM((tt, A_WIDTH), F32),
            pltpu.VMEM((SUBLANES, A_WIDTH), F32),
        ],
        compiler_params=_cparams(("parallel", "arbitrary")),
        name="rglru",
    )(u, u, conv0, h0, cw, cb, wri, bri, lam)


def _kmean_kernel(k_ref, o_ref, *, nblk):
    k = k_ref[...].reshape(nblk, MOBA_BLOCK, 256)
    o_ref[...] = jnp.sum(k, axis=1) * (1.0 / MOBA_BLOCK)


def _kmean(k2d, nblk):
    m = k2d.shape[0]
    rows = nblk * MOBA_BLOCK
    return pl.pallas_call(
        functools.partial(_kmean_kernel, nblk=nblk),
        grid=(m // rows,),
        in_specs=[pl.BlockSpec((rows, 256), lambda i: (i, 0))],
        out_specs=pl.BlockSpec((nblk, 256), lambda i: (i, 0)),
        out_shape=jax.ShapeDtypeStruct((m // MOBA_BLOCK, 256), F32),
        compiler_params=_cparams(("parallel",)),
        name="kmean",
    )(k2d)


def _select_kernel(q_ref, k_ref, v_ref, w_ref, shift_ref, qa_ref, ka_ref, va_ref, *, ts, rs):
    tile0 = pl.program_id(1) * ts
    w = w_ref[0]
    neg_shift = -shift_ref[...]

    def sub_tile(si, carry):
        r0 = pl.multiple_of(si * rs, rs)
        t0 = tile0 + r0
        q = q_ref[pl.ds(r0, rs), :]
        sc = _dot_nt(w, q, precision=HIGHEST)
        shp = (B_HEADS * NB_MAX, rs)
        j = _row_iota(shp) & (NB_MAX - 1)
        n_full = (t0 + _lane_iota(shp)) >> 8
        valid = j < n_full
        scm = jnp.where(valid, sc, -jnp.inf)
        ranks = []
        for h in range(B_HEADS):
            grp = scm[h * NB_MAX:(h + 1) * NB_MAX]
            jg = j[h * NB_MAX:(h + 1) * NB_MAX]
            rank = jnp.zeros((NB_MAX, rs), jnp.int32)
            for jp in range(NB_MAX):
                p = jnp.broadcast_to(grp[jp:jp + 1, :], (NB_MAX, rs))
                rank = rank + jnp.where(jg > jp, jnp.where(p >= grp, 1, 0), jnp.where(p > grp, 1, 0))
            ranks.append(rank)
        rank = jnp.concatenate(ranks, axis=0)
        allowed = (valid & (rank < MOBA_TOPK)) | (j == n_full)
        bias = jnp.where(allowed, 0.0, NEG).T

        shp1 = (rs, LANES)
        lane = _lane_iota(shp1)
        pos1 = t0 + _row_iota(shp1)
        pos_hi = (pos1 >> 6).astype(F32)
        pos_lo = (pos1 & 63).astype(F32)
        blk_onehot = jnp.where((lane - 64) == (pos1 >> 8), 1.0, 0.0)
        kfeat = jnp.where(lane == 96, pos_hi, jnp.where(lane == 97, pos_lo,
                          jnp.where((lane >= 98) & (lane <= 100), 1.0, 0.0)))
        kextra = jnp.where(lane < 96, blk_onehot, kfeat)
        vextra = jnp.where(lane == 64, 1.0, 0.0)
        k = k_ref[pl.ds(r0, rs), :]
        v = v_ref[pl.ds(r0, rs), :]
        for g in range(B_KV_HEADS):
            kt = k[:, (g // 2) * LANES:(g // 2 + 1) * LANES]
            vt = v[:, (g // 2) * LANES:(g // 2 + 1) * LANES]
            if g % 2:
                kt = pltpu.roll(kt, 64, 1)
                vt = pltpu.roll(vt, 64, 1)
            ka_ref[0, g, pl.ds(r0, rs), :] = jnp.where(lane < 64, kt, kextra).astype(BF16)
            va_ref[0, g, pl.ds(r0, rs), :] = jnp.where(lane < 64, vt, vextra).astype(BF16)
            qt = q[:, g * LANES:(g + 1) * LANES]
            for h2 in range(2):
                h = 2 * g + h2
                base = qt if h2 == 0 else pltpu.roll(qt, 64, 1)
                bt = bias[:, (h // 4) * LANES:(h // 4 + 1) * LANES]
                sh = (64 - (h % 4) * NB_MAX) % LANES
                if sh:
                    bt = pltpu.roll(bt, sh, 1)
                slope = 2.0 ** (-(h + 1))
                qfeat = jnp.where(lane == 96, slope * 64.0, jnp.where(lane == 97, slope,
                                  jnp.where(lane == 98, (-slope * 64.0) * pos_hi,
                                            jnp.where(lane == 99, (-slope) * pos_lo,
                                                      jnp.where(lane == 100, neg_shift, 0.0)))))
                aug = jnp.where(lane < 64, base, jnp.where(lane < 96, bt, qfeat))
                qa_ref[0, h, pl.ds(r0, rs), :] = aug.astype(BF16)
        return carry

    lax.fori_loop(0, ts // rs, sub_tile, 0)


def _select(u, k2d, v2d, wsel, shift, bsz, s, ts):
    nt = s // ts
    return pl.pallas_call(
        functools.partial(_select_kernel, ts=ts, rs=min(ts, LANES)),
        grid=(bsz, nt),
        in_specs=[
            pl.BlockSpec((ts, 512), lambda b, i: (b * nt + i, U_QB // 512)),
            pl.BlockSpec((ts, 256), lambda b, i: (b * nt + i, 0)),
            pl.BlockSpec((ts, 256), lambda b, i: (b * nt + i, 0)),
            pl.BlockSpec((1, B_HEADS * NB_MAX, 512), lambda b, i: (b, 0, 0)),
            pl.BlockSpec((1, LANES), lambda b, i: (0, 0)),
        ],
        out_specs=[
            pl.BlockSpec((1, B_HEADS, ts, LANES), lambda b, i: (b, 0, i, 0)),
            pl.BlockSpec((1, B_KV_HEADS, ts, LANES), lambda b, i: (b, 0, i, 0)),
            pl.BlockSpec((1, B_KV_HEADS, ts, LANES), lambda b, i: (b, 0, i, 0)),
        ],
        out_shape=[
            jax.ShapeDtypeStruct((bsz, B_HEADS, s, LANES), BF16),
            jax.ShapeDtypeStruct((bsz, B_KV_HEADS, s, LANES), BF16),
            jax.ShapeDtypeStruct((bsz, B_KV_HEADS, s, LANES), BF16),
        ],
        compiler_params=_cparams(("parallel", "parallel")),
        name="moba_select",
    )(u, k2d, v2d, wsel, shift)


ATTN_KV_BLOCKS = 4
ATTN_GROUPS = 2
SHIFT_LIMIT = 30.0


def _attn_finish(o_ref, acc, gi, tq):
    res = acc / acc[:, HEAD_DIM:HEAD_DIM + 1]
    lane = _lane_iota((tq, LANES))
    o_ref[:, gi * LANES:(gi + 1) * LANES] = jnp.where(lane < 64, res[:tq], pltpu.roll(res[tq:], 64, 1))


def _attn_online(qa_ref, ka_ref, va_ref, o_ref, m_sc, acc_sc, tq):
    qi = pl.program_id(2)
    rows = 2 * tq
    for gi in range(ATTN_GROUPS):
        q = qa_ref[0, 2 * gi:2 * gi + 2].reshape(rows, LANES)

        k0 = ka_ref[0, gi, pl.ds(pl.multiple_of(qi * tq, tq), tq), :]
        v0 = va_ref[0, gi, pl.ds(pl.multiple_of(qi * tq, tq), tq), :]
        s = _dot_nt(q, k0)
        shp = (rows, tq)
        qpos = _row_iota(shp) & (tq - 1)
        s = jnp.where(_lane_iota(shp) <= qpos, s, NEG)
        m0 = jnp.max(s, axis=-1, keepdims=True)
        p = jnp.exp(s - m0)
        m_sc[gi] = jnp.broadcast_to(m0, (rows, LANES))
        acc_sc[gi] = jnp.dot(p.astype(BF16), v0, preferred_element_type=F32)

        def body(jb, carry, gi=gi, q=q):
            r0 = pl.multiple_of(jb * tq, tq)
            kj = ka_ref[0, gi, pl.ds(r0, tq), :]
            vj = va_ref[0, gi, pl.ds(r0, tq), :]
            sj = _dot_nt(q, kj)
            m_prev = m_sc[gi]
            m_new = jnp.maximum(m_prev, jnp.max(sj, axis=-1, keepdims=True))
            alpha = jnp.exp(m_prev - m_new)
            pj = jnp.exp(sj - m_new[:, 0:1])
            acc_sc[gi] = alpha * acc_sc[gi] + jnp.dot(pj.astype(BF16), vj, preferred_element_type=F32)
            m_sc[gi] = m_new
            return carry

        lax.fori_loop(0, qi, body, 0)
        _attn_finish(o_ref, acc_sc[gi], gi, tq)


def _attn_shifted(qa_ref, ka_ref, va_ref, o_ref, acc_sc, tq):
    qi = pl.program_id(2)
    rows = 2 * tq
    keys = ATTN_KV_BLOCKS * tq
    qs = [qa_ref[0, 2 * gi:2 * gi + 2].reshape(rows, LANES) for gi in range(ATTN_GROUPS)]
    shp = (rows, keys)
    qpos = qi * tq + (_row_iota(shp) & (tq - 1))
    kofs = _lane_iota(shp)
    acc_sc[...] = jnp.zeros((ATTN_GROUPS, rows, LANES), F32)

    def body(jp, carry):
        r0 = pl.multiple_of(jp * keys, keys)
        causal = r0 + kofs <= qpos
        for gi in range(ATTN_GROUPS):
            kj = ka_ref[0, gi, pl.ds(r0, keys), :]
            vj = va_ref[0, gi, pl.ds(r0, keys), :]
            p = jnp.exp(jnp.where(causal, _dot_nt(qs[gi], kj), NEG))
            acc_sc[gi] += jnp.dot(p.astype(BF16), vj, preferred_element_type=F32)
        return carry

    lax.fori_loop(0, (qi + ATTN_KV_BLOCKS) // ATTN_KV_BLOCKS, body, 0)
    for gi in range(ATTN_GROUPS):
        _attn_finish(o_ref, acc_sc[gi], gi, tq)


def _attn_kernel(mode_ref, qa_ref, ka_ref, va_ref, o_ref, m_sc, acc_sc, *, tq):
    @pl.when(mode_ref[0] == 1)
    def _():
        _attn_shifted(qa_ref, ka_ref, va_ref, o_ref, acc_sc, tq)

    @pl.when(mode_ref[0] != 1)
    def _():
        _attn_online(qa_ref, ka_ref, va_ref, o_ref, m_sc, acc_sc, tq)


def _attn(mode, qa, ka, va, bsz, s):
    tq = MOBA_BLOCK
    nq = s // tq
    ng = ATTN_GROUPS
    assert nq % ATTN_KV_BLOCKS == 0 and B_KV_HEADS % ng == 0
    grid_spec = pltpu.PrefetchScalarGridSpec(
        num_scalar_prefetch=1,
        grid=(bsz, B_KV_HEADS // ng, nq),
        in_specs=[
            pl.BlockSpec((1, 2 * ng, tq, LANES), lambda b, g, i, mode: (b, g, i, 0)),
            pl.BlockSpec((1, ng, s, LANES), lambda b, g, i, mode: (b, g, 0, 0)),
            pl.BlockSpec((1, ng, s, LANES), lambda b, g, i, mode: (b, g, 0, 0)),
        ],
        out_specs=pl.BlockSpec((tq, ng * LANES), lambda b, g, i, mode: (b * nq + i, g)),
        scratch_shapes=[
            pltpu.VMEM((ng, 2 * tq, LANES), F32),
            pltpu.VMEM((ng, 2 * tq, LANES), F32),
        ],
    )
    return pl.pallas_call(
        functools.partial(_attn_kernel, tq=tq),
        grid_spec=grid_spec,
        out_shape=jax.ShapeDtypeStruct((bsz * s, 512), F32),
        compiler_params=_cparams(("parallel", "parallel", "arbitrary")),
        name="moba_attn",
    )(mode, qa, ka, va)


PAGES_PER_STEP = 16


def _page_copies(pt_ref, src_hbm, buf, sem, b, c, slot):
    return [pltpu.make_async_copy(src_hbm.at[pt_ref[b, c * PAGES_PER_STEP + p]], buf.at[slot, p], sem.at[slot])
            for p in range(PAGES_PER_STEP)]


def _sample_moba_kernel(pt_ref, q_ref, kn_ref, vn_ref, slope_ref, ck_hbm, cv_hbm, o_ref,
                        kst, vst, kres, kmt, ksem, vsem, *, nseq, nch, nblk, tnew, past):
    b = pl.program_id(0)
    cur = b % 2
    nxt = 1 - cur
    has_next = b + 1 < nseq
    rows = 64
    blocks = PAGES_PER_STEP * PAGE_SIZE // MOBA_BLOCK
    ppb = MOBA_BLOCK // PAGE_SIZE
    km_lane = _lane_iota((256, LANES)) & (nblk - 1)

    def k_copies(seq, c, st):
        return _page_copies(pt_ref, ck_hbm, kst, ksem, seq, c, st)

    def v_copies(c, st):
        return _page_copies(pt_ref, cv_hbm, vst, vsem, b, c, st)

    def stage_k_chunk(res_slot, c, st):
        kres[res_slot, pl.ds(c * PAGES_PER_STEP, PAGES_PER_STEP)] = kst[st].astype(BF16)
        tile = jnp.zeros((256, LANES), F32)
        for jj in range(blocks):
            x = kst[st, ppb * jj]
            for pp in range(1, ppb):
                x = x + kst[st, ppb * jj + pp]
            mean = jnp.sum(x, axis=-1, keepdims=True) * (1.0 / MOBA_BLOCK)
            tile = jnp.where(km_lane == c * blocks + jj, mean, tile)
        if c == 0:
            kmt[res_slot] = tile
        else:
            kmt[res_slot] = kmt[res_slot] + tile

    @pl.when(b == 0)
    def _():
        for cp in k_copies(0, 0, 0):
            cp.start()
        for c in range(nch):
            st = c % 2
            if c + 1 < nch:
                for cp in k_copies(0, c + 1, 1 - st):
                    cp.start()
            for cp in k_copies(0, c, st):
                cp.wait()
            stage_k_chunk(0, c, st)

    @pl.when(has_next)
    def _():
        for cp in k_copies(b + 1, 0, 0):
            cp.start()

    for cp in v_copies(0, 0):
        cp.start()

    qf = q_ref[0]
    qb = qf.astype(BF16)
    slope = slope_ref[...]
    shp = (rows, LANES)
    lane = _lane_iota(shp)
    tok = _row_iota(shp) & (tnew - 1)

    sc = jnp.dot(qf, kmt[cur], preferred_element_type=F32, precision=HIGHEST)
    j = lane & (nblk - 1)
    rank = jnp.zeros(shp, jnp.int32)
    for s in range(1, nblk):
        a = pltpu.roll(sc, 2 * nblk - s, 1)
        wrapped = j >= nblk - s
        rank = rank + jnp.where(wrapped, jnp.where(a >= sc, 1, 0), jnp.where(a > sc, 1, 0))
    bias = jnp.where(rank < MOBA_TOPK, 0.0, NEG)

    s_own = _dot_nt(qb, kn_ref[0].astype(BF16))
    s_own = s_own - slope * (tok - lane).astype(F32)
    s_own = jnp.where((lane <= tok) & (lane < tnew), s_own, NEG)
    m = jnp.max(s_own, axis=-1, keepdims=True)
    p = jnp.exp(s_own - m)
    l = jnp.sum(p, axis=-1, keepdims=True)
    acc = jnp.dot(p.astype(BF16), vn_ref[0].astype(BF16), preferred_element_type=F32)

    keys = PAGES_PER_STEP * PAGE_SIZE
    blocks = keys // MOBA_BLOCK
    kshape = (rows, keys)
    klane = _lane_iota(kshape)
    qpos = past + (_row_iota(kshape) & (tnew - 1))
    slope_k = jnp.broadcast_to(slope[:, 0:1], kshape)
    for c in range(nch):
        st = c % 2
        if c + 1 < nch:
            for cp in v_copies(c + 1, 1 - st):
                cp.start()

            @pl.when(has_next)
            def _():
                for cp in k_copies(b + 1, c + 1, 1 - st):
                    cp.start()

        for cp in v_copies(c, st):
            cp.wait()
        vc = vst[st].astype(BF16)
        s_c = jnp.concatenate(
            [jnp.dot(qb, kres[cur, c * PAGES_PER_STEP + pg], preferred_element_type=F32)
             for pg in range(PAGES_PER_STEP)], axis=1)
        s_c = s_c - slope_k * (qpos - (c * keys + klane)).astype(F32)
        sel = jnp.concatenate(
            [jnp.broadcast_to(bias[:, c * blocks + jj:c * blocks + jj + 1], (rows, MOBA_BLOCK))
             for jj in range(blocks)], axis=1)
        s_c = s_c + sel
        m_new = jnp.maximum(m, jnp.max(s_c, axis=-1, keepdims=True))
        alpha = jnp.exp(m - m_new)
        p = jnp.exp(s_c - m_new)
        l = alpha * l + jnp.sum(p, axis=-1, keepdims=True)
        pb = p.astype(BF16)
        pv = _dot_nt(pb[:, 0:PAGE_SIZE], vc[0])
        for pg in range(1, PAGES_PER_STEP):
            pv = pv + _dot_nt(pb[:, pg * PAGE_SIZE:(pg + 1) * PAGE_SIZE], vc[pg])
        acc = alpha * acc + pv
        m = m_new

        @pl.when(has_next)
        def _():
            for cp in k_copies(b + 1, c, st):
                cp.wait()
            stage_k_chunk(nxt, c, st)

    res = acc / l
    oshape = (rows, 256)
    own = (_lane_iota(oshape) >> 6) == (_row_iota(oshape) >> 4)
    res = jnp.where(own, res, 0.0)
    folded = res + pltpu.roll(res, 64, 1) + pltpu.roll(res, 128, 1) + pltpu.roll(res, 192, 1)
    o_ref[0] = folded[:, :HEAD_DIM]


def _sample_moba(pt, qbd, knew, vnew, slopes, ck, cv, bsz, n_pages, tnew):
    nch = n_pages // PAGES_PER_STEP
    nblk = n_pages * PAGE_SIZE // MOBA_BLOCK
    past = n_pages * PAGE_SIZE
    grid_spec = pltpu.PrefetchScalarGridSpec(
        num_scalar_prefetch=1,
        grid=(bsz,),
        in_specs=[
            pl.BlockSpec((1, 64, 256), lambda b, pt: (b, 0, 0)),
            pl.BlockSpec((1, LANES, 256), lambda b, pt: (b, 0, 0)),
            pl.BlockSpec((1, LANES, 256), lambda b, pt: (b, 0, 0)),
            pl.BlockSpec((64, LANES), lambda b, pt: (0, 0)),
            pl.BlockSpec(memory_space=pl.ANY),
            pl.BlockSpec(memory_space=pl.ANY),
        ],
        out_specs=pl.BlockSpec((1, 64, HEAD_DIM), lambda b, pt: (b, 0, 0)),
        scratch_shapes=[
            pltpu.VMEM((2, PAGES_PER_STEP, 256, PAGE_SIZE), F32),
            pltpu.VMEM((2, PAGES_PER_STEP, 256, PAGE_SIZE), F32),
            pltpu.VMEM((2, n_pages, 256, PAGE_SIZE), BF16),
            pltpu.VMEM((2, 256, LANES), F32),
            pltpu.SemaphoreType.DMA((2,)),
            pltpu.SemaphoreType.DMA((2,)),
        ],
    )
    return pl.pallas_call(
        functools.partial(_sample_moba_kernel, nseq=bsz, nch=nch, nblk=nblk, tnew=tnew, past=past),
        grid_spec=grid_spec,
        out_shape=jax.ShapeDtypeStruct((bsz, 64, HEAD_DIM), F32),
        compiler_params=_cparams(("arbitrary",)),
        name="sample_moba",
    )(pt, qbd, knew, vnew, slopes, ck, cv)


def _mlstm_kernel(q_ref, k_ref, v_ref, if_ref, o_ref, g_ref, bias_ref, og_ref, c0_ref, m0_ref,
                  y_ref, cout_ref, mout_ref, c_sc, m_sc, *, tc, L):
    ti = pl.program_id(1)
    nt = pl.num_programs(1)

    @pl.when(ti == 0)
    def _():
        c_sc[...] = c0_ref[0]
        m_sc[...] = m0_ref[0]

    gates = if_ref[...] + bias_ref[...]
    lf = jnp.minimum(gates, 0.0) - jnp.log1p(jnp.exp(-jnp.abs(gates)))
    tri = _lane_iota((L, L)) <= _row_iota((L, L))
    tri_f = jnp.where(tri, 1.0, 0.0)
    lane = _lane_iota((L, LANES))
    og = og_ref[...]
    for c in range(tc // L):
        r0 = c * L
        fcum = jnp.dot(tri_f, lf[r0:r0 + L], preferred_element_type=F32, precision=HIGHEST)
        z = gates[r0:r0 + L] - pltpu.roll(fcum, LANES - 4, 1)
        qt = q_ref[r0:r0 + L, :]
        kt = k_ref[r0:r0 + L, :] * (HEAD_DIM ** -0.5)
        vt = v_ref[r0:r0 + L, :]
        ot = _sigmoid(o_ref[r0:r0 + L, :])
        for pair in range(C_HEADS // 2):
            ytile = jnp.zeros((L, LANES), F32)
            for h2 in range(2):
                h = 2 * pair + h2
                sl = slice(pair * LANES, (pair + 1) * LANES)
                qh, kh, vh, oh = qt[:, sl], kt[:, sl], vt[:, sl], ot[:, sl]
                if h2:
                    qh, kh, vh, oh = [pltpu.roll(a, 64, 1) for a in (qh, kh, vh, oh)]
                qh = jnp.where(lane < 64, qh, 0.0).astype(BF16)
                kh = jnp.where(lane < 64, kh, 0.0).astype(BF16)
                vaug = jnp.where(lane < 64, vh, jnp.where(lane == 64, 1.0, 0.0))
                m_prev = m_sc[h:h + 1, 0:1]
                f_col = fcum[:, 4 + h:5 + h]
                z_col = z[:, h:h + 1]
                pick = jnp.where(lane == h, 1.0, 0.0)
                z_row = _dot_nt(pick, z, precision=HIGHEST)
                d_log = jnp.where(tri, f_col + z_row, -jnp.inf)
                inter = f_col + m_prev
                m_t = jnp.maximum(jnp.max(d_log, axis=-1, keepdims=True), inter)
                w_intra = _dot_nt(qh, kh) * jnp.exp(d_log - m_t)
                w_inter = jnp.exp(inter - m_t)
                c_prev = c_sc[h]
                num = (w_inter * jnp.dot(qh, c_prev.astype(BF16), preferred_element_type=F32)
                       + jnp.dot(w_intra.astype(BF16), vaug.astype(BF16), preferred_element_type=F32))
                den = num[:, HEAD_DIM:HEAD_DIM + 1]
                hh = num / jnp.maximum(jnp.abs(den), jnp.exp(-m_t))
                f_tot = f_col[L - 1:L, :]
                w_log = f_tot + z_col
                m_new = jnp.maximum(f_tot + m_prev, jnp.max(w_log, axis=0, keepdims=True))
                decay = jnp.exp(f_tot + m_prev - m_new)
                w = jnp.exp(w_log - m_new)
                upd = lax.dot_general(kh, (w * vaug).astype(BF16), (((0,), (0,)), ((), ())),
                                      preferred_element_type=F32)
                c_sc[h] = decay * c_prev + upd
                m_sc[h:h + 1, :] = jnp.broadcast_to(m_new, (1, LANES))
                ho = jnp.where(lane < 64, hh * oh, 0.0)
                ms = jnp.sum(ho * ho, axis=-1, keepdims=True) * (1.0 / HEAD_DIM)
                yh = ho * lax.rsqrt(ms + NORM_EPS)
                ytile = ytile + (pltpu.roll(yh, 64, 1) if h2 else yh)
            sl = slice(pair * LANES, (pair + 1) * LANES)
            y_ref[r0:r0 + L, sl] = ytile * og[:, sl] * _silu(g_ref[r0:r0 + L, sl])

    @pl.when(ti == nt - 1)
    def _():
        cout_ref[0] = c_sc[...]
        mout_ref[0] = m_sc[...]


def _mlstm(u, gate_bias, onorm_g, c0aug, m0, bsz, t, tc, L, wide):
    nt = t // tc
    const = lambda b, i: (0, 0)
    ucol = lambda off, w: (lambda b, i: (b * nt + i, off // w))
    return pl.pallas_call(
        functools.partial(_mlstm_wide_kernel if wide else _mlstm_kernel, tc=tc, L=L),
        grid=(bsz, nt),
        in_specs=[
            pl.BlockSpec((tc, 256), ucol(U_CQ, 256)),
            pl.BlockSpec((tc, 256), ucol(U_CK, 256)),
            pl.BlockSpec((tc, 256), ucol(U_CV, 256)),
            pl.BlockSpec((tc, LANES), ucol(U_CIF, LANES)),
            pl.BlockSpec((tc, 256), ucol(U_CO, 256)),
            pl.BlockSpec((tc, 256), ucol(U_CG, 256)),
            pl.BlockSpec((1, LANES), const),
            pl.BlockSpec((1, 256), const),
            pl.BlockSpec((1, C_HEADS, LANES, LANES), lambda b, i: (b, 0, 0, 0)),
            pl.BlockSpec((1, SUBLANES, LANES), lambda b, i: (b, 0, 0)),
        ],
        out_specs=[
            pl.BlockSpec((tc, 256), lambda b, i: (b * nt + i, 0)),
            pl.BlockSpec((1, C_HEADS, LANES, LANES), lambda b, i: (b, 0, 0, 0)),
            pl.BlockSpec((1, SUBLANES, LANES), lambda b, i: (b, 0, 0)),
        ],
        out_shape=[
            jax.ShapeDtypeStruct((bsz * t, 256), F32),
            jax.ShapeDtypeStruct((bsz, C_HEADS, LANES, LANES), F32),
            jax.ShapeDtypeStruct((bsz, SUBLANES, LANES), F32),
        ],
        scratch_shapes=[
            pltpu.VMEM((C_HEADS, LANES, LANES), F32),
            pltpu.VMEM((SUBLANES, LANES), F32),
        ],
        compiler_params=_cparams(("parallel", "arbitrary")),
        name="mlstm",
    )(u, u, u, u, u, u, gate_bias, onorm_g, c0aug, m0)


def _mlstm_wide_kernel(q_ref, k_ref, v_ref, if_ref, o_ref, g_ref, bias_ref, og_ref, c0_ref, m0_ref,
                       y_ref, cout_ref, mout_ref, c_sc, m_sc, *, tc, L):
    ti = pl.program_id(1)
    nt = pl.num_programs(1)

    @pl.when(ti == 0)
    def _():
        c_sc[...] = c0_ref[0]
        m_sc[...] = m0_ref[0]

    gates = if_ref[...] + bias_ref[...]
    lf = jnp.minimum(gates, 0.0) - jnp.log1p(jnp.exp(-jnp.abs(gates)))
    in_chunk = _row_iota((tc, LANES)) & (L - 1)
    fcum = lf
    d = 1
    while d < L:
        fcum = fcum + jnp.where(in_chunk >= d, pltpu.roll(fcum, d, 0), 0.0)
        d *= 2
    z = gates - pltpu.roll(fcum, LANES - 4, 1)
    zmax = z
    d = 1
    while d < L:
        zmax = jnp.maximum(zmax, jnp.where(in_chunk >= d, pltpu.roll(zmax, d, 0), -jnp.inf))
        d *= 2
    z_t = z.T
    k_t = (k_ref[...] * (HEAD_DIM ** -0.5)).T

    tri = _lane_iota((L, L)) <= _row_iota((L, L))
    lane = _lane_iota((L, LANES))
    srow = _row_iota((LANES, L))
    og = og_ref[...]
    for c in range(tc // L):
        r0 = c * L
        for pair in range(C_HEADS // 2):
            sl = slice(pair * LANES, (pair + 1) * LANES)
            qt = q_ref[r0:r0 + L, sl]
            vt = v_ref[r0:r0 + L, sl]
            ot = _sigmoid(o_ref[r0:r0 + L, sl])
            ktt = k_t[pair * LANES:(pair + 1) * LANES, r0:r0 + L]
            ytile = jnp.zeros((L, LANES), F32)
            for h2 in range(2):
                h = 2 * pair + h2
                mine = (lane >= 64) if h2 else (lane < 64)
                ones_col = 0 if h2 else HEAD_DIM
                qh = jnp.where(mine, qt, 0.0).astype(BF16)
                kth = jnp.where((srow >= 64) if h2 else (srow < 64), ktt, 0.0).astype(BF16)
                vaug = jnp.where(mine, vt, jnp.where(lane == ones_col, 1.0, 0.0))
                m_prev = m_sc[h:h + 1, 0:1]
                f_col = fcum[r0:r0 + L, 4 + h:5 + h]
                z_col = z[r0:r0 + L, h:h + 1]
                z_row = z_t[h:h + 1, r0:r0 + L]
                a_col = jnp.maximum(zmax[r0:r0 + L, h:h + 1], m_prev)
                decay_intra = jnp.exp(jnp.where(tri, z_row - a_col, -jnp.inf))
                w_intra = jnp.dot(qh, kth, preferred_element_type=F32) * decay_intra
                w_inter = jnp.exp(m_prev - a_col)
                c_prev = c_sc[h]
                num = (w_inter * jnp.dot(qh, c_prev.astype(BF16), preferred_element_type=F32)
                       + jnp.dot(w_intra.astype(BF16), vaug.astype(BF16), preferred_element_type=F32))
                den = num[:, ones_col:ones_col + 1]
                hh = num / jnp.maximum(jnp.abs(den), jnp.exp(-(f_col + a_col)))
                a_last = a_col[L - 1:L, :]
                decay = jnp.exp(m_prev - a_last)
                w = jnp.exp(z_col - a_last)
                c_sc[h] = decay * c_prev + jnp.dot(kth, (w * vaug).astype(BF16), preferred_element_type=F32)
                m_sc[h:h + 1, :] = jnp.broadcast_to(f_col[L - 1:L, :] + a_last, (1, LANES))
                ho = jnp.where(mine, hh * ot, 0.0)
                ms = jnp.sum(ho * ho, axis=-1, keepdims=True) * (1.0 / HEAD_DIM)
                ytile = ytile + ho * lax.rsqrt(ms + NORM_EPS)
            y_ref[r0:r0 + L, sl] = ytile * og[:, sl] * _silu(g_ref[r0:r0 + L, sl])

    @pl.when(ti == nt - 1)
    def _():
        cout_ref[0] = c_sc[...]
        mout_ref[0] = m_sc[...]


def _outproj_kernel(x_ref, ya_ref, yb_ref, gb_ref, yc_ref, p_ref, wa_ref, wb_ref, wc_ref,
                    pg_ref, pp_ref, png_ref, o_ref):
    yb = yb_ref[...] * _silu(gb_ref[...])
    x = (x_ref[...]
         + jnp.dot(ya_ref[...].astype(BF16), wa_ref[...], preferred_element_type=F32)
         + jnp.dot(yb.astype(BF16), wb_ref[...], preferred_element_type=F32)
         + jnp.dot(yc_ref[...].astype(BF16), wc_ref[...], preferred_element_type=F32))
    gate = _sigmoid(jnp.dot(x.astype(BF16), pg_ref[...], preferred_element_type=F32))
    e = jnp.dot(p_ref[...].astype(BF16), pp_ref[...], preferred_element_type=F32)
    e = e * lax.rsqrt(jnp.mean(e * e, axis=-1, keepdims=True) + NORM_EPS) * png_ref[...]
    o_ref[...] = x + gate * e


def _outproj(x2d, ya, yb, u, yc, p_all, layer, wa, wb, wc, pg, pp, png, tm):
    m = x2d.shape[0]
    row = lambda i: (i, 0)
    p_row = lambda i: (layer * (m // tm) + i, 0)
    const = lambda i: (0, 0)
    return pl.pallas_call(
        _outproj_kernel,
        grid=(m // tm,),
        in_specs=[
            pl.BlockSpec((tm, D_MODEL), row),
            pl.BlockSpec((tm, 256), row),
            pl.BlockSpec((tm, 512), row),
            pl.BlockSpec((tm, 512), lambda i: (i, U_GB // 512)),
            pl.BlockSpec((tm, 256), row),
            pl.BlockSpec((tm, PLE_DIM), p_row),
            pl.BlockSpec((256, D_MODEL), const),
            pl.BlockSpec((512, D_MODEL), const),
            pl.BlockSpec((256, D_MODEL), const),
            pl.BlockSpec((D_MODEL, D_MODEL), const),
            pl.BlockSpec((PLE_DIM, D_MODEL), const),
            pl.BlockSpec((1, D_MODEL), const),
        ],
        out_specs=pl.BlockSpec((tm, D_MODEL), row),
        out_shape=jax.ShapeDtypeStruct((m, D_MODEL), F32),
        compiler_params=_cparams(("parallel",)),
        name="outproj",
    )(x2d, ya, yb, u, yc, p_all, wa, wb, wc, pg, pp, png)


def _block_diag(w):
    n, a, b = w.shape
    eye = jnp.eye(n, dtype=w.dtype)
    return jnp.einsum("nab,nm->namb", w, eye).reshape(n * a, n * b)


def _prep_layer(l, w):
    (norm_g, w_in, a_conv_w, a_conv_b, a_w_r, a_b_r, a_w_i, a_b_i, a_lambda, b_qnorm_g, b_knorm_g,
     c_b_i, c_b_f, c_onorm_g, w_out, ple_gate, ple_proj, ple_norm_g) = [a[l] for a in w]
    col = lambda i: w_in[:, _REF_OFFS[i]:_REF_OFFS[i + 1]]
    w_u = jnp.concatenate(
        [col(0), col(1), col(2), col(5), col(3), col(4), col(6), col(7), col(8), col(11), col(12),
         col(9), col(10), jnp.zeros((D_MODEL, LANES - 2 * C_HEADS), F32)], axis=1).astype(BF16)
    gate_bias = jnp.concatenate([c_b_i, c_b_f, jnp.zeros((LANES - 2 * C_HEADS,), F32)])[None, :]
    shift = 1.02 * HEAD_DIM ** 0.5 * jnp.max(jnp.abs(b_qnorm_g)) * jnp.max(jnp.abs(b_knorm_g)) + 0.5
    return dict(
        shift=shift,
        norm_g=norm_g[None, :], w_u=w_u,
        qg=jnp.tile(b_qnorm_g, B_HEADS)[None, :], kg=jnp.tile(b_knorm_g, B_KV_HEADS)[None, :],
        bd=jnp.asarray(np.kron(np.eye(B_HEADS), np.ones((HEAD_DIM, HEAD_DIM))), BF16),
        cw=a_conv_w, cb=a_conv_b[None, :],
        wri=jnp.concatenate([_block_diag(a_w_r), _block_diag(a_w_i)], axis=1).astype(BF16),
        bri=jnp.concatenate([a_b_r, a_b_i])[None, :], lam=a_lambda[None, :],
        gate_bias=gate_bias, onorm_g=c_onorm_g[None, :],
        wa=w_out[:256].astype(BF16), wb=w_out[256:768].astype(BF16), wc=w_out[768:].astype(BF16),
        pg=ple_gate.astype(BF16), pp=ple_proj.astype(BF16), png=ple_norm_g[None, :],
    )


def _pick_tile(n, cap):
    t = min(n, cap)
    while n % t:
        t //= 2
    return t


def _state_tiles(c, n, upper):
    pad_cols = jnp.zeros(c.shape[:3] + (HEAD_DIM - 1,), F32)
    rows = jnp.concatenate([n[..., None], pad_cols, c] if upper else [c, n[..., None], pad_cols], axis=-1)
    blank = jnp.zeros_like(rows)
    return jnp.concatenate([blank, rows] if upper else [rows, blank], axis=-2)


def _layer(x2d, p_all, layer, bsz, t, conv0, h0, c0, n0, m0, lw, attend):
    m = bsz * t
    tm = _pick_tile(m, 256)
    proj = _inproj(x2d, lw["norm_g"], lw["w_u"], lw["qg"], lw["kg"], lw["bd"], tm, bsz, t)
    u, k2d, v2d = proj[:3]
    if len(proj) > 3:
        k_state, v_state = [a.reshape(bsz, B_KV_HEADS, HEAD_DIM, t).transpose(0, 3, 1, 2) for a in proj[3:]]
    else:
        k_state, v_state = [a.reshape(bsz, t, B_KV_HEADS, HEAD_DIM) for a in (k2d, v2d)]
    ya, conv_new, h_last = _rglru(u, conv0, h0[:, None, :], lw["cw"], lw["cb"], lw["wri"], lw["bri"],
                                  lw["lam"], bsz, t, _pick_tile(t, 512))
    yb = attend(u, k2d, v2d)
    wide = t % LANES == 0
    L = LANES if wide else int(np.gcd(t, MLSTM_CHUNK))
    lo, hi = slice(0, HEAD_DIM), slice(HEAD_DIM, 2 * HEAD_DIM)
    if wide:
        even = _state_tiles(c0[:, 0::2], n0[:, 0::2], upper=False)
        odd = _state_tiles(c0[:, 1::2], n0[:, 1::2], upper=True)
        c0aug = jnp.stack([even, odd], axis=2).reshape(bsz, C_HEADS, LANES, LANES)
    else:
        c0aug = _state_tiles(c0, n0, upper=False)
    m0b = jnp.concatenate([jnp.broadcast_to(m0[:, :, None], (bsz, C_HEADS, LANES)),
                           jnp.zeros((bsz, SUBLANES - C_HEADS, LANES), F32)], axis=1)
    yc, caug, mout = _mlstm(u, lw["gate_bias"], lw["onorm_g"], c0aug, m0b, bsz, t, _pick_tile(t, 512), L, wide)
    if wide:
        c_new = jnp.stack([caug[:, 0, lo, lo], caug[:, 1, hi, hi], caug[:, 2, lo, lo], caug[:, 3, hi, hi]], axis=1)
        n_new = jnp.stack([caug[:, 0, lo, HEAD_DIM], caug[:, 1, hi, 0],
                           caug[:, 2, lo, HEAD_DIM], caug[:, 3, hi, 0]], axis=1)
    else:
        c_new, n_new = caug[:, :, lo, lo], caug[:, :, lo, HEAD_DIM]
    xo = _outproj(x2d, ya, yb, u, yc, p_all, layer, lw["wa"], lw["wb"], lw["wc"], lw["pg"], lw["pp"], lw["png"], tm)
    state = (k_state, v_state, conv_new, h_last[:, 0, :], c_new, n_new, mout[:, :C_HEADS, 0])
    return xo, state


def _attend_prompt(u, k2d, v2d, bsz, s, shift):
    nb = s // MOBA_BLOCK
    assert s % MOBA_BLOCK == 0 and nb <= NB_MAX
    kmean = _kmean(k2d, _pick_tile(bsz * nb, 8)).reshape(bsz, nb, B_KV_HEADS, HEAD_DIM)
    kmh = jnp.repeat(kmean, B_HEADS // B_KV_HEADS, axis=2).transpose(0, 2, 1, 3)
    kmh = jnp.pad(kmh, ((0, 0), (0, 0), (0, NB_MAX - nb), (0, 0)))
    wsel = jnp.einsum("bhjd,hg->bhjgd", kmh, jnp.eye(B_HEADS, dtype=F32)).reshape(bsz, B_HEADS * NB_MAX, 512)
    qa, ka, va = _select(u, k2d, v2d, wsel, jnp.full((1, LANES), shift, F32), bsz, s, _pick_tile(s, 512))
    mode = (shift <= SHIFT_LIMIT).astype(jnp.int32).reshape(1)
    return _attn(mode, qa, ka, va, bsz, s)


def _attend_sample(u, k2d, v2d, bsz, t, pt, ck, cv, slopes):
    n_pages = pt.shape[1]
    assert t == 8 and n_pages % PAGES_PER_STEP == 0
    nblk = n_pages * PAGE_SIZE // MOBA_BLOCK
    assert nblk == 64
    q = u[:, U_QB:U_QB + 512].reshape(bsz, t, B_KV_HEADS, 2, HEAD_DIM).transpose(0, 2, 3, 1, 4)
    qbd = jnp.einsum("bghtd,gk->bghtkd", q, jnp.eye(B_KV_HEADS, dtype=F32)).reshape(bsz, 64, 256)
    pad = ((0, 0), (0, LANES - t), (0, 0))
    knew = jnp.pad(k2d.reshape(bsz, t, 256), pad)
    vnew = jnp.pad(v2d.reshape(bsz, t, 256), pad)
    o = _sample_moba(pt, qbd, knew, vnew, slopes, ck, cv, bsz, n_pages, t)
    return o.reshape(bsz, B_HEADS, t, HEAD_DIM).transpose(0, 2, 1, 3).reshape(bsz * t, 512)


def kernel(x_prompt, x_sample, cache_k, cache_v, state_rglru_conv, state_rglru_h, state_mlstm_C, state_mlstm_n, state_mlstm_m, page_table, p_prompt, p_sample, norm_g, w_in, a_conv_w, a_conv_b, a_w_r, a_b_r, a_w_i, a_b_i, a_lambda, b_qnorm_g, b_knorm_g, c_b_i, c_b_f, c_onorm_g, w_out, ple_gate, ple_proj, ple_norm_g):
    weights = (norm_g, w_in, a_conv_w, a_conv_b, a_w_r, a_b_r, a_w_i, a_b_i, a_lambda, b_qnorm_g, b_knorm_g,
               c_b_i, c_b_f, c_onorm_g, w_out, ple_gate, ple_proj, ple_norm_g)
    depth = w_in.shape[0]
    bp, sp = x_prompt.shape[:2]
    bs, ts = x_sample.shape[:2]
    n_phys = cache_k.shape[1]
    ck = cache_k.transpose(0, 1, 3, 4, 2).reshape(depth * n_phys, 256, PAGE_SIZE)
    cv = cache_v.transpose(0, 1, 3, 4, 2).reshape(depth * n_phys, 256, PAGE_SIZE)
    slopes = jnp.asarray(np.repeat(2.0 ** -(np.arange(1, B_HEADS + 1)), ts)[:, None]
                         * np.ones((1, LANES)), F32)

    xp = x_prompt.reshape(bp * sp, D_MODEL)
    pp_all = p_prompt.reshape(depth * bp * sp, PLE_DIM)
    ps_all = p_sample.reshape(depth * bs * ts, PLE_DIM)
    xs = x_sample.reshape(bs * ts, D_MODEL)
    zeros = lambda *s: jnp.zeros(s, F32)
    sp_all, ss_all = [], []
    for l in range(depth):
        lw = _prep_layer(l, weights)
        xp, st_p = _layer(xp, pp_all, l, bp, sp,
                          zeros(bp, CONV_WIDTH - 1, A_WIDTH), zeros(bp, A_WIDTH),
                          zeros(bp, C_HEADS, HEAD_DIM, HEAD_DIM), zeros(bp, C_HEADS, HEAD_DIM), zeros(bp, C_HEADS),
                          lw, functools.partial(_attend_prompt, bsz=bp, s=sp, shift=lw["shift"]))
        attend_s = functools.partial(_attend_sample, bsz=bs, t=ts, pt=page_table + l * n_phys,
                                     ck=ck, cv=cv, slopes=slopes)
        xs, st_s = _layer(xs, ps_all, l, bs, ts,
                          state_rglru_conv[l], state_rglru_h[l], state_mlstm_C[l], state_mlstm_n[l],
                          state_mlstm_m[l], lw, attend_s)
        sp_all.append(st_p)
        ss_all.append(st_s)

    stk = lambda states, j: jnp.stack([s_[j] for s_ in states])
    return (xp.reshape(bp, sp, D_MODEL), xs.reshape(bs, ts, D_MODEL),
            stk(sp_all, 0), stk(sp_all, 1), stk(ss_all, 0), stk(ss_all, 1),
            stk(sp_all, 2), stk(ss_all, 2), stk(sp_all, 3), stk(ss_all, 3),
            stk(sp_all, 4), stk(ss_all, 4), stk(sp_all, 5), stk(ss_all, 5),
            stk(sp_all, 6), stk(ss_all, 6))
```

```python
import functools

import numpy as np
import jax
import jax.numpy as jnp
from jax import lax
from jax.experimental import pallas as pl
from jax.experimental.pallas import tpu as pltpu

F32 = jnp.float32
BF16 = jnp.bfloat16
HIGHEST = lax.Precision.HIGHEST

LANES = 128
SUBLANES = 8

HEAD_DIM = 64
D_MODEL = 1024
A_WIDTH = 256
CONV_WIDTH = 4
RG_C = 8.0
B_HEADS = 8
B_KV_HEADS = 4
C_HEADS = 4
MOBA_BLOCK = 256
MOBA_TOPK = 3
PAGE_SIZE = 128
MLSTM_CHUNK = 64
PLE_DIM = 256
NORM_EPS = 1e-6
NB_MAX = 32
NEG = -1e30

U_XA, U_GA = 0, 256
U_QB, U_GB, U_KB, U_VB = 512, 1024, 1536, 1792
U_CQ, U_CK, U_CV, U_CO, U_CG, U_CIF = 2048, 2304, 2560, 2816, 3072, 3328
D_U = 3456
_REF_SPLITS = (256, 256, 512, 256, 256, 512, 256, 256, 256, 4, 4, 256, 256)
_REF_OFFS = np.concatenate([[0], np.cumsum(_REF_SPLITS)]).tolist()

VMEM_LIMIT = 56 * 1024 * 1024


def _cparams(sem):
    return pltpu.CompilerParams(dimension_semantics=sem, vmem_limit_bytes=VMEM_LIMIT)


def _lane_iota(shape):
    return lax.broadcasted_iota(jnp.int32, shape, len(shape) - 1)


def _row_iota(shape):
    return lax.broadcasted_iota(jnp.int32, shape, len(shape) - 2)


def _sigmoid(x):
    return 1.0 / (1.0 + jnp.exp(-x))


def _silu(x):
    return x * _sigmoid(x)


def _softplus(x):
    return jnp.maximum(x, 0.0) + jnp.log1p(jnp.exp(-jnp.abs(x)))


def _dot_nt(a, b, **kw):
    return lax.dot_general(a, b, (((1,), (1,)), ((), ())), preferred_element_type=F32, **kw)


def _inproj_kernel(x_ref, g_ref, w_ref, qg_ref, kg_ref, bd_ref, u_ref, k_ref, v_ref, *t_refs):
    x = x_ref[...]
    xn = x * lax.rsqrt(jnp.mean(x * x, axis=-1, keepdims=True) + NORM_EPS) * g_ref[...]
    u = jnp.dot(xn.astype(BF16), w_ref[...], preferred_element_type=F32)
    u_ref[...] = u
    bd = bd_ref[...]
    q = u[:, U_QB:U_QB + 512]
    qss = jnp.dot((q * q).astype(BF16), bd, preferred_element_type=F32) * (1.0 / HEAD_DIM)
    qn = q * lax.rsqrt(qss + NORM_EPS) * qg_ref[...] * (HEAD_DIM ** -0.5)
    u_ref[:, U_QB:U_QB + 512] = qn
    k = u[:, U_KB:U_KB + 256]
    kss = jnp.dot((k * k).astype(BF16), bd[:256, :256], preferred_element_type=F32) * (1.0 / HEAD_DIM)
    kn = k * lax.rsqrt(kss + NORM_EPS) * kg_ref[...]
    u_ref[:, U_KB:U_KB + 256] = kn
    k_ref[...] = kn
    v = u[:, U_VB:U_VB + 256]
    v_ref[...] = v
    if t_refs:
        kt_ref, vt_ref = t_refs
        kt_ref[0] = kn.T
        vt_ref[0] = v.T


def _inproj(x2d, norm_g, w_u, qg, kg, bd, tm, bsz, t):
    m = x2d.shape[0]
    const = lambda i: (0, 0)
    transposed = t % tm == 0
    nt = t // tm if transposed else 1
    t_specs = [pl.BlockSpec((1, 256, tm), lambda i: (i // nt, 0, i % nt))] * 2 if transposed else []
    t_shapes = [jax.ShapeDtypeStruct((bsz, 256, t), F32)] * 2 if transposed else []
    return pl.pallas_call(
        _inproj_kernel,
        grid=(m // tm,),
        in_specs=[
            pl.BlockSpec((tm, D_MODEL), lambda i: (i, 0)),
            pl.BlockSpec((1, D_MODEL), const),
            pl.BlockSpec((D_MODEL, D_U), const),
            pl.BlockSpec((1, 512), const),
            pl.BlockSpec((1, 256), const),
            pl.BlockSpec((512, 512), const),
        ],
        out_specs=[
            pl.BlockSpec((tm, D_U), lambda i: (i, 0)),
            pl.BlockSpec((tm, 256), lambda i: (i, 0)),
            pl.BlockSpec((tm, 256), lambda i: (i, 0)),
        ] + t_specs,
        out_shape=[
            jax.ShapeDtypeStruct((m, D_U), F32),
            jax.ShapeDtypeStruct((m, 256), F32),
            jax.ShapeDtypeStruct((m, 256), F32),
        ] + t_shapes,
        compiler_params=_cparams(("parallel",)),
        name="inproj",
    )(x2d, norm_g, w_u, qg, kg, bd)


def _rglru_kernel(xa_ref, ga_ref, conv0_ref, h0_ref, cw_ref, cb_ref, wri_ref, bri_ref, lam_ref,
                  ya_ref, convn_ref, hl_ref, xbuf, abuf, bbuf, hbuf, hcar, *, tt):
    ti = pl.program_id(1)
    nt = pl.num_programs(1)

    @pl.when(ti == 0)
    def _():
        xbuf[5:8, :] = conv0_ref[0]
        hcar[...] = jnp.broadcast_to(h0_ref[0], (SUBLANES, A_WIDTH))

    xa = xa_ref[...]
    xbuf[8:8 + tt, :] = xa
    cw = cw_ref[...]
    xc = jnp.broadcast_to(cb_ref[...], (tt, A_WIDTH))
    for j in range(CONV_WIDTH):
        xc = xc + xbuf[5 + j:5 + j + tt, :] * cw[j:j + 1, :]
    xbuf[5:8, :] = xa[tt - 3:tt, :]

    gates = jnp.dot(xc.astype(BF16), wri_ref[...], preferred_element_type=F32) + bri_ref[...]
    r = _sigmoid(gates[:, :A_WIDTH])
    ig = _sigmoid(gates[:, A_WIDTH:])
    log_a = (-RG_C) * r * _softplus(-lam_ref[...])
    a = jnp.exp(log_a)
    abuf[...] = a
    bbuf[...] = jnp.sqrt(-jnp.tanh(log_a) * (a * a + 1.0)) * (ig * xc)

    rows = _row_iota((SUBLANES, A_WIDTH))

    def group(gi, carry):
        r0 = pl.multiple_of(gi * SUBLANES, SUBLANES)
        a = abuf[pl.ds(r0, SUBLANES), :]
        b = bbuf[pl.ds(r0, SUBLANES), :]
        for d in (1, 2, 4):
            a_sh = pltpu.roll(a, d, 0)
            b_sh = pltpu.roll(b, d, 0)
            keep = rows >= d
            b = jnp.where(keep, a * b_sh + b, b)
            a = jnp.where(keep, a * a_sh, a)
        h = a * hcar[...] + b
        hbuf[pl.ds(r0, SUBLANES), :] = h
        hcar[...] = jnp.broadcast_to(h[SUBLANES - 1:SUBLANES, :], (SUBLANES, A_WIDTH))
        return carry

    lax.fori_loop(0, tt // SUBLANES, group, 0)

    ya_ref[...] = hbuf[...] * _silu(ga_ref[...])

    @pl.when(ti == nt - 1)
    def _():
        convn_ref[0] = xa[tt - 3:tt, :]
        hl_ref[0] = hcar[0:1, :]


def _rglru(u, conv0, h0, cw, cb, wri, bri, lam, bsz, t, tt):
    nt = t // tt
    kern = functools.partial(_rglru_kernel, tt=tt)
    const = lambda b, i: (0, 0)
    return pl.pallas_call(
        kern,
        grid=(bsz, nt),
        in_specs=[
            pl.BlockSpec((tt, A_WIDTH), lambda b, i: (b * nt + i, U_XA // A_WIDTH)),
            pl.BlockSpec((tt, A_WIDTH), lambda b, i: (b * nt + i, U_GA // A_WIDTH)),
            pl.BlockSpec((1, 3, A_WIDTH), lambda b, i: (b, 0, 0)),
            pl.BlockSpec((1, 1, A_WIDTH), lambda b, i: (b, 0, 0)),
            pl.BlockSpec((CONV_WIDTH, A_WIDTH), const),
            pl.BlockSpec((1, A_WIDTH), const),
            pl.BlockSpec((A_WIDTH, 2 * A_WIDTH), const),
            pl.BlockSpec((1, 2 * A_WIDTH), const),
            pl.BlockSpec((1, A_WIDTH), const),
        ],
        out_specs=[
            pl.BlockSpec((tt, A_WIDTH), lambda b, i: (b * nt + i, 0)),
            pl.BlockSpec((1, 3, A_WIDTH), lambda b, i: (b, 0, 0)),
            pl.BlockSpec((1, 1, A_WIDTH), lambda b, i: (b, 0, 0)),
        ],
        out_shape=[
            jax.ShapeDtypeStruct((bsz * t, A_WIDTH), F32),
            jax.ShapeDtypeStruct((bsz, 3, A_WIDTH), F32),
            jax.ShapeDtypeStruct((bsz, 1, A_WIDTH), F32),
        ],
        scratch_shapes=[
            pltpu.VMEM((tt + 8, A_WIDTH), F32),
            pltpu.VMEM((tt, A_WIDTH), F32),
            pltpu.VMEM((tt, A_WIDTH), F32),
            pltpu.VMEM((tt, A_WIDTH), F32),
            pltpu.VMEM((SUBLANES, A_WIDTH), F32),
        ],
        compiler_params=_cparams(("parallel", "arbitrary")),
        name="rglru",
    )(u, u, conv0, h0, cw, cb, wri, bri, lam)


def _kmean_kernel(k_ref, o_ref, *, nblk):
    k = k_ref[...].reshape(nblk, MOBA_BLOCK, 256)
    o_ref[...] = jnp.sum(k, axis=1) * (1.0 / MOBA_BLOCK)


def _kmean(k2d, nblk):
    m = k2d.shape[0]
    rows = nblk * MOBA_BLOCK
    return pl.pallas_call(
        functools.partial(_kmean_kernel, nblk=nblk),
        grid=(m // rows,),
        in_specs=[pl.BlockSpec((rows, 256), lambda i: (i, 0))],
        out_specs=pl.BlockSpec((nblk, 256), lambda i: (i, 0)),
        out_shape=jax.ShapeDtypeStruct((m // MOBA_BLOCK, 256), F32),
        compiler_params=_cparams(("parallel",)),
        name="kmean",
    )(k2d)


def _select_kernel(q_ref, k_ref, v_ref, w_ref, shift_ref, qa_ref, ka_ref, va_ref, *, ts, rs):
    tile0 = pl.program_id(1) * ts
    w = w_ref[0]
    neg_shift = -shift_ref[...]

    def sub_tile(si, carry):
        r0 = pl.multiple_of(si * rs, rs)
        t0 = tile0 + r0
        q = q_ref[pl.ds(r0, rs), :]
        sc = _dot_nt(w, q, precision=HIGHEST)
        shp = (B_HEADS * NB_MAX, rs)
        j = _row_iota(shp) & (NB_MAX - 1)
        n_full = (t0 + _lane_iota(shp)) >> 8
        valid = j < n_full
        scm = jnp.where(valid, sc, -jnp.inf)
        def over_sublanes(op, v):
            for d in (4, 2, 1):
                v = op(v, pltpu.roll(v, d, 0))
            return v

        nv = NB_MAX // SUBLANES
        js = [j[SUBLANES * v:SUBLANES * (v + 1)] for v in range(nv)]
        nf = n_full[0:SUBLANES]
        picked = []
        for h in range(B_HEADS):
            xs = [scm[h * NB_MAX + SUBLANES * v:h * NB_MAX + SUBLANES * (v + 1)] for v in range(nv)]
            sel = [jnp.zeros((SUBLANES, rs), jnp.int32) for _ in range(nv)]
            for r in range(MOBA_TOPK):
                mx = over_sublanes(jnp.maximum, functools.reduce(jnp.maximum, xs))
                cand = [jnp.where(xs[v] == mx, js[v], NB_MAX) for v in range(nv)]
                first = over_sublanes(jnp.minimum, functools.reduce(jnp.minimum, cand))
                for v in range(nv):
                    hit = js[v] == first
                    sel[v] = jnp.where(hit & (nf > r), 1, sel[v])
                    xs[v] = jnp.where(hit, -jnp.inf, xs[v])
            picked.extend(sel)
        allowed = (jnp.concatenate(picked, axis=0) > 0) | (j == n_full)
        bias = jnp.where(allowed, 0.0, NEG).T

        shp1 = (rs, LANES)
        lane = _lane_iota(shp1)
        pos1 = t0 + _row_iota(shp1)
        pos_hi = (pos1 >> 6).astype(F32)
        pos_lo = (pos1 & 63).astype(F32)
        blk_onehot = jnp.where((lane - 64) == (pos1 >> 8), 1.0, 0.0)
        kfeat = jnp.where(lane == 96, pos_hi, jnp.where(lane == 97, pos_lo,
                          jnp.where((lane >= 98) & (lane <= 100), 1.0, 0.0)))
        kextra = jnp.where(lane < 96, blk_onehot, kfeat)
        vextra = jnp.where(lane == 64, 1.0, 0.0)
        k = k_ref[pl.ds(r0, rs), :]
        v = v_ref[pl.ds(r0, rs), :]
        for g in range(B_KV_HEADS):
            kt = k[:, (g // 2) * LANES:(g // 2 + 1) * LANES]
            vt = v[:, (g // 2) * LANES:(g // 2 + 1) * LANES]
            if g % 2:
                kt = pltpu.roll(kt, 64, 1)
                vt = pltpu.roll(vt, 64, 1)
            ka_ref[0, g, pl.ds(r0, rs), :] = jnp.where(lane < 64, kt, kextra).astype(BF16)
            va_ref[0, g, pl.ds(r0, rs), :] = jnp.where(lane < 64, vt, vextra).astype(BF16)
            qt = q[:, g * LANES:(g + 1) * LANES]
            for h2 in range(2):
                h = 2 * g + h2
                base = qt if h2 == 0 else pltpu.roll(qt, 64, 1)
                bt = bias[:, (h // 4) * LANES:(h // 4 + 1) * LANES]
                sh = (64 - (h % 4) * NB_MAX) % LANES
                if sh:
                    bt = pltpu.roll(bt, sh, 1)
                slope = 2.0 ** (-(h + 1))
                qfeat = jnp.where(lane == 96, slope * 64.0, jnp.where(lane == 97, slope,
                                  jnp.where(lane == 98, (-slope * 64.0) * pos_hi,
                                            jnp.where(lane == 99, (-slope) * pos_lo,
                                                      jnp.where(lane == 100, neg_shift, 0.0)))))
                aug = jnp.where(lane < 64, base, jnp.where(lane < 96, bt, qfeat))
                qa_ref[0, h, pl.ds(r0, rs), :] = aug.astype(BF16)
        return carry

    lax.fori_loop(0, ts // rs, sub_tile, 0)


def _select(u, k2d, v2d, wsel, shift, bsz, s, ts):
    nt = s // ts
    return pl.pallas_call(
        functools.partial(_select_kernel, ts=ts, rs=min(ts, LANES)),
        grid=(bsz, nt),
        in_specs=[
            pl.BlockSpec((ts, 512), lambda b, i: (b * nt + i, U_QB // 512)),
            pl.BlockSpec((ts, 256), lambda b, i: (b * nt + i, 0)),
            pl.BlockSpec((ts, 256), lambda b, i: (b * nt + i, 0)),
            pl.BlockSpec((1, B_HEADS * NB_MAX, 512), lambda b, i: (b, 0, 0)),
            pl.BlockSpec((1, LANES), lambda b, i: (0, 0)),
        ],
        out_specs=[
            pl.BlockSpec((1, B_HEADS, ts, LANES), lambda b, i: (b, 0, i, 0)),
            pl.BlockSpec((1, B_KV_HEADS, ts, LANES), lambda b, i: (b, 0, i, 0)),
            pl.BlockSpec((1, B_KV_HEADS, ts, LANES), lambda b, i: (b, 0, i, 0)),
        ],
        out_shape=[
            jax.ShapeDtypeStruct((bsz, B_HEADS, s, LANES), BF16),
            jax.ShapeDtypeStruct((bsz, B_KV_HEADS, s, LANES), BF16),
            jax.ShapeDtypeStruct((bsz, B_KV_HEADS, s, LANES), BF16),
        ],
        compiler_params=_cparams(("parallel", "parallel")),
        name="moba_select",
    )(u, k2d, v2d, wsel, shift)


ATTN_KV_BLOCKS = 4
ATTN_GROUPS = 2
SHIFT_LIMIT = 30.0


def _attn_finish(o_ref, acc, gi, tq):
    res = acc / acc[:, HEAD_DIM:HEAD_DIM + 1]
    lane = _lane_iota((tq, LANES))
    o_ref[:, gi * LANES:(gi + 1) * LANES] = jnp.where(lane < 64, res[:tq], pltpu.roll(res[tq:], 64, 1))


def _attn_online(qa_ref, ka_ref, va_ref, o_ref, m_sc, acc_sc, tq):
    qi = pl.program_id(2)
    rows = 2 * tq
    for gi in range(ATTN_GROUPS):
        q = qa_ref[0, 2 * gi:2 * gi + 2].reshape(rows, LANES)

        k0 = ka_ref[0, gi, pl.ds(pl.multiple_of(qi * tq, tq), tq), :]
        v0 = va_ref[0, gi, pl.ds(pl.multiple_of(qi * tq, tq), tq), :]
        s = _dot_nt(q, k0)
        shp = (rows, tq)
        qpos = _row_iota(shp) & (tq - 1)
        s = jnp.where(_lane_iota(shp) <= qpos, s, NEG)
        m0 = jnp.max(s, axis=-1, keepdims=True)
        p = jnp.exp(s - m0)
        m_sc[gi] = jnp.broadcast_to(m0, (rows, LANES))
        acc_sc[gi] = jnp.dot(p.astype(BF16), v0, preferred_element_type=F32)

        def body(jb, carry, gi=gi, q=q):
            r0 = pl.multiple_of(jb * tq, tq)
            kj = ka_ref[0, gi, pl.ds(r0, tq), :]
            vj = va_ref[0, gi, pl.ds(r0, tq), :]
            sj = _dot_nt(q, kj)
            m_prev = m_sc[gi]
            m_new = jnp.maximum(m_prev, jnp.max(sj, axis=-1, keepdims=True))
            alpha = jnp.exp(m_prev - m_new)
            pj = jnp.exp(sj - m_new[:, 0:1])
            acc_sc[gi] = alpha * acc_sc[gi] + jnp.dot(pj.astype(BF16), vj, preferred_element_type=F32)
            m_sc[gi] = m_new
            return carry

        lax.fori_loop(0, qi, body, 0)
        _attn_finish(o_ref, acc_sc[gi], gi, tq)


def _attn_shifted(qa_ref, ka_ref, va_ref, o_ref, acc_sc, tq):
    qi = pl.program_id(2)
    rows = 2 * tq
    keys = ATTN_KV_BLOCKS * tq
    qs = [qa_ref[0, 2 * gi:2 * gi + 2].reshape(rows, LANES) for gi in range(ATTN_GROUPS)]
    shp = (rows, keys)
    qpos = qi * tq + (_row_iota(shp) & (tq - 1))
    kofs = _lane_iota(shp)
    acc_sc[...] = jnp.zeros((ATTN_GROUPS, rows, LANES), F32)

    def body(jp, carry):
        r0 = pl.multiple_of(jp * keys, keys)
        causal = r0 + kofs <= qpos
        for gi in range(ATTN_GROUPS):
            kj = ka_ref[0, gi, pl.ds(r0, keys), :]
            vj = va_ref[0, gi, pl.ds(r0, keys), :]
            p = jnp.exp(jnp.where(causal, _dot_nt(qs[gi], kj), NEG))
            acc_sc[gi] += jnp.dot(p.astype(BF16), vj, preferred_element_type=F32)
        return carry

    lax.fori_loop(0, (qi + ATTN_KV_BLOCKS) // ATTN_KV_BLOCKS, body, 0)
    for gi in range(ATTN_GROUPS):
        _attn_finish(o_ref, acc_sc[gi], gi, tq)


def _attn_kernel(mode_ref, qa_ref, ka_ref, va_ref, o_ref, m_sc, acc_sc, *, tq):
    @pl.when(mode_ref[0] == 1)
    def _():
        _attn_shifted(qa_ref, ka_ref, va_ref, o_ref, acc_sc, tq)

    @pl.when(mode_ref[0] != 1)
    def _():
        _attn_online(qa_ref, ka_ref, va_ref, o_ref, m_sc, acc_sc, tq)


def _attn(mode, qa, ka, va, bsz, s):
    tq = MOBA_BLOCK
    nq = s // tq
    ng = ATTN_GROUPS
    assert nq % ATTN_KV_BLOCKS == 0 and B_KV_HEADS % ng == 0
    grid_spec = pltpu.PrefetchScalarGridSpec(
        num_scalar_prefetch=1,
        grid=(bsz, B_KV_HEADS // ng, nq),
        in_specs=[
            pl.BlockSpec((1, 2 * ng, tq, LANES), lambda b, g, i, mode: (b, g, i, 0)),
            pl.BlockSpec((1, ng, s, LANES), lambda b, g, i, mode: (b, g, 0, 0)),
            pl.BlockSpec((1, ng, s, LANES), lambda b, g, i, mode: (b, g, 0, 0)),
        ],
        out_specs=pl.BlockSpec((tq, ng * LANES), lambda b, g, i, mode: (b * nq + i, g)),
        scratch_shapes=[
            pltpu.VMEM((ng, 2 * tq, LANES), F32),
            pltpu.VMEM((ng, 2 * tq, LANES), F32),
        ],
    )
    return pl.pallas_call(
        functools.partial(_attn_kernel, tq=tq),
        grid_spec=grid_spec,
        out_shape=jax.ShapeDtypeStruct((bsz * s, 512), F32),
        compiler_params=_cparams(("parallel", "parallel", "arbitrary")),
        name="moba_attn",
    )(mode, qa, ka, va)


PAGES_PER_STEP = 16


def _page_copies(pt_ref, src_hbm, buf, sem, b, c, slot):
    return [pltpu.make_async_copy(src_hbm.at[pt_ref[b, c * PAGES_PER_STEP + p]], buf.at[slot, p], sem.at[slot])
            for p in range(PAGES_PER_STEP)]


def _sample_moba_kernel(pt_ref, q_ref, kn_ref, vn_ref, slope_ref, ck_hbm, cv_hbm, o_ref,
                        kst, vst, kres, kmt, ksem, vsem, *, nseq, nch, nblk, tnew, past):
    b = pl.program_id(0)
    cur = b % 2
    nxt = 1 - cur
    has_next = b + 1 < nseq
    rows = 64
    blocks = PAGES_PER_STEP * PAGE_SIZE // MOBA_BLOCK
    ppb = MOBA_BLOCK // PAGE_SIZE
    km_lane = _lane_iota((256, LANES)) & (nblk - 1)

    def k_copies(seq, c, st):
        return _page_copies(pt_ref, ck_hbm, kst, ksem, seq, c, st)

    def v_copies(c, st):
        return _page_copies(pt_ref, cv_hbm, vst, vsem, b, c, st)

    def stage_k_chunk(res_slot, c, st):
        kres[res_slot, pl.ds(c * PAGES_PER_STEP, PAGES_PER_STEP)] = kst[st].astype(BF16)
        tile = jnp.zeros((256, LANES), F32)
        for jj in range(blocks):
            x = kst[st, ppb * jj]
            for pp in range(1, ppb):
                x = x + kst[st, ppb * jj + pp]
            mean = jnp.sum(x, axis=-1, keepdims=True) * (1.0 / MOBA_BLOCK)
            tile = jnp.where(km_lane == c * blocks + jj, mean, tile)
        if c == 0:
            kmt[res_slot] = tile
        else:
            kmt[res_slot] = kmt[res_slot] + tile

    @pl.when(b == 0)
    def _():
        for cp in k_copies(0, 0, 0):
            cp.start()
        for c in range(nch):
            st = c % 2
            if c + 1 < nch:
                for cp in k_copies(0, c + 1, 1 - st):
                    cp.start()
            for cp in k_copies(0, c, st):
                cp.wait()
            stage_k_chunk(0, c, st)

    @pl.when(has_next)
    def _():
        for cp in k_copies(b + 1, 0, 0):
            cp.start()

    for cp in v_copies(0, 0):
        cp.start()

    qf = q_ref[0]
    qb = qf.astype(BF16)
    slope = slope_ref[...]
    shp = (rows, LANES)
    lane = _lane_iota(shp)
    tok = _row_iota(shp) & (tnew - 1)

    sc = jnp.dot(qf, kmt[cur], preferred_element_type=F32, precision=HIGHEST)
    j = lane & (nblk - 1)
    rank = jnp.zeros(shp, jnp.int32)
    for s in range(1, nblk):
        a = pltpu.roll(sc, 2 * nblk - s, 1)
        wrapped = j >= nblk - s
        rank = rank + jnp.where(wrapped, jnp.where(a >= sc, 1, 0), jnp.where(a > sc, 1, 0))
    bias = jnp.where(rank < MOBA_TOPK, 0.0, NEG)

    s_own = _dot_nt(qb, kn_ref[0].astype(BF16))
    s_own = s_own - slope * (tok - lane).astype(F32)
    s_own = jnp.where((lane <= tok) & (lane < tnew), s_own, NEG)
    m = jnp.max(s_own, axis=-1, keepdims=True)
    p = jnp.exp(s_own - m)
    l = jnp.sum(p, axis=-1, keepdims=True)
    acc = jnp.dot(p.astype(BF16), vn_ref[0].astype(BF16), preferred_element_type=F32)

    keys = PAGES_PER_STEP * PAGE_SIZE
    blocks = keys // MOBA_BLOCK
    kshape = (rows, keys)
    klane = _lane_iota(kshape)
    qpos = past + (_row_iota(kshape) & (tnew - 1))
    slope_k = jnp.broadcast_to(slope[:, 0:1], kshape)
    for c in range(nch):
        st = c % 2
        if c + 1 < nch:
            for cp in v_copies(c + 1, 1 - st):
                cp.start()

            @pl.when(has_next)
            def _():
                for cp in k_copies(b + 1, c + 1, 1 - st):
                    cp.start()

        for cp in v_copies(c, st):
            cp.wait()
        vc = vst[st].astype(BF16)
        s_c = jnp.concatenate(
            [jnp.dot(qb, kres[cur, c * PAGES_PER_STEP + pg], preferred_element_type=F32)
             for pg in range(PAGES_PER_STEP)], axis=1)
        s_c = s_c - slope_k * (qpos - (c * keys + klane)).astype(F32)
        sel = jnp.concatenate(
            [jnp.broadcast_to(bias[:, c * blocks + jj:c * blocks + jj + 1], (rows, MOBA_BLOCK))
             for jj in range(blocks)], axis=1)
        s_c = s_c + sel
        m_new = jnp.maximum(m, jnp.max(s_c, axis=-1, keepdims=True))
        alpha = jnp.exp(m - m_new)
        p = jnp.exp(s_c - m_new)
        l = alpha * l + jnp.sum(p, axis=-1, keepdims=True)
        pb = p.astype(BF16)
        pv = _dot_nt(pb[:, 0:PAGE_SIZE], vc[0])
        for pg in range(1, PAGES_PER_STEP):
            pv = pv + _dot_nt(pb[:, pg * PAGE_SIZE:(pg + 1) * PAGE_SIZE], vc[pg])
        acc = alpha * acc + pv
        m = m_new

        @pl.when(has_next)
        def _():
            for cp in k_copies(b + 1, c, st):
                cp.wait()
            stage_k_chunk(nxt, c, st)

    res = acc / l
    oshape = (rows, 256)
    own = (_lane_iota(oshape) >> 6) == (_row_iota(oshape) >> 4)
    res = jnp.where(own, res, 0.0)
    folded = res + pltpu.roll(res, 64, 1) + pltpu.roll(res, 128, 1) + pltpu.roll(res, 192, 1)
    o_ref[0] = folded[:, :HEAD_DIM]


def _sample_moba(pt, qbd, knew, vnew, slopes, ck, cv, bsz, n_pages, tnew):
    nch = n_pages // PAGES_PER_STEP
    nblk = n_pages * PAGE_SIZE // MOBA_BLOCK
    past = n_pages * PAGE_SIZE
    grid_spec = pltpu.PrefetchScalarGridSpec(
        num_scalar_prefetch=1,
        grid=(bsz,),
        in_specs=[
            pl.BlockSpec((1, 64, 256), lambda b, pt: (b, 0, 0)),
            pl.BlockSpec((1, LANES, 256), lambda b, pt: (b, 0, 0)),
            pl.BlockSpec((1, LANES, 256), lambda b, pt: (b, 0, 0)),
            pl.BlockSpec((64, LANES), lambda b, pt: (0, 0)),
            pl.BlockSpec(memory_space=pl.ANY),
            pl.BlockSpec(memory_space=pl.ANY),
        ],
        out_specs=pl.BlockSpec((1, 64, HEAD_DIM), lambda b, pt: (b, 0, 0)),
        scratch_shapes=[
            pltpu.VMEM((2, PAGES_PER_STEP, 256, PAGE_SIZE), F32),
            pltpu.VMEM((2, PAGES_PER_STEP, 256, PAGE_SIZE), F32),
            pltpu.VMEM((2, n_pages, 256, PAGE_SIZE), BF16),
            pltpu.VMEM((2, 256, LANES), F32),
            pltpu.SemaphoreType.DMA((2,)),
            pltpu.SemaphoreType.DMA((2,)),
        ],
    )
    return pl.pallas_call(
        functools.partial(_sample_moba_kernel, nseq=bsz, nch=nch, nblk=nblk, tnew=tnew, past=past),
        grid_spec=grid_spec,
        out_shape=jax.ShapeDtypeStruct((bsz, 64, HEAD_DIM), F32),
        compiler_params=_cparams(("arbitrary",)),
        name="sample_moba",
    )(pt, qbd, knew, vnew, slopes, ck, cv)


def _mlstm_kernel(q_ref, k_ref, v_ref, if_ref, o_ref, g_ref, bias_ref, og_ref, c0_ref, m0_ref,
                  y_ref, cout_ref, mout_ref, c_sc, m_sc, *, tc, L):
    ti = pl.program_id(1)
    nt = pl.num_programs(1)

    @pl.when(ti == 0)
    def _():
        c_sc[...] = c0_ref[0]
        m_sc[...] = m0_ref[0]

    gates = if_ref[...] + bias_ref[...]
    lf = jnp.minimum(gates, 0.0) - jnp.log1p(jnp.exp(-jnp.abs(gates)))
    tri = _lane_iota((L, L)) <= _row_iota((L, L))
    tri_f = jnp.where(tri, 1.0, 0.0)
    lane = _lane_iota((L, LANES))
    og = og_ref[...]
    for c in range(tc // L):
        r0 = c * L
        fcum = jnp.dot(tri_f, lf[r0:r0 + L], preferred_element_type=F32, precision=HIGHEST)
        z = gates[r0:r0 + L] - pltpu.roll(fcum, LANES - 4, 1)
        qt = q_ref[r0:r0 + L, :]
        kt = k_ref[r0:r0 + L, :] * (HEAD_DIM ** -0.5)
        vt = v_ref[r0:r0 + L, :]
        ot = _sigmoid(o_ref[r0:r0 + L, :])
        for pair in range(C_HEADS // 2):
            ytile = jnp.zeros((L, LANES), F32)
            for h2 in range(2):
                h = 2 * pair + h2
                sl = slice(pair * LANES, (pair + 1) * LANES)
                qh, kh, vh, oh = qt[:, sl], kt[:, sl], vt[:, sl], ot[:, sl]
                if h2:
                    qh, kh, vh, oh = [pltpu.roll(a, 64, 1) for a in (qh, kh, vh, oh)]
                qh = jnp.where(lane < 64, qh, 0.0).astype(BF16)
                kh = jnp.where(lane < 64, kh, 0.0).astype(BF16)
                vaug = jnp.where(lane < 64, vh, jnp.where(lane == 64, 1.0, 0.0))
                m_prev = m_sc[h:h + 1, 0:1]
                f_col = fcum[:, 4 + h:5 + h]
                z_col = z[:, h:h + 1]
                pick = jnp.where(lane == h, 1.0, 0.0)
                z_row = _dot_nt(pick, z, precision=HIGHEST)
                d_log = jnp.where(tri, f_col + z_row, -jnp.inf)
                inter = f_col + m_prev
                m_t = jnp.maximum(jnp.max(d_log, axis=-1, keepdims=True), inter)
                w_intra = _dot_nt(qh, kh) * jnp.exp(d_log - m_t)
                w_inter = jnp.exp(inter - m_t)
                c_prev = c_sc[h]
                num = (w_inter * jnp.dot(qh, c_prev.astype(BF16), preferred_element_type=F32)
                       + jnp.dot(w_intra.astype(BF16), vaug.astype(BF16), preferred_element_type=F32))
                den = num[:, HEAD_DIM:HEAD_DIM + 1]
                hh = num / jnp.maximum(jnp.abs(den), jnp.exp(-m_t))
                f_tot = f_col[L - 1:L, :]
                w_log = f_tot + z_col
                m_new = jnp.maximum(f_tot + m_prev, jnp.max(w_log, axis=0, keepdims=True))
                decay = jnp.exp(f_tot + m_prev - m_new)
                w = jnp.exp(w_log - m_new)
                upd = lax.dot_general(kh, (w * vaug).astype(BF16), (((0,), (0,)), ((), ())),
                                      preferred_element_type=F32)
                c_sc[h] = decay * c_prev + upd
                m_sc[h:h + 1, :] = jnp.broadcast_to(m_new, (1, LANES))
                ho = jnp.where(lane < 64, hh * oh, 0.0)
                ms = jnp.sum(ho * ho, axis=-1, keepdims=True) * (1.0 / HEAD_DIM)
                yh = ho * lax.rsqrt(ms + NORM_EPS)
                ytile = ytile + (pltpu.roll(yh, 64, 1) if h2 else yh)
            sl = slice(pair * LANES, (pair + 1) * LANES)
            y_ref[r0:r0 + L, sl] = ytile * og[:, sl] * _silu(g_ref[r0:r0 + L, sl])

    @pl.when(ti == nt - 1)
    def _():
        cout_ref[0] = c_sc[...]
        mout_ref[0] = m_sc[...]


def _mlstm(u, gate_bias, onorm_g, c0aug, m0, bsz, t, tc, L, wide):
    nt = t // tc
    const = lambda b, i: (0, 0)
    ucol = lambda off, w: (lambda b, i: (b * nt + i, off // w))
    return pl.pallas_call(
        functools.partial(_mlstm_wide_kernel if wide else _mlstm_kernel, tc=tc, L=L),
        grid=(bsz, nt),
        in_specs=[
            pl.BlockSpec((tc, 256), ucol(U_CQ, 256)),
            pl.BlockSpec((tc, 256), ucol(U_CK, 256)),
            pl.BlockSpec((tc, 256), ucol(U_CV, 256)),
            pl.BlockSpec((tc, LANES), ucol(U_CIF, LANES)),
            pl.BlockSpec((tc, 256), ucol(U_CO, 256)),
            pl.BlockSpec((tc, 256), ucol(U_CG, 256)),
            pl.BlockSpec((1, LANES), const),
            pl.BlockSpec((1, 256), const),
            pl.BlockSpec((1, C_HEADS, LANES, LANES), lambda b, i: (b, 0, 0, 0)),
            pl.BlockSpec((1, SUBLANES, LANES), lambda b, i: (b, 0, 0)),
        ],
        out_specs=[
            pl.BlockSpec((tc, 256), lambda b, i: (b * nt + i, 0)),
            pl.BlockSpec((1, C_HEADS, LANES, LANES), lambda b, i: (b, 0, 0, 0)),
            pl.BlockSpec((1, SUBLANES, LANES), lambda b, i: (b, 0, 0)),
        ],
        out_shape=[
            jax.ShapeDtypeStruct((bsz * t, 256), F32),
            jax.ShapeDtypeStruct((bsz, C_HEADS, LANES, LANES), F32),
            jax.ShapeDtypeStruct((bsz, SUBLANES, LANES), F32),
        ],
        scratch_shapes=[
            pltpu.VMEM((C_HEADS, LANES, LANES), F32),
            pltpu.VMEM((SUBLANES, LANES), F32),
        ],
        compiler_params=_cparams(("parallel", "arbitrary")),
        name="mlstm",
    )(u, u, u, u, u, u, gate_bias, onorm_g, c0aug, m0)


def _mlstm_wide_kernel(q_ref, k_ref, v_ref, if_ref, o_ref, g_ref, bias_ref, og_ref, c0_ref, m0_ref,
                       y_ref, cout_ref, mout_ref, c_sc, m_sc, *, tc, L):
    ti = pl.program_id(1)
    nt = pl.num_programs(1)

    @pl.when(ti == 0)
    def _():
        c_sc[...] = c0_ref[0]
        m_sc[...] = m0_ref[0]

    gates = if_ref[...] + bias_ref[...]
    lf = jnp.minimum(gates, 0.0) - jnp.log1p(jnp.exp(-jnp.abs(gates)))
    in_chunk = _row_iota((tc, LANES)) & (L - 1)
    fcum = lf
    d = 1
    while d < L:
        fcum = fcum + jnp.where(in_chunk >= d, pltpu.roll(fcum, d, 0), 0.0)
        d *= 2
    z = gates - pltpu.roll(fcum, LANES - 4, 1)
    zmax = z
    d = 1
    while d < L:
        zmax = jnp.maximum(zmax, jnp.where(in_chunk >= d, pltpu.roll(zmax, d, 0), -jnp.inf))
        d *= 2
    z_t = z.T
    k_t = (k_ref[...] * (HEAD_DIM ** -0.5)).T

    tri = _lane_iota((L, L)) <= _row_iota((L, L))
    lane = _lane_iota((L, LANES))
    srow = _row_iota((LANES, L))
    og = og_ref[...]
    for c in range(tc // L):
        r0 = c * L
        for pair in range(C_HEADS // 2):
            sl = slice(pair * LANES, (pair + 1) * LANES)
            qt = q_ref[r0:r0 + L, sl]
            vt = v_ref[r0:r0 + L, sl]
            ot = _sigmoid(o_ref[r0:r0 + L, sl])
            ktt = k_t[pair * LANES:(pair + 1) * LANES, r0:r0 + L]
            ytile = jnp.zeros((L, LANES), F32)
            for h2 in range(2):
                h = 2 * pair + h2
                mine = (lane >= 64) if h2 else (lane < 64)
                ones_col = 0 if h2 else HEAD_DIM
                qh = jnp.where(mine, qt, 0.0).astype(BF16)
                kth = jnp.where((srow >= 64) if h2 else (srow < 64), ktt, 0.0).astype(BF16)
                vaug = jnp.where(mine, vt, jnp.where(lane == ones_col, 1.0, 0.0))
                m_prev = m_sc[h:h + 1, 0:1]
                f_col = fcum[r0:r0 + L, 4 + h:5 + h]
                z_col = z[r0:r0 + L, h:h + 1]
                z_row = z_t[h:h + 1, r0:r0 + L]
                a_col = jnp.maximum(zmax[r0:r0 + L, h:h + 1], m_prev)
                decay_intra = jnp.exp(jnp.where(tri, z_row - a_col, -jnp.inf))
                w_intra = jnp.dot(qh, kth, preferred_element_type=F32) * decay_intra
                w_inter = jnp.exp(m_prev - a_col)
                c_prev = c_sc[h]
                num = (w_inter * jnp.dot(qh, c_prev.astype(BF16), preferred_element_type=F32)
                       + jnp.dot(w_intra.astype(BF16), vaug.astype(BF16), preferred_element_type=F32))
                den = num[:, ones_col:ones_col + 1]
                hh = num / jnp.maximum(jnp.abs(den), jnp.exp(-(f_col + a_col)))
                a_last = a_col[L - 1:L, :]
                decay = jnp.exp(m_prev - a_last)
                w = jnp.exp(z_col - a_last)
                c_sc[h] = decay * c_prev + jnp.dot(kth, (w * vaug).astype(BF16), preferred_element_type=F32)
                m_sc[h:h + 1, :] = jnp.broadcast_to(f_col[L - 1:L, :] + a_last, (1, LANES))
                ho = jnp.where(mine, hh * ot, 0.0)
                ms = jnp.sum(ho * ho, axis=-1, keepdims=True) * (1.0 / HEAD_DIM)
                ytile = ytile + ho * lax.rsqrt(ms + NORM_EPS)
            y_ref[r0:r0 + L, sl] = ytile * og[:, sl] * _silu(g_ref[r0:r0 + L, sl])

    @pl.when(ti == nt - 1)
    def _():
        cout_ref[0] = c_sc[...]
        mout_ref[0] = m_sc[...]


def _outproj_kernel(x_ref, ya_ref, yb_ref, gb_ref, yc_ref, p_ref, wa_ref, wb_ref, wc_ref,
                    pg_ref, pp_ref, png_ref, o_ref):
    yb = yb_ref[...] * _silu(gb_ref[...])
    x = (x_ref[...]
         + jnp.dot(ya_ref[...].astype(BF16), wa_ref[...], preferred_element_type=F32)
         + jnp.dot(yb.astype(BF16), wb_ref[...], preferred_element_type=F32)
         + jnp.dot(yc_ref[...].astype(BF16), wc_ref[...], preferred_element_type=F32))
    gate = _sigmoid(jnp.dot(x.astype(BF16), pg_ref[...], preferred_element_type=F32))
    e = jnp.dot(p_ref[...].astype(BF16), pp_ref[...], preferred_element_type=F32)
    e = e * lax.rsqrt(jnp.mean(e * e, axis=-1, keepdims=True) + NORM_EPS) * png_ref[...]
    o_ref[...] = x + gate * e


def _outproj(x2d, ya, yb, u, yc, p_all, layer, wa, wb, wc, pg, pp, png, tm):
    m = x2d.shape[0]
    row = lambda i: (i, 0)
    p_row = lambda i: (layer * (m // tm) + i, 0)
    const = lambda i: (0, 0)
    return pl.pallas_call(
        _outproj_kernel,
        grid=(m // tm,),
        in_specs=[
            pl.BlockSpec((tm, D_MODEL), row),
            pl.BlockSpec((tm, 256), row),
            pl.BlockSpec((tm, 512), row),
            pl.BlockSpec((tm, 512), lambda i: (i, U_GB // 512)),
            pl.BlockSpec((tm, 256), row),
            pl.BlockSpec((tm, PLE_DIM), p_row),
            pl.BlockSpec((256, D_MODEL), const),
            pl.BlockSpec((512, D_MODEL), const),
            pl.BlockSpec((256, D_MODEL), const),
            pl.BlockSpec((D_MODEL, D_MODEL), const),
            pl.BlockSpec((PLE_DIM, D_MODEL), const),
            pl.BlockSpec((1, D_MODEL), const),
        ],
        out_specs=pl.BlockSpec((tm, D_MODEL), row),
        out_shape=jax.ShapeDtypeStruct((m, D_MODEL), F32),
        compiler_params=_cparams(("parallel",)),
        name="outproj",
    )(x2d, ya, yb, u, yc, p_all, wa, wb, wc, pg, pp, png)


def _block_diag(w):
    n, a, b = w.shape
    eye = jnp.eye(n, dtype=w.dtype)
    return jnp.einsum("nab,nm->namb", w, eye).reshape(n * a, n * b)


def _prep_layer(l, w):
    (norm_g, w_in, a_conv_w, a_conv_b, a_w_r, a_b_r, a_w_i, a_b_i, a_lambda, b_qnorm_g, b_knorm_g,
     c_b_i, c_b_f, c_onorm_g, w_out, ple_gate, ple_proj, ple_norm_g) = [a[l] for a in w]
    col = lambda i: w_in[:, _REF_OFFS[i]:_REF_OFFS[i + 1]]
    w_u = jnp.concatenate(
        [col(0), col(1), col(2), col(5), col(3), col(4), col(6), col(7), col(8), col(11), col(12),
         col(9), col(10), jnp.zeros((D_MODEL, LANES - 2 * C_HEADS), F32)], axis=1).astype(BF16)
    gate_bias = jnp.concatenate([c_b_i, c_b_f, jnp.zeros((LANES - 2 * C_HEADS,), F32)])[None, :]
    shift = 1.02 * HEAD_DIM ** 0.5 * jnp.max(jnp.abs(b_qnorm_g)) * jnp.max(jnp.abs(b_knorm_g)) + 0.5
    return dict(
        shift=shift,
        norm_g=norm_g[None, :], w_u=w_u,
        qg=jnp.tile(b_qnorm_g, B_HEADS)[None, :], kg=jnp.tile(b_knorm_g, B_KV_HEADS)[None, :],
        bd=jnp.asarray(np.kron(np.eye(B_HEADS), np.ones((HEAD_DIM, HEAD_DIM))), BF16),
        cw=a_conv_w, cb=a_conv_b[None, :],
        wri=jnp.concatenate([_block_diag(a_w_r), _block_diag(a_w_i)], axis=1).astype(BF16),
        bri=jnp.concatenate([a_b_r, a_b_i])[None, :], lam=a_lambda[None, :],
        gate_bias=gate_bias, onorm_g=c_onorm_g[None, :],
        wa=w_out[:256].astype(BF16), wb=w_out[256:768].astype(BF16), wc=w_out[768:].astype(BF16),
        pg=ple_gate.astype(BF16), pp=ple_proj.astype(BF16), png=ple_norm_g[None, :],
    )


def _pick_tile(n, cap):
    t = min(n, cap)
    while n % t:
        t //= 2
    return t


def _state_tiles(c, n, upper):
    pad_cols = jnp.zeros(c.shape[:3] + (HEAD_DIM - 1,), F32)
    rows = jnp.concatenate([n[..., None], pad_cols, c] if upper else [c, n[..., None], pad_cols], axis=-1)
    blank = jnp.zeros_like(rows)
    return jnp.concatenate([blank, rows] if upper else [rows, blank], axis=-2)


def _layer(x2d, p_all, layer, bsz, t, conv0, h0, c0, n0, m0, lw, attend):
    m = bsz * t
    tm = _pick_tile(m, 512)
    proj = _inproj(x2d, lw["norm_g"], lw["w_u"], lw["qg"], lw["kg"], lw["bd"], tm, bsz, t)
    u, k2d, v2d = proj[:3]
    if len(proj) > 3:
        k_state, v_state = [a.reshape(bsz, B_KV_HEADS, HEAD_DIM, t).transpose(0, 3, 1, 2) for a in proj[3:]]
    else:
        k_state, v_state = [a.reshape(bsz, t, B_KV_HEADS, HEAD_DIM) for a in (k2d, v2d)]
    ya, conv_new, h_last = _rglru(u, conv0, h0[:, None, :], lw["cw"], lw["cb"], lw["wri"], lw["bri"],
                                  lw["lam"], bsz, t, _pick_tile(t, 512))
    yb = attend(u, k2d, v2d)
    wide = t % LANES == 0
    L = LANES if wide else int(np.gcd(t, MLSTM_CHUNK))
    lo, hi = slice(0, HEAD_DIM), slice(HEAD_DIM, 2 * HEAD_DIM)
    if wide:
        even = _state_tiles(c0[:, 0::2], n0[:, 0::2], upper=False)
        odd = _state_tiles(c0[:, 1::2], n0[:, 1::2], upper=True)
        c0aug = jnp.stack([even, odd], axis=2).reshape(bsz, C_HEADS, LANES, LANES)
    else:
        c0aug = _state_tiles(c0, n0, upper=False)
    m0b = jnp.concatenate([jnp.broadcast_to(m0[:, :, None], (bsz, C_HEADS, LANES)),
                           jnp.zeros((bsz, SUBLANES - C_HEADS, LANES), F32)], axis=1)
    yc, caug, mout = _mlstm(u, lw["gate_bias"], lw["onorm_g"], c0aug, m0b, bsz, t, _pick_tile(t, 512), L, wide)
    if wide:
        c_new = jnp.stack([caug[:, 0, lo, lo], caug[:, 1, hi, hi], caug[:, 2, lo, lo], caug[:, 3, hi, hi]], axis=1)
        n_new = jnp.stack([caug[:, 0, lo, HEAD_DIM], caug[:, 1, hi, 0],
                           caug[:, 2, lo, HEAD_DIM], caug[:, 3, hi, 0]], axis=1)
    else:
        c_new, n_new = caug[:, :, lo, lo], caug[:, :, lo, HEAD_DIM]
    xo = _outproj(x2d, ya, yb, u, yc, p_all, layer, lw["wa"], lw["wb"], lw["wc"], lw["pg"], lw["pp"], lw["png"], tm)
    state = (k_state, v_state, conv_new, h_last[:, 0, :], c_new, n_new, mout[:, :C_HEADS, 0])
    return xo, state


def _attend_prompt(u, k2d, v2d, bsz, s, shift):
    nb = s // MOBA_BLOCK
    assert s % MOBA_BLOCK == 0 and nb <= NB_MAX
    kmean = _kmean(k2d, _pick_tile(bsz * nb, 8)).reshape(bsz, nb, B_KV_HEADS, HEAD_DIM)
    kmh = jnp.repeat(kmean, B_HEADS // B_KV_HEADS, axis=2).transpose(0, 2, 1, 3)
    kmh = jnp.pad(kmh, ((0, 0), (0, 0), (0, NB_MAX - nb), (0, 0)))
    wsel = jnp.einsum("bhjd,hg->bhjgd", kmh, jnp.eye(B_HEADS, dtype=F32)).reshape(bsz, B_HEADS * NB_MAX, 512)
    qa, ka, va = _select(u, k2d, v2d, wsel, jnp.full((1, LANES), shift, F32), bsz, s, _pick_tile(s, 512))
    mode = (shift <= SHIFT_LIMIT).astype(jnp.int32).reshape(1)
    return _attn(mode, qa, ka, va, bsz, s)


def _attend_sample(u, k2d, v2d, bsz, t, pt, ck, cv, slopes):
    n_pages = pt.shape[1]
    assert t == 8 and n_pages % PAGES_PER_STEP == 0
    nblk = n_pages * PAGE_SIZE // MOBA_BLOCK
    assert nblk == 64
    q = u[:, U_QB:U_QB + 512].reshape(bsz, t, B_KV_HEADS, 2, HEAD_DIM).transpose(0, 2, 3, 1, 4)
    qbd = jnp.einsum("bghtd,gk->bghtkd", q, jnp.eye(B_KV_HEADS, dtype=F32)).reshape(bsz, 64, 256)
    pad = ((0, 0), (0, LANES - t), (0, 0))
    knew = jnp.pad(k2d.reshape(bsz, t, 256), pad)
    vnew = jnp.pad(v2d.reshape(bsz, t, 256), pad)
    o = _sample_moba(pt, qbd, knew, vnew, slopes, ck, cv, bsz, n_pages, t)
    return o.reshape(bsz, B_HEADS, t, HEAD_DIM).transpose(0, 2, 1, 3).reshape(bsz * t, 512)


def kernel(x_prompt, x_sample, cache_k, cache_v, state_rglru_conv, state_rglru_h, state_mlstm_C, state_mlstm_n, state_mlstm_m, page_table, p_prompt, p_sample, norm_g, w_in, a_conv_w, a_conv_b, a_w_r, a_b_r, a_w_i, a_b_i, a_lambda, b_qnorm_g, b_knorm_g, c_b_i, c_b_f, c_onorm_g, w_out, ple_gate, ple_proj, ple_norm_g):
    weights = (norm_g, w_in, a_conv_w, a_conv_b, a_w_r, a_b_r, a_w_i, a_b_i, a_lambda, b_qnorm_g, b_knorm_g,
               c_b_i, c_b_f, c_onorm_g, w_out, ple_gate, ple_proj, ple_norm_g)
    depth = w_in.shape[0]
    bp, sp = x_prompt.shape[:2]
    bs, ts = x_sample.shape[:2]
    n_phys = cache_k.shape[1]
    ck = cache_k.transpose(0, 1, 3, 4, 2).reshape(depth * n_phys, 256, PAGE_SIZE)
    cv = cache_v.transpose(0, 1, 3, 4, 2).reshape(depth * n_phys, 256, PAGE_SIZE)
    slopes = jnp.asarray(np.repeat(2.0 ** -(np.arange(1, B_HEADS + 1)), ts)[:, None]
                         * np.ones((1, LANES)), F32)

    xp = x_prompt.reshape(bp * sp, D_MODEL)
    pp_all = p_prompt.reshape(depth * bp * sp, PLE_DIM)
    ps_all = p_sample.reshape(depth * bs * ts, PLE_DIM)
    xs = x_sample.reshape(bs * ts, D_MODEL)
    zeros = lambda *s: jnp.zeros(s, F32)
    sp_all, ss_all = [], []
    for l in range(depth):
        lw = _prep_layer(l, weights)
        xp, st_p = _layer(xp, pp_all, l, bp, sp,
                          zeros(bp, CONV_WIDTH - 1, A_WIDTH), zeros(bp, A_WIDTH),
                          zeros(bp, C_HEADS, HEAD_DIM, HEAD_DIM), zeros(bp, C_HEADS, HEAD_DIM), zeros(bp, C_HEADS),
                          lw, functools.partial(_attend_prompt, bsz=bp, s=sp, shift=lw["shift"]))
        attend_s = functools.partial(_attend_sample, bsz=bs, t=ts, pt=page_table + l * n_phys,
                                     ck=ck, cv=cv, slopes=slopes)
        xs, st_s = _layer(xs, ps_all, l, bs, ts,
                          state_rglru_conv[l], state_rglru_h[l], state_mlstm_C[l], state_mlstm_n[l],
                          state_mlstm_m[l], lw, attend_s)
        sp_all.append(st_p)
        ss_all.append(st_s)

    stk = lambda states, j: jnp.stack([s_[j] for s_ in states])
    return (xp.reshape(bp, sp, D_MODEL), xs.reshape(bs, ts, D_MODEL),
            stk(sp_all, 0), stk(sp_all, 1), stk(ss_all, 0), stk(ss_all, 1),
            stk(sp_all, 2), stk(ss_all, 2), stk(sp_all, 3), stk(ss_all, 3),
            stk(sp_all, 4), stk(ss_all, 4), stk(sp_all, 5), stk(ss_all, 5),
            stk(sp_all, 6), stk(ss_all, 6))
```

```python
import functools

import numpy as np
import jax
import jax.numpy as jnp
from jax import lax
from jax.experimental import pallas as pl
from jax.experimental.pallas import tpu as pltpu

F32 = jnp.float32
BF16 = jnp.bfloat16
HIGHEST = lax.Precision.HIGHEST

LANES = 128
SUBLANES = 8

HEAD_DIM = 64
D_MODEL = 1024
A_WIDTH = 256
CONV_WIDTH = 4
RG_C = 8.0
B_HEADS = 8
B_KV_HEADS = 4
C_HEADS = 4
MOBA_BLOCK = 256
MOBA_TOPK = 3
PAGE_SIZE = 128
MLSTM_CHUNK = 64
PLE_DIM = 256
NORM_EPS = 1e-6
NB_MAX = 32
NEG = -1e30

U_XA, U_GA = 0, 256
U_QB, U_GB, U_KB, U_VB = 512, 1024, 1536, 1792
U_CQ, U_CK, U_CV, U_CO, U_CG, U_CIF = 2048, 2304, 2560, 2816, 3072, 3328
D_U = 3456
_REF_SPLITS = (256, 256, 512, 256, 256, 512, 256, 256, 256, 4, 4, 256, 256)
_REF_OFFS = np.concatenate([[0], np.cumsum(_REF_SPLITS)]).tolist()

VMEM_LIMIT = 56 * 1024 * 1024


def _cparams(sem):
    return pltpu.CompilerParams(dimension_semantics=sem, vmem_limit_bytes=VMEM_LIMIT)


def _lane_iota(shape):
    return lax.broadcasted_iota(jnp.int32, shape, len(shape) - 1)


def _row_iota(shape):
    return lax.broadcasted_iota(jnp.int32, shape, len(shape) - 2)


def _sigmoid(x):
    return 1.0 / (1.0 + jnp.exp(-x))


def _silu(x):
    return x * _sigmoid(x)


def _softplus(x):
    return jnp.maximum(x, 0.0) + jnp.log1p(jnp.exp(-jnp.abs(x)))


def _dot_nt(a, b, **kw):
    return lax.dot_general(a, b, (((1,), (1,)), ((), ())), preferred_element_type=F32, **kw)


def _inproj_kernel(x_ref, g_ref, w_ref, qg_ref, kg_ref, bd_ref, u_ref, k_ref, v_ref, *t_refs):
    x = x_ref[...]
    xn = x * lax.rsqrt(jnp.mean(x * x, axis=-1, keepdims=True) + NORM_EPS) * g_ref[...]
    u = jnp.dot(xn.astype(BF16), w_ref[...], preferred_element_type=F32)
    u_ref[...] = u
    bd = bd_ref[...]
    q = u[:, U_QB:U_QB + 512]
    qss = jnp.dot((q * q).astype(BF16), bd, preferred_element_type=F32) * (1.0 / HEAD_DIM)
    qn = q * lax.rsqrt(qss + NORM_EPS) * qg_ref[...] * (HEAD_DIM ** -0.5)
    u_ref[:, U_QB:U_QB + 512] = qn
    k = u[:, U_KB:U_KB + 256]
    kss = jnp.dot((k * k).astype(BF16), bd[:256, :256], preferred_element_type=F32) * (1.0 / HEAD_DIM)
    kn = k * lax.rsqrt(kss + NORM_EPS) * kg_ref[...]
    u_ref[:, U_KB:U_KB + 256] = kn
    k_ref[...] = kn
    v = u[:, U_VB:U_VB + 256]
    v_ref[...] = v
    if t_refs:
        kt_ref, vt_ref = t_refs
        kt_ref[0] = kn.T
        vt_ref[0] = v.T


def _inproj(x2d, norm_g, w_u, qg, kg, bd, tm, bsz, t):
    m = x2d.shape[0]
    const = lambda i: (0, 0)
    transposed = t % tm == 0
    nt = t // tm if transposed else 1
    t_specs = [pl.BlockSpec((1, 256, tm), lambda i: (i // nt, 0, i % nt))] * 2 if transposed else []
    t_shapes = [jax.ShapeDtypeStruct((bsz, 256, t), F32)] * 2 if transposed else []
    return pl.pallas_call(
        _inproj_kernel,
        grid=(m // tm,),
        in_specs=[
            pl.BlockSpec((tm, D_MODEL), lambda i: (i, 0)),
            pl.BlockSpec((1, D_MODEL), const),
            pl.BlockSpec((D_MODEL, D_U), const),
            pl.BlockSpec((1, 512), const),
            pl.BlockSpec((1, 256), const),
            pl.BlockSpec((512, 512), const),
        ],
        out_specs=[
            pl.BlockSpec((tm, D_U), lambda i: (i, 0)),
            pl.BlockSpec((tm, 256), lambda i: (i, 0)),
            pl.BlockSpec((tm, 256), lambda i: (i, 0)),
        ] + t_specs,
        out_shape=[
            jax.ShapeDtypeStruct((m, D_U), F32),
            jax.ShapeDtypeStruct((m, 256), F32),
            jax.ShapeDtypeStruct((m, 256), F32),
        ] + t_shapes,
        compiler_params=_cparams(("parallel",)),
        name="inproj",
    )(x2d, norm_g, w_u, qg, kg, bd)


def _rglru_kernel(xa_ref, ga_ref, conv0_ref, h0_ref, cw_ref, cb_ref, wri_ref, bri_ref, lam_ref,
                  ya_ref, convn_ref, hl_ref, xbuf, abuf, bbuf, hbuf, hcar, *, tt):
    ti = pl.program_id(1)
    nt = pl.num_programs(1)

    @pl.when(ti == 0)
    def _():
        xbuf[5:8, :] = conv0_ref[0]
        hcar[...] = jnp.broadcast_to(h0_ref[0], (SUBLANES, A_WIDTH))

    xa = xa_ref[...]
    xbuf[8:8 + tt, :] = xa
    cw = cw_ref[...]
    xc = jnp.broadcast_to(cb_ref[...], (tt, A_WIDTH))
    for j in range(CONV_WIDTH):
        xc = xc + xbuf[5 + j:5 + j + tt, :] * cw[j:j + 1, :]
    xbuf[5:8, :] = xa[tt - 3:tt, :]

    gates = jnp.dot(xc.astype(BF16), wri_ref[...], preferred_element_type=F32) + bri_ref[...]
    r = _sigmoid(gates[:, :A_WIDTH])
    ig = _sigmoid(gates[:, A_WIDTH:])
    log_a = (-RG_C) * r * _softplus(-lam_ref[...])
    a = jnp.exp(log_a)
    abuf[...] = a
    bbuf[...] = jnp.sqrt(-jnp.tanh(log_a) * (a * a + 1.0)) * (ig * xc)

    rows = _row_iota((SUBLANES, A_WIDTH))

    def group(gi, carry):
        r0 = pl.multiple_of(gi * SUBLANES, SUBLANES)
        a = abuf[pl.ds(r0, SUBLANES), :]
        b = bbuf[pl.ds(r0, SUBLANES), :]
        for d in (1, 2, 4):
            a_sh = pltpu.roll(a, d, 0)
            b_sh = pltpu.roll(b, d, 0)
            keep = rows >= d
            b = jnp.where(keep, a * b_sh + b, b)
            a = jnp.where(keep, a * a_sh, a)
        h = a * hcar[...] + b
        hbuf[pl.ds(r0, SUBLANES), :] = h
        hcar[...] = jnp.broadcast_to(h[SUBLANES - 1:SUBLANES, :], (SUBLANES, A_WIDTH))
        return carry

    lax.fori_loop(0, tt // SUBLANES, group, 0)

    ya_ref[...] = hbuf[...] * _silu(ga_ref[...])

    @pl.when(ti == nt - 1)
    def _():
        convn_ref[0] = xa[tt - 3:tt, :]
        hl_ref[0] = hcar[0:1, :]


def _rglru(u, conv0, h0, cw, cb, wri, bri, lam, bsz, t, tt):
    nt = t // tt
    kern = functools.partial(_rglru_kernel, tt=tt)
    const = lambda b, i: (0, 0)
    return pl.pallas_call(
        kern,
        grid=(bsz, nt),
        in_specs=[
            pl.BlockSpec((tt, A_WIDTH), lambda b, i: (b * nt + i, U_XA // A_WIDTH)),
            pl.BlockSpec((tt, A_WIDTH), lambda b, i: (b * nt + i, U_GA // A_WIDTH)),
            pl.BlockSpec((1, 3, A_WIDTH), lambda b, i: (b, 0, 0)),
            pl.BlockSpec((1, 1, A_WIDTH), lambda b, i: (b, 0, 0)),
            pl.BlockSpec((CONV_WIDTH, A_WIDTH), const),
            pl.BlockSpec((1, A_WIDTH), const),
            pl.BlockSpec((A_WIDTH, 2 * A_WIDTH), const),
            pl.BlockSpec((1, 2 * A_WIDTH), const),
            pl.BlockSpec((1, A_WIDTH), const),
        ],
        out_specs=[
            pl.BlockSpec((tt, A_WIDTH), lambda b, i: (b * nt + i, 0)),
            pl.BlockSpec((1, 3, A_WIDTH), lambda b, i: (b, 0, 0)),
            pl.BlockSpec((1, 1, A_WIDTH), lambda b, i: (b, 0, 0)),
        ],
        out_shape=[
            jax.ShapeDtypeStruct((bsz * t, A_WIDTH), F32),
            jax.ShapeDtypeStruct((bsz, 3, A_WIDTH), F32),
            jax.ShapeDtypeStruct((bsz, 1, A_WIDTH), F32),
        ],
        scratch_shapes=[
            pltpu.VMEM((tt + 8, A_WIDTH), F32),
            pltpu.VMEM((tt, A_WIDTH), F32),
            pltpu.VMEM((tt, A_WIDTH), F32),
            pltpu.VMEM((tt, A_WIDTH), F32),
            pltpu.VMEM((SUBLANES, A_WIDTH), F32),
        ],
        compiler_params=_cparams(("parallel", "arbitrary")),
        name="rglru",
    )(u, u, conv0, h0, cw, cb, wri, bri, lam)


def _kmean_kernel(k_ref, o_ref, *, nblk):
    k = k_ref[...].reshape(nblk, MOBA_BLOCK, 256)
    o_ref[...] = jnp.sum(k, axis=1) * (1.0 / MOBA_BLOCK)


def _kmean(k2d, nblk):
    m = k2d.shape[0]
    rows = nblk * MOBA_BLOCK
    return pl.pallas_call(
        functools.partial(_kmean_kernel, nblk=nblk),
        grid=(m // rows,),
        in_specs=[pl.BlockSpec((rows, 256), lambda i: (i, 0))],
        out_specs=pl.BlockSpec((nblk, 256), lambda i: (i, 0)),
        out_shape=jax.ShapeDtypeStruct((m // MOBA_BLOCK, 256), F32),
        compiler_params=_cparams(("parallel",)),
        name="kmean",
    )(k2d)


def _select_kernel(q_ref, k_ref, v_ref, w_ref, shift_ref, qa_ref, ka_ref, va_ref, *, ts, rs):
    tile0 = pl.program_id(1) * ts
    w = w_ref[0]
    neg_shift = -shift_ref[...]

    def sub_tile(si, carry):
        r0 = pl.multiple_of(si * rs, rs)
        t0 = tile0 + r0
        q = q_ref[pl.ds(r0, rs), :]
        sc = _dot_nt(w, q, precision=HIGHEST)
        shp = (B_HEADS * NB_MAX, rs)
        j = _row_iota(shp) & (NB_MAX - 1)
        n_full = (t0 + _lane_iota(shp)) >> 8
        valid = j < n_full
        scm = jnp.where(valid, sc, -jnp.inf)
        def over_sublanes(op, v):
            for d in (4, 2, 1):
                v = op(v, pltpu.roll(v, d, 0))
            return v

        nv = NB_MAX // SUBLANES
        js = [j[SUBLANES * v:SUBLANES * (v + 1)] for v in range(nv)]
        nf = n_full[0:SUBLANES]
        picked = []
        for h in range(B_HEADS):
            xs = [scm[h * NB_MAX + SUBLANES * v:h * NB_MAX + SUBLANES * (v + 1)] for v in range(nv)]
            sel = [jnp.zeros((SUBLANES, rs), jnp.int32) for _ in range(nv)]
            for r in range(MOBA_TOPK):
                mx = over_sublanes(jnp.maximum, functools.reduce(jnp.maximum, xs))
                cand = [jnp.where(xs[v] == mx, js[v], NB_MAX) for v in range(nv)]
                first = over_sublanes(jnp.minimum, functools.reduce(jnp.minimum, cand))
                for v in range(nv):
                    hit = js[v] == first
                    sel[v] = jnp.where(hit & (nf > r), 1, sel[v])
                    xs[v] = jnp.where(hit, -jnp.inf, xs[v])
            picked.extend(sel)
        allowed = (jnp.concatenate(picked, axis=0) > 0) | (j == n_full)
        bias = jnp.where(allowed, 0.0, NEG).T

        shp1 = (rs, LANES)
        lane = _lane_iota(shp1)
        pos1 = t0 + _row_iota(shp1)
        pos_hi = (pos1 >> 6).astype(F32)
        pos_lo = (pos1 & 63).astype(F32)
        blk_onehot = jnp.where((lane - 64) == (pos1 >> 8), 1.0, 0.0)
        kfeat = jnp.where(lane == 96, pos_hi, jnp.where(lane == 97, pos_lo,
                          jnp.where((lane >= 98) & (lane <= 100), 1.0, 0.0)))
        kextra = jnp.where(lane < 96, blk_onehot, kfeat)
        vextra = jnp.where(lane == 64, 1.0, 0.0)
        k = k_ref[pl.ds(r0, rs), :]
        v = v_ref[pl.ds(r0, rs), :]
        for g in range(B_KV_HEADS):
            kt = k[:, (g // 2) * LANES:(g // 2 + 1) * LANES]
            vt = v[:, (g // 2) * LANES:(g // 2 + 1) * LANES]
            if g % 2:
                kt = pltpu.roll(kt, 64, 1)
                vt = pltpu.roll(vt, 64, 1)
            ka_ref[0, g, pl.ds(r0, rs), :] = jnp.where(lane < 64, kt, kextra).astype(BF16)
            va_ref[0, g, pl.ds(r0, rs), :] = jnp.where(lane < 64, vt, vextra).astype(BF16)
            qt = q[:, g * LANES:(g + 1) * LANES]
            for h2 in range(2):
                h = 2 * g + h2
                base = qt if h2 == 0 else pltpu.roll(qt, 64, 1)
                bt = bias[:, (h // 4) * LANES:(h // 4 + 1) * LANES]
                sh = (64 - (h % 4) * NB_MAX) % LANES
                if sh:
                    bt = pltpu.roll(bt, sh, 1)
                slope = 2.0 ** (-(h + 1))
                qfeat = jnp.where(lane == 96, slope * 64.0, jnp.where(lane == 97, slope,
                                  jnp.where(lane == 98, (-slope * 64.0) * pos_hi,
                                            jnp.where(lane == 99, (-slope) * pos_lo,
                                                      jnp.where(lane == 100, neg_shift, 0.0)))))
                aug = jnp.where(lane < 64, base, jnp.where(lane < 96, bt, qfeat))
                qa_ref[0, h, pl.ds(r0, rs), :] = aug.astype(BF16)
        return carry

    lax.fori_loop(0, ts // rs, sub_tile, 0)


def _select(u, k2d, v2d, wsel, shift, bsz, s, ts):
    nt = s // ts
    return pl.pallas_call(
        functools.partial(_select_kernel, ts=ts, rs=min(ts, LANES)),
        grid=(bsz, nt),
        in_specs=[
            pl.BlockSpec((ts, 512), lambda b, i: (b * nt + i, U_QB // 512)),
            pl.BlockSpec((ts, 256), lambda b, i: (b * nt + i, 0)),
            pl.BlockSpec((ts, 256), lambda b, i: (b * nt + i, 0)),
            pl.BlockSpec((1, B_HEADS * NB_MAX, 512), lambda b, i: (b, 0, 0)),
            pl.BlockSpec((1, LANES), lambda b, i: (0, 0)),
        ],
        out_specs=[
            pl.BlockSpec((1, B_HEADS, ts, LANES), lambda b, i: (b, 0, i, 0)),
            pl.BlockSpec((1, B_KV_HEADS, ts, LANES), lambda b, i: (b, 0, i, 0)),
            pl.BlockSpec((1, B_KV_HEADS, ts, LANES), lambda b, i: (b, 0, i, 0)),
        ],
        out_shape=[
            jax.ShapeDtypeStruct((bsz, B_HEADS, s, LANES), BF16),
            jax.ShapeDtypeStruct((bsz, B_KV_HEADS, s, LANES), BF16),
            jax.ShapeDtypeStruct((bsz, B_KV_HEADS, s, LANES), BF16),
        ],
        compiler_params=_cparams(("parallel", "parallel")),
        name="moba_select",
    )(u, k2d, v2d, wsel, shift)


ATTN_KV_BLOCKS = 4
ATTN_GROUPS = 2
SHIFT_LIMIT = 30.0


def _attn_finish(o_ref, acc, gi, tq):
    res = acc / acc[:, HEAD_DIM:HEAD_DIM + 1]
    lane = _lane_iota((tq, LANES))
    o_ref[:, gi * LANES:(gi + 1) * LANES] = jnp.where(lane < 64, res[:tq], pltpu.roll(res[tq:], 64, 1))


def _attn_online(qa_ref, ka_ref, va_ref, o_ref, m_sc, acc_sc, tq):
    qi = pl.program_id(2)
    rows = 2 * tq
    for gi in range(ATTN_GROUPS):
        q = qa_ref[0, 2 * gi:2 * gi + 2].reshape(rows, LANES)

        k0 = ka_ref[0, gi, pl.ds(pl.multiple_of(qi * tq, tq), tq), :]
        v0 = va_ref[0, gi, pl.ds(pl.multiple_of(qi * tq, tq), tq), :]
        s = _dot_nt(q, k0)
        shp = (rows, tq)
        qpos = _row_iota(shp) & (tq - 1)
        s = jnp.where(_lane_iota(shp) <= qpos, s, NEG)
        m0 = jnp.max(s, axis=-1, keepdims=True)
        p = jnp.exp(s - m0)
        m_sc[gi] = jnp.broadcast_to(m0, (rows, LANES))
        acc_sc[gi] = jnp.dot(p.astype(BF16), v0, preferred_element_type=F32)

        def body(jb, carry, gi=gi, q=q):
            r0 = pl.multiple_of(jb * tq, tq)
            kj = ka_ref[0, gi, pl.ds(r0, tq), :]
            vj = va_ref[0, gi, pl.ds(r0, tq), :]
            sj = _dot_nt(q, kj)
            m_prev = m_sc[gi]
            m_new = jnp.maximum(m_prev, jnp.max(sj, axis=-1, keepdims=True))
            alpha = jnp.exp(m_prev - m_new)
            pj = jnp.exp(sj - m_new[:, 0:1])
            acc_sc[gi] = alpha * acc_sc[gi] + jnp.dot(pj.astype(BF16), vj, preferred_element_type=F32)
            m_sc[gi] = m_new
            return carry

        lax.fori_loop(0, qi, body, 0)
        _attn_finish(o_ref, acc_sc[gi], gi, tq)


def _attn_shifted(qa_ref, ka_ref, va_ref, o_ref, acc_sc, tq):
    qi = pl.program_id(2)
    rows = 2 * tq
    qs = [qa_ref[0, 2 * gi:2 * gi + 2].reshape(rows, LANES) for gi in range(ATTN_GROUPS)]
    acc_sc[...] = jnp.zeros((ATTN_GROUPS, rows, LANES), F32)

    def step(r0, nblk, own_last):
        keys = nblk * tq
        if own_last:
            shp = (rows, keys)
            visible = _lane_iota(shp) - (keys - tq) <= (_row_iota(shp) & (tq - 1))
        for gi in range(ATTN_GROUPS):
            kj = ka_ref[0, gi, pl.ds(r0, keys), :]
            vj = va_ref[0, gi, pl.ds(r0, keys), :]
            s = _dot_nt(qs[gi], kj)
            p = jnp.exp(jnp.where(visible, s, NEG) if own_last else s)
            acc_sc[gi] += jnp.dot(p.astype(BF16), vj, preferred_element_type=F32)

    def body(jp, carry):
        step(pl.multiple_of(jp * (ATTN_KV_BLOCKS * tq), ATTN_KV_BLOCKS * tq), ATTN_KV_BLOCKS, False)
        return carry

    n_full = qi // ATTN_KV_BLOCKS
    lax.fori_loop(0, n_full, body, 0)
    tail0 = pl.multiple_of(n_full * (ATTN_KV_BLOCKS * tq), ATTN_KV_BLOCKS * tq)
    for nblk in range(1, ATTN_KV_BLOCKS + 1):
        @pl.when(qi - n_full * ATTN_KV_BLOCKS + 1 == nblk)
        def _():
            step(tail0, nblk, True)

    for gi in range(ATTN_GROUPS):
        _attn_finish(o_ref, acc_sc[gi], gi, tq)


def _attn_kernel(mode_ref, qa_ref, ka_ref, va_ref, o_ref, m_sc, acc_sc, *, tq):
    @pl.when(mode_ref[0] == 1)
    def _():
        _attn_shifted(qa_ref, ka_ref, va_ref, o_ref, acc_sc, tq)

    @pl.when(mode_ref[0] != 1)
    def _():
        _attn_online(qa_ref, ka_ref, va_ref, o_ref, m_sc, acc_sc, tq)


def _attn(mode, qa, ka, va, bsz, s):
    tq = MOBA_BLOCK
    nq = s // tq
    ng = ATTN_GROUPS
    assert nq % ATTN_KV_BLOCKS == 0 and B_KV_HEADS % ng == 0
    grid_spec = pltpu.PrefetchScalarGridSpec(
        num_scalar_prefetch=1,
        grid=(bsz, B_KV_HEADS // ng, nq),
        in_specs=[
            pl.BlockSpec((1, 2 * ng, tq, LANES), lambda b, g, i, mode: (b, g, i, 0)),
            pl.BlockSpec((1, ng, s, LANES), lambda b, g, i, mode: (b, g, 0, 0)),
            pl.BlockSpec((1, ng, s, LANES), lambda b, g, i, mode: (b, g, 0, 0)),
        ],
        out_specs=pl.BlockSpec((tq, ng * LANES), lambda b, g, i, mode: (b * nq + i, g)),
        scratch_shapes=[
            pltpu.VMEM((ng, 2 * tq, LANES), F32),
            pltpu.VMEM((ng, 2 * tq, LANES), F32),
        ],
    )
    return pl.pallas_call(
        functools.partial(_attn_kernel, tq=tq),
        grid_spec=grid_spec,
        out_shape=jax.ShapeDtypeStruct((bsz * s, 512), F32),
        compiler_params=_cparams(("parallel", "parallel", "arbitrary")),
        name="moba_attn",
    )(mode, qa, ka, va)


PAGES_PER_STEP = 16


def _page_copies(pt_ref, src_hbm, buf, sem, b, c, slot):
    return [pltpu.make_async_copy(src_hbm.at[pt_ref[b, c * PAGES_PER_STEP + p]], buf.at[slot, p], sem.at[slot])
            for p in range(PAGES_PER_STEP)]


def _sample_moba_kernel(pt_ref, q_ref, kn_ref, vn_ref, slope_ref, ck_hbm, cv_hbm, o_ref,
                        kst, vst, kres, kmt, ksem, vsem, *, nseq, nch, nblk, tnew, past):
    b = pl.program_id(0)
    cur = b % 2
    nxt = 1 - cur
    has_next = b + 1 < nseq
    rows = 64
    blocks = PAGES_PER_STEP * PAGE_SIZE // MOBA_BLOCK
    ppb = MOBA_BLOCK // PAGE_SIZE
    km_lane = _lane_iota((256, LANES)) & (nblk - 1)

    def k_copies(seq, c, st):
        return _page_copies(pt_ref, ck_hbm, kst, ksem, seq, c, st)

    def v_copies(c, st):
        return _page_copies(pt_ref, cv_hbm, vst, vsem, b, c, st)

    def stage_k_chunk(res_slot, c, st):
        kres[res_slot, pl.ds(c * PAGES_PER_STEP, PAGES_PER_STEP)] = kst[st].astype(BF16)
        tile = jnp.zeros((256, LANES), F32)
        for jj in range(blocks):
            x = kst[st, ppb * jj]
            for pp in range(1, ppb):
                x = x + kst[st, ppb * jj + pp]
            mean = jnp.sum(x, axis=-1, keepdims=True) * (1.0 / MOBA_BLOCK)
            tile = jnp.where(km_lane == c * blocks + jj, mean, tile)
        if c == 0:
            kmt[res_slot] = tile
        else:
            kmt[res_slot] = kmt[res_slot] + tile

    @pl.when(b == 0)
    def _():
        for cp in k_copies(0, 0, 0):
            cp.start()
        for c in range(nch):
            st = c % 2
            if c + 1 < nch:
                for cp in k_copies(0, c + 1, 1 - st):
                    cp.start()
            for cp in k_copies(0, c, st):
                cp.wait()
            stage_k_chunk(0, c, st)

    @pl.when(has_next)
    def _():
        for cp in k_copies(b + 1, 0, 0):
            cp.start()

    for cp in v_copies(0, 0):
        cp.start()

    qf = q_ref[0]
    qb = qf.astype(BF16)
    slope = slope_ref[...]
    shp = (rows, LANES)
    lane = _lane_iota(shp)
    tok = _row_iota(shp) & (tnew - 1)

    sc = jnp.dot(qf, kmt[cur], preferred_element_type=F32, precision=HIGHEST)
    j = lane & (nblk - 1)
    rank = jnp.zeros(shp, jnp.int32)
    for s in range(1, nblk):
        a = pltpu.roll(sc, 2 * nblk - s, 1)
        wrapped = j >= nblk - s
        rank = rank + jnp.where(wrapped, jnp.where(a >= sc, 1, 0), jnp.where(a > sc, 1, 0))
    bias = jnp.where(rank < MOBA_TOPK, 0.0, NEG)

    s_own = _dot_nt(qb, kn_ref[0].astype(BF16))
    s_own = s_own - slope * (tok - lane).astype(F32)
    s_own = jnp.where((lane <= tok) & (lane < tnew), s_own, NEG)
    m = jnp.max(s_own, axis=-1, keepdims=True)
    p = jnp.exp(s_own - m)
    l = jnp.sum(p, axis=-1, keepdims=True)
    acc = jnp.dot(p.astype(BF16), vn_ref[0].astype(BF16), preferred_element_type=F32)

    keys = PAGES_PER_STEP * PAGE_SIZE
    blocks = keys // MOBA_BLOCK
    kshape = (rows, keys)
    klane = _lane_iota(kshape)
    qpos = past + (_row_iota(kshape) & (tnew - 1))
    slope_k = jnp.broadcast_to(slope[:, 0:1], kshape)
    for c in range(nch):
        st = c % 2
        if c + 1 < nch:
            for cp in v_copies(c + 1, 1 - st):
                cp.start()

            @pl.when(has_next)
            def _():
                for cp in k_copies(b + 1, c + 1, 1 - st):
                    cp.start()

        for cp in v_copies(c, st):
            cp.wait()
        vc = vst[st].astype(BF16)
        s_c = jnp.concatenate(
            [jnp.dot(qb, kres[cur, c * PAGES_PER_STEP + pg], preferred_element_type=F32)
             for pg in range(PAGES_PER_STEP)], axis=1)
        s_c = s_c - slope_k * (qpos - (c * keys + klane)).astype(F32)
        sel = jnp.concatenate(
            [jnp.broadcast_to(bias[:, c * blocks + jj:c * blocks + jj + 1], (rows, MOBA_BLOCK))
             for jj in range(blocks)], axis=1)
        s_c = s_c + sel
        m_new = jnp.maximum(m, jnp.max(s_c, axis=-1, keepdims=True))
        alpha = jnp.exp(m - m_new)
        p = jnp.exp(s_c - m_new)
        l = alpha * l + jnp.sum(p, axis=-1, keepdims=True)
        pb = p.astype(BF16)
        pv = _dot_nt(pb[:, 0:PAGE_SIZE], vc[0])
        for pg in range(1, PAGES_PER_STEP):
            pv = pv + _dot_nt(pb[:, pg * PAGE_SIZE:(pg + 1) * PAGE_SIZE], vc[pg])
        acc = alpha * acc + pv
        m = m_new

        @pl.when(has_next)
        def _():
            for cp in k_copies(b + 1, c, st):
                cp.wait()
            stage_k_chunk(nxt, c, st)

    res = acc / l
    oshape = (rows, 256)
    own = (_lane_iota(oshape) >> 6) == (_row_iota(oshape) >> 4)
    res = jnp.where(own, res, 0.0)
    folded = res + pltpu.roll(res, 64, 1) + pltpu.roll(res, 128, 1) + pltpu.roll(res, 192, 1)
    o_ref[0] = folded[:, :HEAD_DIM]


def _sample_moba(pt, qbd, knew, vnew, slopes, ck, cv, bsz, n_pages, tnew):
    nch = n_pages // PAGES_PER_STEP
    nblk = n_pages * PAGE_SIZE // MOBA_BLOCK
    past = n_pages * PAGE_SIZE
    grid_spec = pltpu.PrefetchScalarGridSpec(
        num_scalar_prefetch=1,
        grid=(bsz,),
        in_specs=[
            pl.BlockSpec((1, 64, 256), lambda b, pt: (b, 0, 0)),
            pl.BlockSpec((1, LANES, 256), lambda b, pt: (b, 0, 0)),
            pl.BlockSpec((1, LANES, 256), lambda b, pt: (b, 0, 0)),
            pl.BlockSpec((64, LANES), lambda b, pt: (0, 0)),
            pl.BlockSpec(memory_space=pl.ANY),
            pl.BlockSpec(memory_space=pl.ANY),
        ],
        out_specs=pl.BlockSpec((1, 64, HEAD_DIM), lambda b, pt: (b, 0, 0)),
        scratch_shapes=[
            pltpu.VMEM((2, PAGES_PER_STEP, 256, PAGE_SIZE), F32),
            pltpu.VMEM((2, PAGES_PER_STEP, 256, PAGE_SIZE), F32),
            pltpu.VMEM((2, n_pages, 256, PAGE_SIZE), BF16),
            pltpu.VMEM((2, 256, LANES), F32),
            pltpu.SemaphoreType.DMA((2,)),
            pltpu.SemaphoreType.DMA((2,)),
        ],
    )
    return pl.pallas_call(
        functools.partial(_sample_moba_kernel, nseq=bsz, nch=nch, nblk=nblk, tnew=tnew, past=past),
        grid_spec=grid_spec,
        out_shape=jax.ShapeDtypeStruct((bsz, 64, HEAD_DIM), F32),
        compiler_params=_cparams(("arbitrary",)),
        name="sample_moba",
    )(pt, qbd, knew, vnew, slopes, ck, cv)


def _mlstm_kernel(q_ref, k_ref, v_ref, if_ref, o_ref, g_ref, bias_ref, og_ref, c0_ref, m0_ref,
                  y_ref, cout_ref, mout_ref, c_sc, m_sc, *, tc, L):
    ti = pl.program_id(1)
    nt = pl.num_programs(1)

    @pl.when(ti == 0)
    def _():
        c_sc[...] = c0_ref[0]
        m_sc[...] = m0_ref[0]

    gates = if_ref[...] + bias_ref[...]
    lf = jnp.minimum(gates, 0.0) - jnp.log1p(jnp.exp(-jnp.abs(gates)))
    tri = _lane_iota((L, L)) <= _row_iota((L, L))
    tri_f = jnp.where(tri, 1.0, 0.0)
    lane = _lane_iota((L, LANES))
    og = og_ref[...]
    for c in range(tc // L):
        r0 = c * L
        fcum = jnp.dot(tri_f, lf[r0:r0 + L], preferred_element_type=F32, precision=HIGHEST)
        z = gates[r0:r0 + L] - pltpu.roll(fcum, LANES - 4, 1)
        qt = q_ref[r0:r0 + L, :]
        kt = k_ref[r0:r0 + L, :] * (HEAD_DIM ** -0.5)
        vt = v_ref[r0:r0 + L, :]
        ot = _sigmoid(o_ref[r0:r0 + L, :])
        for pair in range(C_HEADS // 2):
            ytile = jnp.zeros((L, LANES), F32)
            for h2 in range(2):
                h = 2 * pair + h2
                sl = slice(pair * LANES, (pair + 1) * LANES)
                qh, kh, vh, oh = qt[:, sl], kt[:, sl], vt[:, sl], ot[:, sl]
                if h2:
                    qh, kh, vh, oh = [pltpu.roll(a, 64, 1) for a in (qh, kh, vh, oh)]
                qh = jnp.where(lane < 64, qh, 0.0).astype(BF16)
                kh = jnp.where(lane < 64, kh, 0.0).astype(BF16)
                vaug = jnp.where(lane < 64, vh, jnp.where(lane == 64, 1.0, 0.0))
                m_prev = m_sc[h:h + 1, 0:1]
                f_col = fcum[:, 4 + h:5 + h]
                z_col = z[:, h:h + 1]
                pick = jnp.where(lane == h, 1.0, 0.0)
                z_row = _dot_nt(pick, z, precision=HIGHEST)
                d_log = jnp.where(tri, f_col + z_row, -jnp.inf)
                inter = f_col + m_prev
                m_t = jnp.maximum(jnp.max(d_log, axis=-1, keepdims=True), inter)
                w_intra = _dot_nt(qh, kh) * jnp.exp(d_log - m_t)
                w_inter = jnp.exp(inter - m_t)
                c_prev = c_sc[h]
                num = (w_inter * jnp.dot(qh, c_prev.astype(BF16), preferred_element_type=F32)
                       + jnp.dot(w_intra.astype(BF16), vaug.astype(BF16), preferred_element_type=F32))
                den = num[:, HEAD_DIM:HEAD_DIM + 1]
                hh = num / jnp.maximum(jnp.abs(den), jnp.exp(-m_t))
                f_tot = f_col[L - 1:L, :]
                w_log = f_tot + z_col
                m_new = jnp.maximum(f_tot + m_prev, jnp.max(w_log, axis=0, keepdims=True))
                decay = jnp.exp(f_tot + m_prev - m_new)
                w = jnp.exp(w_log - m_new)
                upd = lax.dot_general(kh, (w * vaug).astype(BF16), (((0,), (0,)), ((), ())),
                                      preferred_element_type=F32)
                c_sc[h] = decay * c_prev + upd
                m_sc[h:h + 1, :] = jnp.broadcast_to(m_new, (1, LANES))
                ho = jnp.where(lane < 64, hh * oh, 0.0)
                ms = jnp.sum(ho * ho, axis=-1, keepdims=True) * (1.0 / HEAD_DIM)
                yh = ho * lax.rsqrt(ms + NORM_EPS)
                ytile = ytile + (pltpu.roll(yh, 64, 1) if h2 else yh)
            sl = slice(pair * LANES, (pair + 1) * LANES)
            y_ref[r0:r0 + L, sl] = ytile * og[:, sl] * _silu(g_ref[r0:r0 + L, sl])

    @pl.when(ti == nt - 1)
    def _():
        cout_ref[0] = c_sc[...]
        mout_ref[0] = m_sc[...]


def _mlstm(u, gate_bias, onorm_g, c0aug, m0, bsz, t, tc, L, wide):
    nt = t // tc
    const = lambda b, i: (0, 0)
    ucol = lambda off, w: (lambda b, i: (b * nt + i, off // w))
    return pl.pallas_call(
        functools.partial(_mlstm_wide_kernel if wide else _mlstm_kernel, tc=tc, L=L),
        grid=(bsz, nt),
        in_specs=[
            pl.BlockSpec((tc, 256), ucol(U_CQ, 256)),
            pl.BlockSpec((tc, 256), ucol(U_CK, 256)),
            pl.BlockSpec((tc, 256), ucol(U_CV, 256)),
            pl.BlockSpec((tc, LANES), ucol(U_CIF, LANES)),
            pl.BlockSpec((tc, 256), ucol(U_CO, 256)),
            pl.BlockSpec((tc, 256), ucol(U_CG, 256)),
            pl.BlockSpec((1, LANES), const),
            pl.BlockSpec((1, 256), const),
            pl.BlockSpec((1, C_HEADS, LANES, LANES), lambda b, i: (b, 0, 0, 0)),
            pl.BlockSpec((1, SUBLANES, LANES), lambda b, i: (b, 0, 0)),
        ],
        out_specs=[
            pl.BlockSpec((tc, 256), lambda b, i: (b * nt + i, 0)),
            pl.BlockSpec((1, C_HEADS, LANES, LANES), lambda b, i: (b, 0, 0, 0)),
            pl.BlockSpec((1, SUBLANES, LANES), lambda b, i: (b, 0, 0)),
        ],
        out_shape=[
            jax.ShapeDtypeStruct((bsz * t, 256), F32),
            jax.ShapeDtypeStruct((bsz, C_HEADS, LANES, LANES), F32),
            jax.ShapeDtypeStruct((bsz, SUBLANES, LANES), F32),
        ],
        scratch_shapes=[
            pltpu.VMEM((C_HEADS, LANES, LANES), F32),
            pltpu.VMEM((SUBLANES, LANES), F32),
        ],
        compiler_params=_cparams(("parallel", "arbitrary")),
        name="mlstm",
    )(u, u, u, u, u, u, gate_bias, onorm_g, c0aug, m0)


def _mlstm_wide_kernel(q_ref, k_ref, v_ref, if_ref, o_ref, g_ref, bias_ref, og_ref, c0_ref, m0_ref,
                       y_ref, cout_ref, mout_ref, c_sc, m_sc, *, tc, L):
    ti = pl.program_id(1)
    nt = pl.num_programs(1)

    @pl.when(ti == 0)
    def _():
        c_sc[...] = c0_ref[0]
        m_sc[...] = m0_ref[0]

    gates = if_ref[...] + bias_ref[...]
    lf = jnp.minimum(gates, 0.0) - jnp.log1p(jnp.exp(-jnp.abs(gates)))
    in_chunk = _row_iota((tc, LANES)) & (L - 1)
    fcum = lf
    d = 1
    while d < L:
        fcum = fcum + jnp.where(in_chunk >= d, pltpu.roll(fcum, d, 0), 0.0)
        d *= 2
    z = gates - pltpu.roll(fcum, LANES - 4, 1)
    z_t = z.T
    f_rep, z_rep, zmax_rep = [], [], []
    for h in range(C_HEADS):
        fh = jnp.broadcast_to(fcum[:, 4 + h:5 + h], (tc, LANES))
        zh = jnp.broadcast_to(z[:, h:h + 1], (tc, LANES))
        zm = zh
        d = 1
        while d < L:
            zm = jnp.maximum(zm, jnp.where(in_chunk >= d, pltpu.roll(zm, d, 0), -jnp.inf))
            d *= 2
        f_rep.append(fh)
        z_rep.append(zh)
        zmax_rep.append(zm)
    k_t = (k_ref[...] * (HEAD_DIM ** -0.5)).T

    tri = _lane_iota((L, L)) <= _row_iota((L, L))
    lane = _lane_iota((L, LANES))
    srow = _row_iota((LANES, L))
    og = og_ref[...]
    for c in range(tc // L):
        r0 = c * L
        for pair in range(C_HEADS // 2):
            sl = slice(pair * LANES, (pair + 1) * LANES)
            qt = q_ref[r0:r0 + L, sl]
            vt = v_ref[r0:r0 + L, sl]
            ot = _sigmoid(o_ref[r0:r0 + L, sl])
            ktt = k_t[pair * LANES:(pair + 1) * LANES, r0:r0 + L]
            ytile = jnp.zeros((L, LANES), F32)
            for h2 in range(2):
                h = 2 * pair + h2
                mine = (lane >= 64) if h2 else (lane < 64)
                ones_col = 0 if h2 else HEAD_DIM
                qh = jnp.where(mine, qt, 0.0).astype(BF16)
                kth = jnp.where((srow >= 64) if h2 else (srow < 64), ktt, 0.0).astype(BF16)
                vaug = jnp.where(mine, vt, jnp.where(lane == ones_col, 1.0, 0.0))
                m_prev = m_sc[h:h + 1, :]
                f_t = f_rep[h][r0:r0 + L]
                z_s = z_rep[h][r0:r0 + L]
                z_row = z_t[h:h + 1, r0:r0 + L]
                a_t = jnp.maximum(zmax_rep[h][r0:r0 + L], m_prev)
                decay_intra = jnp.exp(jnp.where(tri, z_row - a_t, -jnp.inf))
                w_intra = jnp.dot(qh, kth, preferred_element_type=F32) * decay_intra
                w_inter = jnp.exp(m_prev - a_t)
                c_prev = c_sc[h]
                num = (w_inter * jnp.dot(qh, c_prev.astype(BF16), preferred_element_type=F32)
                       + jnp.dot(w_intra.astype(BF16), vaug.astype(BF16), preferred_element_type=F32))
                den = num[:, ones_col:ones_col + 1]
                hh = num / jnp.maximum(jnp.abs(den), jnp.exp(-(f_t + a_t)))
                a_last = a_t[L - 1:L, :]
                decay = jnp.exp(m_prev - a_last)
                w = jnp.exp(z_s - a_last)
                c_sc[h] = decay * c_prev + jnp.dot(kth, (w * vaug).astype(BF16), preferred_element_type=F32)
                m_sc[h:h + 1, :] = f_t[L - 1:L, :] + a_last
                ho = jnp.where(mine, hh * ot, 0.0)
                ms = jnp.sum(ho * ho, axis=-1, keepdims=True) * (1.0 / HEAD_DIM)
                ytile = ytile + ho * lax.rsqrt(ms + NORM_EPS)
            y_ref[r0:r0 + L, sl] = ytile * og[:, sl] * _silu(g_ref[r0:r0 + L, sl])

    @pl.when(ti == nt - 1)
    def _():
        cout_ref[0] = c_sc[...]
        mout_ref[0] = m_sc[...]


def _outproj_kernel(x_ref, ya_ref, yb_ref, gb_ref, yc_ref, p_ref, wa_ref, wb_ref, wc_ref,
                    pg_ref, pp_ref, png_ref, o_ref):
    yb = yb_ref[...] * _silu(gb_ref[...])
    x = (x_ref[...]
         + jnp.dot(ya_ref[...].astype(BF16), wa_ref[...], preferred_element_type=F32)
         + jnp.dot(yb.astype(BF16), wb_ref[...], preferred_element_type=F32)
         + jnp.dot(yc_ref[...].astype(BF16), wc_ref[...], preferred_element_type=F32))
    gate = _sigmoid(jnp.dot(x.astype(BF16), pg_ref[...], preferred_element_type=F32))
    e = jnp.dot(p_ref[...].astype(BF16), pp_ref[...], preferred_element_type=F32)
    e = e * lax.rsqrt(jnp.mean(e * e, axis=-1, keepdims=True) + NORM_EPS) * png_ref[...]
    o_ref[...] = x + gate * e


def _outproj(x2d, ya, yb, u, yc, p_all, layer, wa, wb, wc, pg, pp, png, tm):
    m = x2d.shape[0]
    row = lambda i: (i, 0)
    p_row = lambda i: (layer * (m // tm) + i, 0)
    const = lambda i: (0, 0)
    return pl.pallas_call(
        _outproj_kernel,
        grid=(m // tm,),
        in_specs=[
            pl.BlockSpec((tm, D_MODEL), row),
            pl.BlockSpec((tm, 256), row),
            pl.BlockSpec((tm, 512), row),
            pl.BlockSpec((tm, 512), lambda i: (i, U_GB // 512)),
            pl.BlockSpec((tm, 256), row),
            pl.BlockSpec((tm, PLE_DIM), p_row),
            pl.BlockSpec((256, D_MODEL), const),
            pl.BlockSpec((512, D_MODEL), const),
            pl.BlockSpec((256, D_MODEL), const),
            pl.BlockSpec((D_MODEL, D_MODEL), const),
            pl.BlockSpec((PLE_DIM, D_MODEL), const),
            pl.BlockSpec((1, D_MODEL), const),
        ],
        out_specs=pl.BlockSpec((tm, D_MODEL), row),
        out_shape=jax.ShapeDtypeStruct((m, D_MODEL), F32),
        compiler_params=_cparams(("parallel",)),
        name="outproj",
    )(x2d, ya, yb, u, yc, p_all, wa, wb, wc, pg, pp, png)


def _block_diag(w):
    n, a, b = w.shape
    eye = jnp.eye(n, dtype=w.dtype)
    return jnp.einsum("nab,nm->namb", w, eye).reshape(n * a, n * b)


def _prep_layer(l, w):
    (norm_g, w_in, a_conv_w, a_conv_b, a_w_r, a_b_r, a_w_i, a_b_i, a_lambda, b_qnorm_g, b_knorm_g,
     c_b_i, c_b_f, c_onorm_g, w_out, ple_gate, ple_proj, ple_norm_g) = [a[l] for a in w]
    col = lambda i: w_in[:, _REF_OFFS[i]:_REF_OFFS[i + 1]]
    w_u = jnp.concatenate(
        [col(0), col(1), col(2), col(5), col(3), col(4), col(6), col(7), col(8), col(11), col(12),
         col(9), col(10), jnp.zeros((D_MODEL, LANES - 2 * C_HEADS), F32)], axis=1).astype(BF16)
    gate_bias = jnp.concatenate([c_b_i, c_b_f, jnp.zeros((LANES - 2 * C_HEADS,), F32)])[None, :]
    shift = 1.02 * HEAD_DIM ** 0.5 * jnp.max(jnp.abs(b_qnorm_g)) * jnp.max(jnp.abs(b_knorm_g)) + 0.5
    return dict(
        shift=shift,
        norm_g=norm_g[None, :], w_u=w_u,
        qg=jnp.tile(b_qnorm_g, B_HEADS)[None, :], kg=jnp.tile(b_knorm_g, B_KV_HEADS)[None, :],
        bd=jnp.asarray(np.kron(np.eye(B_HEADS), np.ones((HEAD_DIM, HEAD_DIM))), BF16),
        cw=a_conv_w, cb=a_conv_b[None, :],
        wri=jnp.concatenate([_block_diag(a_w_r), _block_diag(a_w_i)], axis=1).astype(BF16),
        bri=jnp.concatenate([a_b_r, a_b_i])[None, :], lam=a_lambda[None, :],
        gate_bias=gate_bias, onorm_g=c_onorm_g[None, :],
        wa=w_out[:256].astype(BF16), wb=w_out[256:768].astype(BF16), wc=w_out[768:].astype(BF16),
        pg=ple_gate.astype(BF16), pp=ple_proj.astype(BF16), png=ple_norm_g[None, :],
    )


def _pick_tile(n, cap):
    t = min(n, cap)
    while n % t:
        t //= 2
    return t


def _state_tiles(c, n, upper):
    pad_cols = jnp.zeros(c.shape[:3] + (HEAD_DIM - 1,), F32)
    rows = jnp.concatenate([n[..., None], pad_cols, c] if upper else [c, n[..., None], pad_cols], axis=-1)
    blank = jnp.zeros_like(rows)
    return jnp.concatenate([blank, rows] if upper else [rows, blank], axis=-2)


def _layer(x2d, p_all, layer, bsz, t, conv0, h0, c0, n0, m0, lw, attend):
    m = bsz * t
    tm = _pick_tile(m, 512)
    proj = _inproj(x2d, lw["norm_g"], lw["w_u"], lw["qg"], lw["kg"], lw["bd"], tm, bsz, t)
    u, k2d, v2d = proj[:3]
    if len(proj) > 3:
        k_state, v_state = [a.reshape(bsz, B_KV_HEADS, HEAD_DIM, t).transpose(0, 3, 1, 2) for a in proj[3:]]
    else:
        k_state, v_state = [a.reshape(bsz, t, B_KV_HEADS, HEAD_DIM) for a in (k2d, v2d)]
    ya, conv_new, h_last = _rglru(u, conv0, h0[:, None, :], lw["cw"], lw["cb"], lw["wri"], lw["bri"],
                                  lw["lam"], bsz, t, _pick_tile(t, 512))
    yb = attend(u, k2d, v2d)
    wide = t % LANES == 0
    L = LANES if wide else int(np.gcd(t, MLSTM_CHUNK))
    lo, hi = slice(0, HEAD_DIM), slice(HEAD_DIM, 2 * HEAD_DIM)
    if wide:
        even = _state_tiles(c0[:, 0::2], n0[:, 0::2], upper=False)
        odd = _state_tiles(c0[:, 1::2], n0[:, 1::2], upper=True)
        c0aug = jnp.stack([even, odd], axis=2).reshape(bsz, C_HEADS, LANES, LANES)
    else:
        c0aug = _state_tiles(c0, n0, upper=False)
    m0b = jnp.concatenate([jnp.broadcast_to(m0[:, :, None], (bsz, C_HEADS, LANES)),
                           jnp.zeros((bsz, SUBLANES - C_HEADS, LANES), F32)], axis=1)
    yc, caug, mout = _mlstm(u, lw["gate_bias"], lw["onorm_g"], c0aug, m0b, bsz, t, _pick_tile(t, 512), L, wide)
    if wide:
        c_new = jnp.stack([caug[:, 0, lo, lo], caug[:, 1, hi, hi], caug[:, 2, lo, lo], caug[:, 3, hi, hi]], axis=1)
        n_new = jnp.stack([caug[:, 0, lo, HEAD_DIM], caug[:, 1, hi, 0],
                           caug[:, 2, lo, HEAD_DIM], caug[:, 3, hi, 0]], axis=1)
    else:
        c_new, n_new = caug[:, :, lo, lo], caug[:, :, lo, HEAD_DIM]
    xo = _outproj(x2d, ya, yb, u, yc, p_all, layer, lw["wa"], lw["wb"], lw["wc"], lw["pg"], lw["pp"], lw["png"], tm)
    state = (k_state, v_state, conv_new, h_last[:, 0, :], c_new, n_new, mout[:, :C_HEADS, 0])
    return xo, state


def _attend_prompt(u, k2d, v2d, bsz, s, shift):
    nb = s // MOBA_BLOCK
    assert s % MOBA_BLOCK == 0 and nb <= NB_MAX
    kmean = _kmean(k2d, _pick_tile(bsz * nb, 8)).reshape(bsz, nb, B_KV_HEADS, HEAD_DIM)
    kmh = jnp.repeat(kmean, B_HEADS // B_KV_HEADS, axis=2).transpose(0, 2, 1, 3)
    kmh = jnp.pad(kmh, ((0, 0), (0, 0), (0, NB_MAX - nb), (0, 0)))
    wsel = jnp.einsum("bhjd,hg->bhjgd", kmh, jnp.eye(B_HEADS, dtype=F32)).reshape(bsz, B_HEADS * NB_MAX, 512)
    qa, ka, va = _select(u, k2d, v2d, wsel, jnp.full((1, LANES), shift, F32), bsz, s, _pick_tile(s, 512))
    mode = (shift <= SHIFT_LIMIT).astype(jnp.int32).reshape(1)
    return _attn(mode, qa, ka, va, bsz, s)


def _attend_sample(u, k2d, v2d, bsz, t, pt, ck, cv, slopes):
    n_pages = pt.shape[1]
    assert t == 8 and n_pages % PAGES_PER_STEP == 0
    nblk = n_pages * PAGE_SIZE // MOBA_BLOCK
    assert nblk == 64
    q = u[:, U_QB:U_QB + 512].reshape(bsz, t, B_KV_HEADS, 2, HEAD_DIM).transpose(0, 2, 3, 1, 4)
    qbd = jnp.einsum("bghtd,gk->bghtkd", q, jnp.eye(B_KV_HEADS, dtype=F32)).reshape(bsz, 64, 256)
    pad = ((0, 0), (0, LANES - t), (0, 0))
    knew = jnp.pad(k2d.reshape(bsz, t, 256), pad)
    vnew = jnp.pad(v2d.reshape(bsz, t, 256), pad)
    o = _sample_moba(pt, qbd, knew, vnew, slopes, ck, cv, bsz, n_pages, t)
    return o.reshape(bsz, B_HEADS, t, HEAD_DIM).transpose(0, 2, 1, 3).reshape(bsz * t, 512)


def kernel(x_prompt, x_sample, cache_k, cache_v, state_rglru_conv, state_rglru_h, state_mlstm_C, state_mlstm_n, state_mlstm_m, page_table, p_prompt, p_sample, norm_g, w_in, a_conv_w, a_conv_b, a_w_r, a_b_r, a_w_i, a_b_i, a_lambda, b_qnorm_g, b_knorm_g, c_b_i, c_b_f, c_onorm_g, w_out, ple_gate, ple_proj, ple_norm_g):
    weights = (norm_g, w_in, a_conv_w, a_conv_b, a_w_r, a_b_r, a_w_i, a_b_i, a_lambda, b_qnorm_g, b_knorm_g,
               c_b_i, c_b_f, c_onorm_g, w_out, ple_gate, ple_proj, ple_norm_g)
    depth = w_in.shape[0]
    bp, sp = x_prompt.shape[:2]
    bs, ts = x_sample.shape[:2]
    n_phys = cache_k.shape[1]
    ck = cache_k.transpose(0, 1, 3, 4, 2).reshape(depth * n_phys, 256, PAGE_SIZE)
    cv = cache_v.transpose(0, 1, 3, 4, 2).reshape(depth * n_phys, 256, PAGE_SIZE)
    slopes = jnp.asarray(np.repeat(2.0 ** -(np.arange(1, B_HEADS + 1)), ts)[:, None]
                         * np.ones((1, LANES)), F32)

    xp = x_prompt.reshape(bp * sp, D_MODEL)
    pp_all = p_prompt.reshape(depth * bp * sp, PLE_DIM)
    ps_all = p_sample.reshape(depth * bs * ts, PLE_DIM)
    xs = x_sample.reshape(bs * ts, D_MODEL)
    zeros = lambda *s: jnp.zeros(s, F32)
    sp_all, ss_all = [], []
    for l in range(depth):
        lw = _prep_layer(l, weights)
        xp, st_p = _layer(xp, pp_all, l, bp, sp,
                          zeros(bp, CONV_WIDTH - 1, A_WIDTH), zeros(bp, A_WIDTH),
                          zeros(bp, C_HEADS, HEAD_DIM, HEAD_DIM), zeros(bp, C_HEADS, HEAD_DIM), zeros(bp, C_HEADS),
                          lw, functools.partial(_attend_prompt, bsz=bp, s=sp, shift=lw["shift"]))
        attend_s = functools.partial(_attend_sample, bsz=bs, t=ts, pt=page_table + l * n_phys,
                                     ck=ck, cv=cv, slopes=slopes)
        xs, st_s = _layer(xs, ps_all, l, bs, ts,
                          state_rglru_conv[l], state_rglru_h[l], state_mlstm_C[l], state_mlstm_n[l],
                          state_mlstm_m[l], lw, attend_s)
        sp_all.append(st_p)
        ss_all.append(st_s)

    stk = lambda states, j: jnp.stack([s_[j] for s_ in states])
    return (xp.reshape(bp, sp, D_MODEL), xs.reshape(bs, ts, D_MODEL),
            stk(sp_all, 0), stk(sp_all, 1), stk(ss_all, 0), stk(ss_all, 1),
            stk(sp_all, 2), stk(ss_all, 2), stk(sp_all, 3), stk(ss_all, 3),
            stk(sp_all, 4), stk(ss_all, 4), stk(sp_all, 5), stk(ss_all, 5),
            stk(sp_all, 6), stk(ss_all, 6))
```

```python
import functools

import numpy as np
import jax
import jax.numpy as jnp
from jax import lax
from jax.experimental import pallas as pl
from jax.experimental.pallas import tpu as pltpu

F32 = jnp.float32
BF16 = jnp.bfloat16
HIGHEST = lax.Precision.HIGHEST

LANES = 128
SUBLANES = 8

HEAD_DIM = 64
D_MODEL = 1024
A_WIDTH = 256
CONV_WIDTH = 4
RG_C = 8.0
B_HEADS = 8
B_KV_HEADS = 4
C_HEADS = 4
MOBA_BLOCK = 256
MOBA_TOPK = 3
PAGE_SIZE = 128
MLSTM_CHUNK = 64
PLE_DIM = 256
NORM_EPS = 1e-6
NB_MAX = 32
NEG = -1e30

U_XA, U_GA = 0, 256
U_QB, U_GB, U_KB, U_VB = 512, 1024, 1536, 1792
U_CQ, U_CK, U_CV, U_CO, U_CG, U_CIF = 2048, 2304, 2560, 2816, 3072, 3328
D_U = 3456
_REF_SPLITS = (256, 256, 512, 256, 256, 512, 256, 256, 256, 4, 4, 256, 256)
_REF_OFFS = np.concatenate([[0], np.cumsum(_REF_SPLITS)]).tolist()

VMEM_LIMIT = 56 * 1024 * 1024


def _cparams(sem):
    return pltpu.CompilerParams(dimension_semantics=sem, vmem_limit_bytes=VMEM_LIMIT)


def _lane_iota(shape):
    return lax.broadcasted_iota(jnp.int32, shape, len(shape) - 1)


def _row_iota(shape):
    return lax.broadcasted_iota(jnp.int32, shape, len(shape) - 2)


def _sigmoid(x):
    return 1.0 / (1.0 + jnp.exp(-x))


def _silu(x):
    return x * _sigmoid(x)


def _softplus(x):
    return jnp.maximum(x, 0.0) + jnp.log1p(jnp.exp(-jnp.abs(x)))


def _dot_nt(a, b, **kw):
    return lax.dot_general(a, b, (((1,), (1,)), ((), ())), preferred_element_type=F32, **kw)


def _inproj_kernel(x_ref, g_ref, w_ref, qg_ref, kg_ref, bd_ref, u_ref, k_ref, v_ref, *t_refs):
    x = x_ref[...]
    xn = x * lax.rsqrt(jnp.mean(x * x, axis=-1, keepdims=True) + NORM_EPS) * g_ref[...]
    u = jnp.dot(xn.astype(BF16), w_ref[...], preferred_element_type=F32)
    u_ref[...] = u
    bd = bd_ref[...]
    q = u[:, U_QB:U_QB + 512]
    qss = jnp.dot((q * q).astype(BF16), bd, preferred_element_type=F32) * (1.0 / HEAD_DIM)
    qn = q * lax.rsqrt(qss + NORM_EPS) * qg_ref[...] * (HEAD_DIM ** -0.5)
    u_ref[:, U_QB:U_QB + 512] = qn
    k = u[:, U_KB:U_KB + 256]
    kss = jnp.dot((k * k).astype(BF16), bd[:256, :256], preferred_element_type=F32) * (1.0 / HEAD_DIM)
    kn = k * lax.rsqrt(kss + NORM_EPS) * kg_ref[...]
    u_ref[:, U_KB:U_KB + 256] = kn
    k_ref[...] = kn
    v = u[:, U_VB:U_VB + 256]
    v_ref[...] = v
    if t_refs:
        kt_ref, vt_ref = t_refs
        kt_ref[0] = kn.T
        vt_ref[0] = v.T


def _inproj(x2d, norm_g, w_u, qg, kg, bd, tm, bsz, t):
    m = x2d.shape[0]
    const = lambda i: (0, 0)
    transposed = t % tm == 0
    nt = t // tm if transposed else 1
    t_specs = [pl.BlockSpec((1, 256, tm), lambda i: (i // nt, 0, i % nt))] * 2 if transposed else []
    t_shapes = [jax.ShapeDtypeStruct((bsz, 256, t), F32)] * 2 if transposed else []
    return pl.pallas_call(
        _inproj_kernel,
        grid=(m // tm,),
        in_specs=[
            pl.BlockSpec((tm, D_MODEL), lambda i: (i, 0)),
            pl.BlockSpec((1, D_MODEL), const),
            pl.BlockSpec((D_MODEL, D_U), const),
            pl.BlockSpec((1, 512), const),
            pl.BlockSpec((1, 256), const),
            pl.BlockSpec((512, 512), const),
        ],
        out_specs=[
            pl.BlockSpec((tm, D_U), lambda i: (i, 0)),
            pl.BlockSpec((tm, 256), lambda i: (i, 0)),
            pl.BlockSpec((tm, 256), lambda i: (i, 0)),
        ] + t_specs,
        out_shape=[
            jax.ShapeDtypeStruct((m, D_U), F32),
            jax.ShapeDtypeStruct((m, 256), F32),
            jax.ShapeDtypeStruct((m, 256), F32),
        ] + t_shapes,
        compiler_params=_cparams(("parallel",)),
        name="inproj",
    )(x2d, norm_g, w_u, qg, kg, bd)


def _rglru_kernel(xa_ref, ga_ref, conv0_ref, h0_ref, cw_ref, cb_ref, wri_ref, bri_ref, lam_ref,
                  ya_ref, convn_ref, hl_ref, xbuf, abuf, bbuf, hbuf, hcar, *, tt):
    ti = pl.program_id(1)
    nt = pl.num_programs(1)

    @pl.when(ti == 0)
    def _():
        xbuf[5:8, :] = conv0_ref[0]
        hcar[...] = jnp.broadcast_to(h0_ref[0], (SUBLANES, A_WIDTH))

    xa = xa_ref[...]
    xbuf[8:8 + tt, :] = xa
    cw = cw_ref[...]
    xc = jnp.broadcast_to(cb_ref[...], (tt, A_WIDTH))
    for j in range(CONV_WIDTH):
        xc = xc + xbuf[5 + j:5 + j + tt, :] * cw[j:j + 1, :]
    xbuf[5:8, :] = xa[tt - 3:tt, :]

    gates = jnp.dot(xc.astype(BF16), wri_ref[...], preferred_element_type=F32) + bri_ref[...]
    r = _sigmoid(gates[:, :A_WIDTH])
    ig = _sigmoid(gates[:, A_WIDTH:])
    log_a = (-RG_C) * r * _softplus(-lam_ref[...])
    a = jnp.exp(log_a)
    abuf[...] = a
    bbuf[...] = jnp.sqrt(-jnp.tanh(log_a) * (a * a + 1.0)) * (ig * xc)

    rows = _row_iota((SUBLANES, A_WIDTH))

    def group(gi, carry):
        r0 = pl.multiple_of(gi * SUBLANES, SUBLANES)
        a = abuf[pl.ds(r0, SUBLANES), :]
        b = bbuf[pl.ds(r0, SUBLANES), :]
        for d in (1, 2, 4):
            a_sh = pltpu.roll(a, d, 0)
            b_sh = pltpu.roll(b, d, 0)
            keep = rows >= d
            b = jnp.where(keep, a * b_sh + b, b)
            a = jnp.where(keep, a * a_sh, a)
        h = a * hcar[...] + b
        hbuf[pl.ds(r0, SUBLANES), :] = h
        hcar[...] = jnp.broadcast_to(h[SUBLANES - 1:SUBLANES, :], (SUBLANES, A_WIDTH))
        return carry

    lax.fori_loop(0, tt // SUBLANES, group, 0)

    ya_ref[...] = hbuf[...] * _silu(ga_ref[...])

    @pl.when(ti == nt - 1)
    def _():
        convn_ref[0] = xa[tt - 3:tt, :]
        hl_ref[0] = hcar[0:1, :]


def _rglru(u, conv0, h0, cw, cb, wri, bri, lam, bsz, t, tt):
    nt = t // tt
    kern = functools.partial(_rglru_kernel, tt=tt)
    const = lambda b, i: (0, 0)
    return pl.pallas_call(
        kern,
        grid=(bsz, nt),
        in_specs=[
            pl.BlockSpec((tt, A_WIDTH), lambda b, i: (b * nt + i, U_XA // A_WIDTH)),
            pl.BlockSpec((tt, A_WIDTH), lambda b, i: (b * nt + i, U_GA // A_WIDTH)),
            pl.BlockSpec((1, 3, A_WIDTH), lambda b, i: (b, 0, 0)),
            pl.BlockSpec((1, 1, A_WIDTH), lambda b, i: (b, 0, 0)),
            pl.BlockSpec((CONV_WIDTH, A_WIDTH), const),
            pl.BlockSpec((1, A_WIDTH), const),
            pl.BlockSpec((A_WIDTH, 2 * A_WIDTH), const),
            pl.BlockSpec((1, 2 * A_WIDTH), const),
            pl.BlockSpec((1, A_WIDTH), const),
        ],
        out_specs=[
            pl.BlockSpec((tt, A_WIDTH), lambda b, i: (b * nt + i, 0)),
            pl.BlockSpec((1, 3, A_WIDTH), lambda b, i: (b, 0, 0)),
            pl.BlockSpec((1, 1, A_WIDTH), lambda b, i: (b, 0, 0)),
        ],
        out_shape=[
            jax.ShapeDtypeStruct((bsz * t, A_WIDTH), F32),
            jax.ShapeDtypeStruct((bsz, 3, A_WIDTH), F32),
            jax.ShapeDtypeStruct((bsz, 1, A_WIDTH), F32),
        ],
        scratch_shapes=[
            pltpu.VMEM((tt + 8, A_WIDTH), F32),
            pltpu.VMEM((tt, A_WIDTH), F32),
            pltpu.VMEM((tt, A_WIDTH), F32),
            pltpu.VMEM((tt, A_WIDTH), F32),
            pltpu.VMEM((SUBLANES, A_WIDTH), F32),
        ],
        compiler_params=_cparams(("parallel", "arbitrary")),
        name="rglru",
    )(u, u, conv0, h0, cw, cb, wri, bri, lam)


def _kmean_kernel(k_ref, o_ref, *, nblk):
    k = k_ref[...].reshape(nblk, MOBA_BLOCK, 256)
    o_ref[...] = jnp.sum(k, axis=1) * (1.0 / MOBA_BLOCK)


def _kmean(k2d, nblk):
    m = k2d.shape[0]
    rows = nblk * MOBA_BLOCK
    return pl.pallas_call(
        functools.partial(_kmean_kernel, nblk=nblk),
        grid=(m // rows,),
        in_specs=[pl.BlockSpec((rows, 256), lambda i: (i, 0))],
        out_specs=pl.BlockSpec((nblk, 256), lambda i: (i, 0)),
        out_shape=jax.ShapeDtypeStruct((m // MOBA_BLOCK, 256), F32),
        compiler_params=_cparams(("parallel",)),
        name="kmean",
    )(k2d)


def _select_kernel(q_ref, k_ref, v_ref, w_ref, shift_ref, qa_ref, ka_ref, va_ref, *, ts, rs):
    tile0 = pl.program_id(1) * ts
    w = w_ref[0]
    neg_shift = -shift_ref[...]

    def sub_tile(si, carry):
        r0 = pl.multiple_of(si * rs, rs)
        t0 = tile0 + r0
        q = q_ref[pl.ds(r0, rs), :]
        sc = _dot_nt(w, q, precision=HIGHEST)
        shp = (B_HEADS * NB_MAX, rs)
        j = _row_iota(shp) & (NB_MAX - 1)
        n_full = (t0 + _lane_iota(shp)) >> 8
        valid = j < n_full
        scm = jnp.where(valid, sc, -jnp.inf)
        def over_sublanes(op, v):
            for d in (4, 2, 1):
                v = op(v, pltpu.roll(v, d, 0))
            return v

        nv = NB_MAX // SUBLANES
        js = [j[SUBLANES * v:SUBLANES * (v + 1)] for v in range(nv)]
        nf = n_full[0:SUBLANES]
        picked = []
        for h in range(B_HEADS):
            xs = [scm[h * NB_MAX + SUBLANES * v:h * NB_MAX + SUBLANES * (v + 1)] for v in range(nv)]
            sel = [jnp.zeros((SUBLANES, rs), jnp.int32) for _ in range(nv)]
            for r in range(MOBA_TOPK):
                mx = over_sublanes(jnp.maximum, functools.reduce(jnp.maximum, xs))
                cand = [jnp.where(xs[v] == mx, js[v], NB_MAX) for v in range(nv)]
                first = over_sublanes(jnp.minimum, functools.reduce(jnp.minimum, cand))
                for v in range(nv):
                    hit = js[v] == first
                    sel[v] = jnp.where(hit & (nf > r), 1, sel[v])
                    xs[v] = jnp.where(hit, -jnp.inf, xs[v])
            picked.extend(sel)
        allowed = (jnp.concatenate(picked, axis=0) > 0) | (j == n_full)
        bias = jnp.where(allowed, 0.0, NEG).T

        shp1 = (rs, LANES)
        lane = _lane_iota(shp1)
        pos1 = t0 + _row_iota(shp1)
        pos_hi = (pos1 >> 6).astype(F32)
        pos_lo = (pos1 & 63).astype(F32)
        blk_onehot = jnp.where((lane - 64) == (pos1 >> 8), 1.0, 0.0)
        kfeat = jnp.where(lane == 96, pos_hi, jnp.where(lane == 97, pos_lo,
                          jnp.where((lane >= 98) & (lane <= 100), 1.0, 0.0)))
        kextra = jnp.where(lane < 96, blk_onehot, kfeat)
        vextra = jnp.where(lane == 64, 1.0, 0.0)
        k = k_ref[pl.ds(r0, rs), :]
        v = v_ref[pl.ds(r0, rs), :]
        for g in range(B_KV_HEADS):
            kt = k[:, (g // 2) * LANES:(g // 2 + 1) * LANES]
            vt = v[:, (g // 2) * LANES:(g // 2 + 1) * LANES]
            if g % 2:
                kt = pltpu.roll(kt, 64, 1)
                vt = pltpu.roll(vt, 64, 1)
            ka_ref[0, g, pl.ds(r0, rs), :] = jnp.where(lane < 64, kt, kextra).astype(BF16)
            va_ref[0, g, pl.ds(r0, rs), :] = jnp.where(lane < 64, vt, vextra).astype(BF16)
            qt = q[:, g * LANES:(g + 1) * LANES]
            for h2 in range(2):
                h = 2 * g + h2
                base = qt if h2 == 0 else pltpu.roll(qt, 64, 1)
                bt = bias[:, (h // 4) * LANES:(h // 4 + 1) * LANES]
                sh = (64 - (h % 4) * NB_MAX) % LANES
                if sh:
                    bt = pltpu.roll(bt, sh, 1)
                slope = 2.0 ** (-(h + 1))
                qfeat = jnp.where(lane == 96, slope * 64.0, jnp.where(lane == 97, slope,
                                  jnp.where(lane == 98, (-slope * 64.0) * pos_hi,
                                            jnp.where(lane == 99, (-slope) * pos_lo,
                                                      jnp.where(lane == 100, neg_shift, 0.0)))))
                aug = jnp.where(lane < 64, base, jnp.where(lane < 96, bt, qfeat))
                qa_ref[0, h, pl.ds(r0, rs), :] = aug.astype(BF16)
        return carry

    lax.fori_loop(0, ts // rs, sub_tile, 0)


def _select(u, k2d, v2d, wsel, shift, bsz, s, ts):
    nt = s // ts
    return pl.pallas_call(
        functools.partial(_select_kernel, ts=ts, rs=min(ts, LANES)),
        grid=(bsz, nt),
        in_specs=[
            pl.BlockSpec((ts, 512), lambda b, i: (b * nt + i, U_QB // 512)),
            pl.BlockSpec((ts, 256), lambda b, i: (b * nt + i, 0)),
            pl.BlockSpec((ts, 256), lambda b, i: (b * nt + i, 0)),
            pl.BlockSpec((1, B_HEADS * NB_MAX, 512), lambda b, i: (b, 0, 0)),
            pl.BlockSpec((1, LANES), lambda b, i: (0, 0)),
        ],
        out_specs=[
            pl.BlockSpec((1, B_HEADS, ts, LANES), lambda b, i: (b, 0, i, 0)),
            pl.BlockSpec((1, B_KV_HEADS, ts, LANES), lambda b, i: (b, 0, i, 0)),
            pl.BlockSpec((1, B_KV_HEADS, ts, LANES), lambda b, i: (b, 0, i, 0)),
        ],
        out_shape=[
            jax.ShapeDtypeStruct((bsz, B_HEADS, s, LANES), BF16),
            jax.ShapeDtypeStruct((bsz, B_KV_HEADS, s, LANES), BF16),
            jax.ShapeDtypeStruct((bsz, B_KV_HEADS, s, LANES), BF16),
        ],
        compiler_params=_cparams(("parallel", "parallel")),
        name="moba_select",
    )(u, k2d, v2d, wsel, shift)


ATTN_KV_BLOCKS = 4
ATTN_GROUPS = 4
SHIFT_LIMIT = 30.0


def _attn_finish(o_ref, acc, gi, tq):
    res = acc / acc[:, HEAD_DIM:HEAD_DIM + 1]
    lane = _lane_iota((tq, LANES))
    o_ref[:, gi * LANES:(gi + 1) * LANES] = jnp.where(lane < 64, res[:tq], pltpu.roll(res[tq:], 64, 1))


def _attn_online(qa_ref, ka_ref, va_ref, o_ref, m_sc, acc_sc, tq):
    qi = pl.program_id(2)
    rows = 2 * tq
    for gi in range(ATTN_GROUPS):
        q = qa_ref[0, 2 * gi:2 * gi + 2].reshape(rows, LANES)

        k0 = ka_ref[0, gi, pl.ds(pl.multiple_of(qi * tq, tq), tq), :]
        v0 = va_ref[0, gi, pl.ds(pl.multiple_of(qi * tq, tq), tq), :]
        s = _dot_nt(q, k0)
        shp = (rows, tq)
        qpos = _row_iota(shp) & (tq - 1)
        s = jnp.where(_lane_iota(shp) <= qpos, s, NEG)
        m0 = jnp.max(s, axis=-1, keepdims=True)
        p = jnp.exp(s - m0)
        m_sc[gi] = jnp.broadcast_to(m0, (rows, LANES))
        acc_sc[gi] = jnp.dot(p.astype(BF16), v0, preferred_element_type=F32)

        def body(jb, carry, gi=gi, q=q):
            r0 = pl.multiple_of(jb * tq, tq)
            kj = ka_ref[0, gi, pl.ds(r0, tq), :]
            vj = va_ref[0, gi, pl.ds(r0, tq), :]
            sj = _dot_nt(q, kj)
            m_prev = m_sc[gi]
            m_new = jnp.maximum(m_prev, jnp.max(sj, axis=-1, keepdims=True))
            alpha = jnp.exp(m_prev - m_new)
            pj = jnp.exp(sj - m_new[:, 0:1])
            acc_sc[gi] = alpha * acc_sc[gi] + jnp.dot(pj.astype(BF16), vj, preferred_element_type=F32)
            m_sc[gi] = m_new
            return carry

        lax.fori_loop(0, qi, body, 0)
        _attn_finish(o_ref, acc_sc[gi], gi, tq)


def _attn_shifted(qa_ref, ka_ref, va_ref, o_ref, acc_sc, tq):
    qi = pl.program_id(2)
    rows = 2 * tq
    qs = [qa_ref[0, 2 * gi:2 * gi + 2].reshape(rows, LANES) for gi in range(ATTN_GROUPS)]
    acc_sc[...] = jnp.zeros((ATTN_GROUPS, rows, LANES), F32)

    def step(r0, nblk, own_last):
        keys = nblk * tq
        if own_last:
            shp = (rows, keys)
            visible = _lane_iota(shp) - (keys - tq) <= (_row_iota(shp) & (tq - 1))
        for gi in range(ATTN_GROUPS):
            kj = ka_ref[0, gi, pl.ds(r0, keys), :]
            vj = va_ref[0, gi, pl.ds(r0, keys), :]
            s = _dot_nt(qs[gi], kj)
            p = jnp.exp(jnp.where(visible, s, NEG) if own_last else s)
            acc_sc[gi] += jnp.dot(p.astype(BF16), vj, preferred_element_type=F32)

    def body(jp, carry):
        step(pl.multiple_of(jp * (ATTN_KV_BLOCKS * tq), ATTN_KV_BLOCKS * tq), ATTN_KV_BLOCKS, False)
        return carry

    n_full = qi // ATTN_KV_BLOCKS
    lax.fori_loop(0, n_full, body, 0)
    tail0 = pl.multiple_of(n_full * (ATTN_KV_BLOCKS * tq), ATTN_KV_BLOCKS * tq)
    for nblk in range(1, ATTN_KV_BLOCKS + 1):
        @pl.when(qi - n_full * ATTN_KV_BLOCKS + 1 == nblk)
        def _():
            step(tail0, nblk, True)

    for gi in range(ATTN_GROUPS):
        _attn_finish(o_ref, acc_sc[gi], gi, tq)


def _attn_kernel(mode_ref, qa_ref, ka_ref, va_ref, o_ref, m_sc, acc_sc, *, tq):
    @pl.when(mode_ref[0] == 1)
    def _():
        _attn_shifted(qa_ref, ka_ref, va_ref, o_ref, acc_sc, tq)

    @pl.when(mode_ref[0] != 1)
    def _():
        _attn_online(qa_ref, ka_ref, va_ref, o_ref, m_sc, acc_sc, tq)


def _attn(mode, qa, ka, va, bsz, s):
    tq = MOBA_BLOCK
    nq = s // tq
    ng = ATTN_GROUPS
    assert nq % ATTN_KV_BLOCKS == 0 and B_KV_HEADS % ng == 0
    grid_spec = pltpu.PrefetchScalarGridSpec(
        num_scalar_prefetch=1,
        grid=(bsz, B_KV_HEADS // ng, nq),
        in_specs=[
            pl.BlockSpec((1, 2 * ng, tq, LANES), lambda b, g, i, mode: (b, g, i, 0)),
            pl.BlockSpec((1, ng, s, LANES), lambda b, g, i, mode: (b, g, 0, 0)),
            pl.BlockSpec((1, ng, s, LANES), lambda b, g, i, mode: (b, g, 0, 0)),
        ],
        out_specs=pl.BlockSpec((tq, ng * LANES), lambda b, g, i, mode: (b * nq + i, g)),
        scratch_shapes=[
            pltpu.VMEM((ng, 2 * tq, LANES), F32),
            pltpu.VMEM((ng, 2 * tq, LANES), F32),
        ],
    )
    return pl.pallas_call(
        functools.partial(_attn_kernel, tq=tq),
        grid_spec=grid_spec,
        out_shape=jax.ShapeDtypeStruct((bsz * s, 512), F32),
        compiler_params=_cparams(("parallel", "parallel", "arbitrary")),
        name="moba_attn",
    )(mode, qa, ka, va)


PAGES_PER_STEP = 16


def _page_copies(pt_ref, src_hbm, buf, sem, b, c, slot):
    return [pltpu.make_async_copy(src_hbm.at[pt_ref[b, c * PAGES_PER_STEP + p]], buf.at[slot, p], sem.at[slot])
            for p in range(PAGES_PER_STEP)]


def _sample_moba_kernel(pt_ref, q_ref, kn_ref, vn_ref, slope_ref, ck_hbm, cv_hbm, o_ref,
                        kst, vst, kres, kmt, ksem, vsem, *, nseq, nch, nblk, tnew, past):
    b = pl.program_id(0)
    cur = b % 2
    nxt = 1 - cur
    has_next = b + 1 < nseq
    rows = 64
    blocks = PAGES_PER_STEP * PAGE_SIZE // MOBA_BLOCK
    ppb = MOBA_BLOCK // PAGE_SIZE
    km_lane = _lane_iota((256, LANES)) & (nblk - 1)

    def k_copies(seq, c, st):
        return _page_copies(pt_ref, ck_hbm, kst, ksem, seq, c, st)

    def v_copies(c, st):
        return _page_copies(pt_ref, cv_hbm, vst, vsem, b, c, st)

    def stage_k_chunk(res_slot, c, st):
        kres[res_slot, pl.ds(c * PAGES_PER_STEP, PAGES_PER_STEP)] = kst[st].astype(BF16)
        tile = jnp.zeros((256, LANES), F32)
        for jj in range(blocks):
            x = kst[st, ppb * jj]
            for pp in range(1, ppb):
                x = x + kst[st, ppb * jj + pp]
            mean = jnp.sum(x, axis=-1, keepdims=True) * (1.0 / MOBA_BLOCK)
            tile = jnp.where(km_lane == c * blocks + jj, mean, tile)
        if c == 0:
            kmt[res_slot] = tile
        else:
            kmt[res_slot] = kmt[res_slot] + tile

    @pl.when(b == 0)
    def _():
        for cp in k_copies(0, 0, 0):
            cp.start()
        for c in range(nch):
            st = c % 2
            if c + 1 < nch:
                for cp in k_copies(0, c + 1, 1 - st):
                    cp.start()
            for cp in k_copies(0, c, st):
                cp.wait()
            stage_k_chunk(0, c, st)

    @pl.when(has_next)
    def _():
        for cp in k_copies(b + 1, 0, 0):
            cp.start()

    for cp in v_copies(0, 0):
        cp.start()

    qf = q_ref[0]
    qb = qf.astype(BF16)
    slope = slope_ref[...]
    shp = (rows, LANES)
    lane = _lane_iota(shp)
    tok = _row_iota(shp) & (tnew - 1)

    sc = jnp.dot(qf, kmt[cur], preferred_element_type=F32, precision=HIGHEST)
    j = lane & (nblk - 1)
    rank = jnp.zeros(shp, jnp.int32)
    for s in range(1, nblk):
        a = pltpu.roll(sc, 2 * nblk - s, 1)
        wrapped = j >= nblk - s
        rank = rank + jnp.where(wrapped, jnp.where(a >= sc, 1, 0), jnp.where(a > sc, 1, 0))
    bias = jnp.where(rank < MOBA_TOPK, 0.0, NEG)

    s_own = _dot_nt(qb, kn_ref[0].astype(BF16))
    s_own = s_own - slope * (tok - lane).astype(F32)
    s_own = jnp.where((lane <= tok) & (lane < tnew), s_own, NEG)
    m = jnp.max(s_own, axis=-1, keepdims=True)
    p = jnp.exp(s_own - m)
    l = jnp.sum(p, axis=-1, keepdims=True)
    acc = jnp.dot(p.astype(BF16), vn_ref[0].astype(BF16), preferred_element_type=F32)

    keys = PAGES_PER_STEP * PAGE_SIZE
    blocks = keys // MOBA_BLOCK
    kshape = (rows, keys)
    klane = _lane_iota(kshape)
    slope_col = slope[:, 0:1]
    alibi_in_chunk = slope_col * (klane - (_row_iota(kshape) & (tnew - 1))).astype(F32)
    for c in range(nch):
        st = c % 2
        if c + 1 < nch:
            for cp in v_copies(c + 1, 1 - st):
                cp.start()

            @pl.when(has_next)
            def _():
                for cp in k_copies(b + 1, c + 1, 1 - st):
                    cp.start()

        for cp in v_copies(c, st):
            cp.wait()
        vc = vst[st].astype(BF16)
        s_c = jnp.concatenate(
            [jnp.dot(qb, kres[cur, c * PAGES_PER_STEP + pg], preferred_element_type=F32)
             for pg in range(PAGES_PER_STEP)], axis=1)
        chunk_term = slope_col * float(c * keys - past)
        sel = jnp.concatenate(
            [jnp.broadcast_to(bias[:, c * blocks + jj:c * blocks + jj + 1] + chunk_term, (rows, MOBA_BLOCK))
             for jj in range(blocks)], axis=1)
        s_c = s_c + alibi_in_chunk + sel
        m_new = jnp.maximum(m, jnp.max(s_c, axis=-1, keepdims=True))
        alpha = jnp.exp(m - m_new)
        p = jnp.exp(s_c - m_new)
        l = alpha * l + jnp.sum(p, axis=-1, keepdims=True)
        pb = p.astype(BF16)
        pv = _dot_nt(pb[:, 0:PAGE_SIZE], vc[0])
        for pg in range(1, PAGES_PER_STEP):
            pv = pv + _dot_nt(pb[:, pg * PAGE_SIZE:(pg + 1) * PAGE_SIZE], vc[pg])
        acc = alpha * acc + pv
        m = m_new

        @pl.when(has_next)
        def _():
            for cp in k_copies(b + 1, c, st):
                cp.wait()
            stage_k_chunk(nxt, c, st)

    res = acc / l
    oshape = (rows, 256)
    own = (_lane_iota(oshape) >> 6) == (_row_iota(oshape) >> 4)
    res = jnp.where(own, res, 0.0)
    folded = res + pltpu.roll(res, 64, 1) + pltpu.roll(res, 128, 1) + pltpu.roll(res, 192, 1)
    o_ref[0] = folded[:, :HEAD_DIM]


def _sample_moba(pt, qbd, knew, vnew, slopes, ck, cv, bsz, n_pages, tnew):
    nch = n_pages // PAGES_PER_STEP
    nblk = n_pages * PAGE_SIZE // MOBA_BLOCK
    past = n_pages * PAGE_SIZE
    grid_spec = pltpu.PrefetchScalarGridSpec(
        num_scalar_prefetch=1,
        grid=(bsz,),
        in_specs=[
            pl.BlockSpec((1, 64, 256), lambda b, pt: (b, 0, 0)),
            pl.BlockSpec((1, LANES, 256), lambda b, pt: (b, 0, 0)),
            pl.BlockSpec((1, LANES, 256), lambda b, pt: (b, 0, 0)),
            pl.BlockSpec((64, LANES), lambda b, pt: (0, 0)),
            pl.BlockSpec(memory_space=pl.ANY),
            pl.BlockSpec(memory_space=pl.ANY),
        ],
        out_specs=pl.BlockSpec((1, 64, HEAD_DIM), lambda b, pt: (b, 0, 0)),
        scratch_shapes=[
            pltpu.VMEM((2, PAGES_PER_STEP, 256, PAGE_SIZE), F32),
            pltpu.VMEM((2, PAGES_PER_STEP, 256, PAGE_SIZE), F32),
            pltpu.VMEM((2, n_pages, 256, PAGE_SIZE), BF16),
            pltpu.VMEM((2, 256, LANES), F32),
            pltpu.SemaphoreType.DMA((2,)),
            pltpu.SemaphoreType.DMA((2,)),
        ],
    )
    return pl.pallas_call(
        functools.partial(_sample_moba_kernel, nseq=bsz, nch=nch, nblk=nblk, tnew=tnew, past=past),
        grid_spec=grid_spec,
        out_shape=jax.ShapeDtypeStruct((bsz, 64, HEAD_DIM), F32),
        compiler_params=_cparams(("arbitrary",)),
        name="sample_moba",
    )(pt, qbd, knew, vnew, slopes, ck, cv)


def _mlstm_kernel(q_ref, k_ref, v_ref, if_ref, o_ref, g_ref, bias_ref, og_ref, c0_ref, m0_ref,
                  y_ref, cout_ref, mout_ref, c_sc, m_sc, *, tc, L):
    ti = pl.program_id(1)
    nt = pl.num_programs(1)

    @pl.when(ti == 0)
    def _():
        c_sc[...] = c0_ref[0]
        m_sc[...] = m0_ref[0]

    gates = if_ref[...] + bias_ref[...]
    lf = jnp.minimum(gates, 0.0) - jnp.log1p(jnp.exp(-jnp.abs(gates)))
    tri = _lane_iota((L, L)) <= _row_iota((L, L))
    tri_f = jnp.where(tri, 1.0, 0.0)
    lane = _lane_iota((L, LANES))
    og = og_ref[...]
    for c in range(tc // L):
        r0 = c * L
        fcum = jnp.dot(tri_f, lf[r0:r0 + L], preferred_element_type=F32, precision=HIGHEST)
        z = gates[r0:r0 + L] - pltpu.roll(fcum, LANES - 4, 1)
        qt = q_ref[r0:r0 + L, :]
        kt = k_ref[r0:r0 + L, :] * (HEAD_DIM ** -0.5)
        vt = v_ref[r0:r0 + L, :]
        ot = _sigmoid(o_ref[r0:r0 + L, :])
        for pair in range(C_HEADS // 2):
            ytile = jnp.zeros((L, LANES), F32)
            for h2 in range(2):
                h = 2 * pair + h2
                sl = slice(pair * LANES, (pair + 1) * LANES)
                qh, kh, vh, oh = qt[:, sl], kt[:, sl], vt[:, sl], ot[:, sl]
                if h2:
                    qh, kh, vh, oh = [pltpu.roll(a, 64, 1) for a in (qh, kh, vh, oh)]
                qh = jnp.where(lane < 64, qh, 0.0).astype(BF16)
                kh = jnp.where(lane < 64, kh, 0.0).astype(BF16)
                vaug = jnp.where(lane < 64, vh, jnp.where(lane == 64, 1.0, 0.0))
                m_prev = m_sc[h:h + 1, 0:1]
                f_col = fcum[:, 4 + h:5 + h]
                z_col = z[:, h:h + 1]
                pick = jnp.where(lane == h, 1.0, 0.0)
                z_row = _dot_nt(pick, z, precision=HIGHEST)
                d_log = jnp.where(tri, f_col + z_row, -jnp.inf)
                inter = f_col + m_prev
                m_t = jnp.maximum(jnp.max(d_log, axis=-1, keepdims=True), inter)
                w_intra = _dot_nt(qh, kh) * jnp.exp(d_log - m_t)
                w_inter = jnp.exp(inter - m_t)
                c_prev = c_sc[h]
                num = (w_inter * jnp.dot(qh, c_prev.astype(BF16), preferred_element_type=F32)
                       + jnp.dot(w_intra.astype(BF16), vaug.astype(BF16), preferred_element_type=F32))
                den = num[:, HEAD_DIM:HEAD_DIM + 1]
                hh = num / jnp.maximum(jnp.abs(den), jnp.exp(-m_t))
                f_tot = f_col[L - 1:L, :]
                w_log = f_tot + z_col
                m_new = jnp.maximum(f_tot + m_prev, jnp.max(w_log, axis=0, keepdims=True))
                decay = jnp.exp(f_tot + m_prev - m_new)
                w = jnp.exp(w_log - m_new)
                upd = lax.dot_general(kh, (w * vaug).astype(BF16), (((0,), (0,)), ((), ())),
                                      preferred_element_type=F32)
                c_sc[h] = decay * c_prev + upd
                m_sc[h:h + 1, :] = jnp.broadcast_to(m_new, (1, LANES))
                ho = jnp.where(lane < 64, hh * oh, 0.0)
                ms = jnp.sum(ho * ho, axis=-1, keepdims=True) * (1.0 / HEAD_DIM)
                yh = ho * lax.rsqrt(ms + NORM_EPS)
                ytile = ytile + (pltpu.roll(yh, 64, 1) if h2 else yh)
            sl = slice(pair * LANES, (pair + 1) * LANES)
            y_ref[r0:r0 + L, sl] = ytile * og[:, sl] * _silu(g_ref[r0:r0 + L, sl])

    @pl.when(ti == nt - 1)
    def _():
        cout_ref[0] = c_sc[...]
        mout_ref[0] = m_sc[...]


def _mlstm(u, gate_bias, onorm_g, c0aug, m0, bsz, t, tc, L, wide):
    nt = t // tc
    const = lambda b, i: (0, 0)
    ucol = lambda off, w: (lambda b, i: (b * nt + i, off // w))
    return pl.pallas_call(
        functools.partial(_mlstm_wide_kernel if wide else _mlstm_kernel, tc=tc, L=L),
        grid=(bsz, nt),
        in_specs=[
            pl.BlockSpec((tc, 256), ucol(U_CQ, 256)),
            pl.BlockSpec((tc, 256), ucol(U_CK, 256)),
            pl.BlockSpec((tc, 256), ucol(U_CV, 256)),
            pl.BlockSpec((tc, LANES), ucol(U_CIF, LANES)),
            pl.BlockSpec((tc, 256), ucol(U_CO, 256)),
            pl.BlockSpec((tc, 256), ucol(U_CG, 256)),
            pl.BlockSpec((1, LANES), const),
            pl.BlockSpec((1, 256), const),
            pl.BlockSpec((1, C_HEADS, LANES, LANES), lambda b, i: (b, 0, 0, 0)),
            pl.BlockSpec((1, SUBLANES, LANES), lambda b, i: (b, 0, 0)),
        ],
        out_specs=[
            pl.BlockSpec((tc, 256), lambda b, i: (b * nt + i, 0)),
            pl.BlockSpec((1, C_HEADS, LANES, LANES), lambda b, i: (b, 0, 0, 0)),
            pl.BlockSpec((1, SUBLANES, LANES), lambda b, i: (b, 0, 0)),
        ],
        out_shape=[
            jax.ShapeDtypeStruct((bsz * t, 256), F32),
            jax.ShapeDtypeStruct((bsz, C_HEADS, LANES, LANES), F32),
            jax.ShapeDtypeStruct((bsz, SUBLANES, LANES), F32),
        ],
        scratch_shapes=[
            pltpu.VMEM((C_HEADS, LANES, LANES), F32),
            pltpu.VMEM((SUBLANES, LANES), F32),
        ],
        compiler_params=_cparams(("parallel", "arbitrary")),
        name="mlstm",
    )(u, u, u, u, u, u, gate_bias, onorm_g, c0aug, m0)


def _mlstm_wide_kernel(q_ref, k_ref, v_ref, if_ref, o_ref, g_ref, bias_ref, og_ref, c0_ref, m0_ref,
                       y_ref, cout_ref, mout_ref, c_sc, m_sc, *, tc, L):
    ti = pl.program_id(1)
    nt = pl.num_programs(1)

    @pl.when(ti == 0)
    def _():
        c_sc[...] = c0_ref[0]
        m_sc[...] = m0_ref[0]

    gates = if_ref[...] + bias_ref[...]
    lf = jnp.minimum(gates, 0.0) - jnp.log1p(jnp.exp(-jnp.abs(gates)))
    in_chunk = _row_iota((tc, LANES)) & (L - 1)
    fcum = lf
    d = 1
    while d < L:
        fcum = fcum + jnp.where(in_chunk >= d, pltpu.roll(fcum, d, 0), 0.0)
        d *= 2
    z = gates - pltpu.roll(fcum, LANES - 4, 1)
    z_t = z.T
    f_rep, z_rep, zmax_rep = [], [], []
    for h in range(C_HEADS):
        fh = jnp.broadcast_to(fcum[:, 4 + h:5 + h], (tc, LANES))
        zh = jnp.broadcast_to(z[:, h:h + 1], (tc, LANES))
        zm = zh
        d = 1
        while d < L:
            zm = jnp.maximum(zm, jnp.where(in_chunk >= d, pltpu.roll(zm, d, 0), -jnp.inf))
            d *= 2
        f_rep.append(fh)
        z_rep.append(zh)
        zmax_rep.append(zm)
    k_t = (k_ref[...] * (HEAD_DIM ** -0.5)).T

    tri = _lane_iota((L, L)) <= _row_iota((L, L))
    lane = _lane_iota((L, LANES))
    srow = _row_iota((LANES, L))
    og = og_ref[...]
    for c in range(tc // L):
        r0 = c * L
        for pair in range(C_HEADS // 2):
            sl = slice(pair * LANES, (pair + 1) * LANES)
            qt = q_ref[r0:r0 + L, sl]
            vt = v_ref[r0:r0 + L, sl]
            ot = _sigmoid(o_ref[r0:r0 + L, sl])
            ktt = k_t[pair * LANES:(pair + 1) * LANES, r0:r0 + L]
            ytile = jnp.zeros((L, LANES), F32)
            for h2 in range(2):
                h = 2 * pair + h2
                mine = (lane >= 64) if h2 else (lane < 64)
                ones_col = 0 if h2 else HEAD_DIM
                qh = jnp.where(mine, qt, 0.0).astype(BF16)
                kth = jnp.where((srow >= 64) if h2 else (srow < 64), ktt, 0.0).astype(BF16)
                vaug = jnp.where(mine, vt, jnp.where(lane == ones_col, 1.0, 0.0))
                m_prev = m_sc[h:h + 1, :]
                f_t = f_rep[h][r0:r0 + L]
                z_s = z_rep[h][r0:r0 + L]
                z_row = z_t[h:h + 1, r0:r0 + L]
                a_t = jnp.maximum(zmax_rep[h][r0:r0 + L], m_prev)
                decay_intra = jnp.exp(jnp.where(tri, z_row - a_t, -jnp.inf))
                w_intra = jnp.dot(qh, kth, preferred_element_type=F32) * decay_intra
                w_inter = jnp.exp(m_prev - a_t)
                c_prev = c_sc[h]
                num = (w_inter * jnp.dot(qh, c_prev.astype(BF16), preferred_element_type=F32)
                       + jnp.dot(w_intra.astype(BF16), vaug.astype(BF16), preferred_element_type=F32))
                den = num[:, ones_col:ones_col + 1]
                hh = num / jnp.maximum(jnp.abs(den), jnp.exp(-(f_t + a_t)))
                a_last = a_t[L - 1:L, :]
                decay = jnp.exp(m_prev - a_last)
                w = jnp.exp(z_s - a_last)
                c_sc[h] = decay * c_prev + jnp.dot(kth, (w * vaug).astype(BF16), preferred_element_type=F32)
                m_sc[h:h + 1, :] = f_t[L - 1:L, :] + a_last
                ho = jnp.where(mine, hh * ot, 0.0)
                ms = jnp.sum(ho * ho, axis=-1, keepdims=True) * (1.0 / HEAD_DIM)
                ytile = ytile + ho * lax.rsqrt(ms + NORM_EPS)
            y_ref[r0:r0 + L, sl] = ytile * og[:, sl] * _silu(g_ref[r0:r0 + L, sl])

    @pl.when(ti == nt - 1)
    def _():
        cout_ref[0] = c_sc[...]
        mout_ref[0] = m_sc[...]


def _outproj_kernel(x_ref, ya_ref, yb_ref, gb_ref, yc_ref, p_ref, wa_ref, wb_ref, wc_ref,
                    pg_ref, pp_ref, png_ref, o_ref):
    yb = yb_ref[...] * _silu(gb_ref[...])
    x = (x_ref[...]
         + jnp.dot(ya_ref[...].astype(BF16), wa_ref[...], preferred_element_type=F32)
         + jnp.dot(yb.astype(BF16), wb_ref[...], preferred_element_type=F32)
         + jnp.dot(yc_ref[...].astype(BF16), wc_ref[...], preferred_element_type=F32))
    gate = _sigmoid(jnp.dot(x.astype(BF16), pg_ref[...], preferred_element_type=F32))
    e = jnp.dot(p_ref[...].astype(BF16), pp_ref[...], preferred_element_type=F32)
    e = e * lax.rsqrt(jnp.mean(e * e, axis=-1, keepdims=True) + NORM_EPS) * png_ref[...]
    o_ref[...] = x + gate * e


def _outproj(x2d, ya, yb, u, yc, p_all, layer, wa, wb, wc, pg, pp, png, tm):
    m = x2d.shape[0]
    row = lambda i: (i, 0)
    p_row = lambda i: (layer * (m // tm) + i, 0)
    const = lambda i: (0, 0)
    return pl.pallas_call(
        _outproj_kernel,
        grid=(m // tm,),
        in_specs=[
            pl.BlockSpec((tm, D_MODEL), row),
            pl.BlockSpec((tm, 256), row),
            pl.BlockSpec((tm, 512), row),
            pl.BlockSpec((tm, 512), lambda i: (i, U_GB // 512)),
            pl.BlockSpec((tm, 256), row),
            pl.BlockSpec((tm, PLE_DIM), p_row),
            pl.BlockSpec((256, D_MODEL), const),
            pl.BlockSpec((512, D_MODEL), const),
            pl.BlockSpec((256, D_MODEL), const),
            pl.BlockSpec((D_MODEL, D_MODEL), const),
            pl.BlockSpec((PLE_DIM, D_MODEL), const),
            pl.BlockSpec((1, D_MODEL), const),
        ],
        out_specs=pl.BlockSpec((tm, D_MODEL), row),
        out_shape=jax.ShapeDtypeStruct((m, D_MODEL), F32),
        compiler_params=_cparams(("parallel",)),
        name="outproj",
    )(x2d, ya, yb, u, yc, p_all, wa, wb, wc, pg, pp, png)


def _block_diag(w):
    n, a, b = w.shape
    eye = jnp.eye(n, dtype=w.dtype)
    return jnp.einsum("nab,nm->namb", w, eye).reshape(n * a, n * b)


def _prep_layer(l, w):
    (norm_g, w_in, a_conv_w, a_conv_b, a_w_r, a_b_r, a_w_i, a_b_i, a_lambda, b_qnorm_g, b_knorm_g,
     c_b_i, c_b_f, c_onorm_g, w_out, ple_gate, ple_proj, ple_norm_g) = [a[l] for a in w]
    col = lambda i: w_in[:, _REF_OFFS[i]:_REF_OFFS[i + 1]]
    w_u = jnp.concatenate(
        [col(0), col(1), col(2), col(5), col(3), col(4), col(6), col(7), col(8), col(11), col(12),
         col(9), col(10), jnp.zeros((D_MODEL, LANES - 2 * C_HEADS), F32)], axis=1).astype(BF16)
    gate_bias = jnp.concatenate([c_b_i, c_b_f, jnp.zeros((LANES - 2 * C_HEADS,), F32)])[None, :]
    shift = 1.02 * HEAD_DIM ** 0.5 * jnp.max(jnp.abs(b_qnorm_g)) * jnp.max(jnp.abs(b_knorm_g)) + 0.5
    return dict(
        shift=shift,
        norm_g=norm_g[None, :], w_u=w_u,
        qg=jnp.tile(b_qnorm_g, B_HEADS)[None, :], kg=jnp.tile(b_knorm_g, B_KV_HEADS)[None, :],
        bd=jnp.asarray(np.kron(np.eye(B_HEADS), np.ones((HEAD_DIM, HEAD_DIM))), BF16),
        cw=a_conv_w, cb=a_conv_b[None, :],
        wri=jnp.concatenate([_block_diag(a_w_r), _block_diag(a_w_i)], axis=1).astype(BF16),
        bri=jnp.concatenate([a_b_r, a_b_i])[None, :], lam=a_lambda[None, :],
        gate_bias=gate_bias, onorm_g=c_onorm_g[None, :],
        wa=w_out[:256].astype(BF16), wb=w_out[256:768].astype(BF16), wc=w_out[768:].astype(BF16),
        pg=ple_gate.astype(BF16), pp=ple_proj.astype(BF16), png=ple_norm_g[None, :],
    )


def _pick_tile(n, cap):
    t = min(n, cap)
    while n % t:
        t //= 2
    return t


def _state_tiles(c, n, upper):
    pad_cols = jnp.zeros(c.shape[:3] + (HEAD_DIM - 1,), F32)
    rows = jnp.concatenate([n[..., None], pad_cols, c] if upper else [c, n[..., None], pad_cols], axis=-1)
    blank = jnp.zeros_like(rows)
    return jnp.concatenate([blank, rows] if upper else [rows, blank], axis=-2)


def _layer(x2d, p_all, layer, bsz, t, conv0, h0, c0, n0, m0, lw, attend):
    m = bsz * t
    tm = _pick_tile(m, 512)
    proj = _inproj(x2d, lw["norm_g"], lw["w_u"], lw["qg"], lw["kg"], lw["bd"], tm, bsz, t)
    u, k2d, v2d = proj[:3]
    if len(proj) > 3:
        k_state, v_state = [a.reshape(bsz, B_KV_HEADS, HEAD_DIM, t).transpose(0, 3, 1, 2) for a in proj[3:]]
    else:
        k_state, v_state = [a.reshape(bsz, t, B_KV_HEADS, HEAD_DIM) for a in (k2d, v2d)]
    ya, conv_new, h_last = _rglru(u, conv0, h0[:, None, :], lw["cw"], lw["cb"], lw["wri"], lw["bri"],
                                  lw["lam"], bsz, t, _pick_tile(t, 512))
    yb = attend(u, k2d, v2d)
    wide = t % LANES == 0
    L = LANES if wide else int(np.gcd(t, MLSTM_CHUNK))
    lo, hi = slice(0, HEAD_DIM), slice(HEAD_DIM, 2 * HEAD_DIM)
    if wide:
        even = _state_tiles(c0[:, 0::2], n0[:, 0::2], upper=False)
        odd = _state_tiles(c0[:, 1::2], n0[:, 1::2], upper=True)
        c0aug = jnp.stack([even, odd], axis=2).reshape(bsz, C_HEADS, LANES, LANES)
    else:
        c0aug = _state_tiles(c0, n0, upper=False)
    m0b = jnp.concatenate([jnp.broadcast_to(m0[:, :, None], (bsz, C_HEADS, LANES)),
                           jnp.zeros((bsz, SUBLANES - C_HEADS, LANES), F32)], axis=1)
    yc, caug, mout = _mlstm(u, lw["gate_bias"], lw["onorm_g"], c0aug, m0b, bsz, t, _pick_tile(t, 512), L, wide)
    if wide:
        c_new = jnp.stack([caug[:, 0, lo, lo], caug[:, 1, hi, hi], caug[:, 2, lo, lo], caug[:, 3, hi, hi]], axis=1)
        n_new = jnp.stack([caug[:, 0, lo, HEAD_DIM], caug[:, 1, hi, 0],
                           caug[:, 2, lo, HEAD_DIM], caug[:, 3, hi, 0]], axis=1)
    else:
        c_new, n_new = caug[:, :, lo, lo], caug[:, :, lo, HEAD_DIM]
    xo = _outproj(x2d, ya, yb, u, yc, p_all, layer, lw["wa"], lw["wb"], lw["wc"], lw["pg"], lw["pp"], lw["png"], tm)
    state = (k_state, v_state, conv_new, h_last[:, 0, :], c_new, n_new, mout[:, :C_HEADS, 0])
    return xo, state


def _attend_prompt(u, k2d, v2d, bsz, s, shift):
    nb = s // MOBA_BLOCK
    assert s % MOBA_BLOCK == 0 and nb <= NB_MAX
    kmean = _kmean(k2d, _pick_tile(bsz * nb, 8)).reshape(bsz, nb, B_KV_HEADS, HEAD_DIM)
    kmh = jnp.repeat(kmean, B_HEADS // B_KV_HEADS, axis=2).transpose(0, 2, 1, 3)
    kmh = jnp.pad(kmh, ((0, 0), (0, 0), (0, NB_MAX - nb), (0, 0)))
    wsel = jnp.einsum("bhjd,hg->bhjgd", kmh, jnp.eye(B_HEADS, dtype=F32)).reshape(bsz, B_HEADS * NB_MAX, 512)
    qa, ka, va = _select(u, k2d, v2d, wsel, jnp.full((1, LANES), shift, F32), bsz, s, _pick_tile(s, 512))
    mode = (shift <= SHIFT_LIMIT).astype(jnp.int32).reshape(1)
    return _attn(mode, qa, ka, va, bsz, s)


def _attend_sample(u, k2d, v2d, bsz, t, pt, ck, cv, slopes):
    n_pages = pt.shape[1]
    assert t == 8 and n_pages % PAGES_PER_STEP == 0
    nblk = n_pages * PAGE_SIZE // MOBA_BLOCK
    assert nblk == 64
    q = u[:, U_QB:U_QB + 512].reshape(bsz, t, B_KV_HEADS, 2, HEAD_DIM).transpose(0, 2, 3, 1, 4)
    qbd = jnp.einsum("bghtd,gk->bghtkd", q, jnp.eye(B_KV_HEADS, dtype=F32)).reshape(bsz, 64, 256)
    pad = ((0, 0), (0, LANES - t), (0, 0))
    knew = jnp.pad(k2d.reshape(bsz, t, 256), pad)
    vnew = jnp.pad(v2d.reshape(bsz, t, 256), pad)
    o = _sample_moba(pt, qbd, knew, vnew, slopes, ck, cv, bsz, n_pages, t)
    return o.reshape(bsz, B_HEADS, t, HEAD_DIM).transpose(0, 2, 1, 3).reshape(bsz * t, 512)


def kernel(x_prompt, x_sample, cache_k, cache_v, state_rglru_conv, state_rglru_h, state_mlstm_C, state_mlstm_n, state_mlstm_m, page_table, p_prompt, p_sample, norm_g, w_in, a_conv_w, a_conv_b, a_w_r, a_b_r, a_w_i, a_b_i, a_lambda, b_qnorm_g, b_knorm_g, c_b_i, c_b_f, c_onorm_g, w_out, ple_gate, ple_proj, ple_norm_g):
    weights = (norm_g, w_in, a_conv_w, a_conv_b, a_w_r, a_b_r, a_w_i, a_b_i, a_lambda, b_qnorm_g, b_knorm_g,
               c_b_i, c_b_f, c_onorm_g, w_out, ple_gate, ple_proj, ple_norm_g)
    depth = w_in.shape[0]
    bp, sp = x_prompt.shape[:2]
    bs, ts = x_sample.shape[:2]
    n_phys = cache_k.shape[1]
    ck = cache_k.transpose(0, 1, 3, 4, 2).reshape(depth * n_phys, 256, PAGE_SIZE)
    cv = cache_v.transpose(0, 1, 3, 4, 2).reshape(depth * n_phys, 256, PAGE_SIZE)
    slopes = jnp.asarray(np.repeat(2.0 ** -(np.arange(1, B_HEADS + 1)), ts)[:, None]
                         * np.ones((1, LANES)), F32)

    xp = x_prompt.reshape(bp * sp, D_MODEL)
    pp_all = p_prompt.reshape(depth * bp * sp, PLE_DIM)
    ps_all = p_sample.reshape(depth * bs * ts, PLE_DIM)
    xs = x_sample.reshape(bs * ts, D_MODEL)
    zeros = lambda *s: jnp.zeros(s, F32)
    sp_all, ss_all = [], []
    for l in range(depth):
        lw = _prep_layer(l, weights)
        xp, st_p = _layer(xp, pp_all, l, bp, sp,
                          zeros(bp, CONV_WIDTH - 1, A_WIDTH), zeros(bp, A_WIDTH),
                          zeros(bp, C_HEADS, HEAD_DIM, HEAD_DIM), zeros(bp, C_HEADS, HEAD_DIM), zeros(bp, C_HEADS),
                          lw, functools.partial(_attend_prompt, bsz=bp, s=sp, shift=lw["shift"]))
        attend_s = functools.partial(_attend_sample, bsz=bs, t=ts, pt=page_table + l * n_phys,
                                     ck=ck, cv=cv, slopes=slopes)
        xs, st_s = _layer(xs, ps_all, l, bs, ts,
                          state_rglru_conv[l], state_rglru_h[l], state_mlstm_C[l], state_mlstm_n[l],
                          state_mlstm_m[l], lw, attend_s)
        sp_all.append(st_p)
        ss_all.append(st_s)

    stk = lambda states, j: jnp.stack([s_[j] for s_ in states])
    return (xp.reshape(bp, sp, D_MODEL), xs.reshape(bs, ts, D_MODEL),
            stk(sp_all, 0), stk(sp_all, 1), stk(ss_all, 0), stk(ss_all, 1),
            stk(sp_all, 2), stk(ss_all, 2), stk(sp_all, 3), stk(ss_all, 3),
            stk(sp_all, 4), stk(ss_all, 4), stk(sp_all, 5), stk(ss_all, 5),
            stk(sp_all, 6), stk(ss_all, 6))
```

```python
import functools

import numpy as np
import jax
import jax.numpy as jnp
from jax import lax
from jax.experimental import pallas as pl
from jax.experimental.pallas import tpu as pltpu

F32 = jnp.float32
BF16 = jnp.bfloat16
HIGHEST = lax.Precision.HIGHEST

LANES = 128
SUBLANES = 8

HEAD_DIM = 64
D_MODEL = 1024
A_WIDTH = 256
CONV_WIDTH = 4
RG_C = 8.0
B_HEADS = 8
B_KV_HEADS = 4
C_HEADS = 4
MOBA_BLOCK = 256
MOBA_TOPK = 3
PAGE_SIZE = 128
MLSTM_CHUNK = 64
PLE_DIM = 256
NORM_EPS = 1e-6
NB_MAX = 32
NEG = -1e30

U_XA, U_GA = 0, 256
U_QB, U_GB, U_KB, U_VB = 512, 1024, 1536, 1792
U_CQ, U_CK, U_CV, U_CO, U_CG, U_CIF = 2048, 2304, 2560, 2816, 3072, 3328
D_U = 3456
_REF_SPLITS = (256, 256, 512, 256, 256, 512, 256, 256, 256, 4, 4, 256, 256)
_REF_OFFS = np.concatenate([[0], np.cumsum(_REF_SPLITS)]).tolist()

VMEM_LIMIT = 56 * 1024 * 1024


def _cparams(sem):
    return pltpu.CompilerParams(dimension_semantics=sem, vmem_limit_bytes=VMEM_LIMIT)


def _lane_iota(shape):
    return lax.broadcasted_iota(jnp.int32, shape, len(shape) - 1)


def _row_iota(shape):
    return lax.broadcasted_iota(jnp.int32, shape, len(shape) - 2)


def _sigmoid(x):
    return 1.0 / (1.0 + jnp.exp(-x))


def _silu(x):
    return x * _sigmoid(x)


def _softplus(x):
    return jnp.maximum(x, 0.0) + jnp.log1p(jnp.exp(-jnp.abs(x)))


def _dot_nt(a, b, **kw):
    return lax.dot_general(a, b, (((1,), (1,)), ((), ())), preferred_element_type=F32, **kw)


def _inproj_kernel(x_ref, g_ref, w_ref, qg_ref, kg_ref, bd_ref, u_ref, k_ref, v_ref, *t_refs):
    x = x_ref[...]
    xn = x * lax.rsqrt(jnp.mean(x * x, axis=-1, keepdims=True) + NORM_EPS) * g_ref[...]
    u = jnp.dot(xn.astype(BF16), w_ref[...], preferred_element_type=F32)
    u_ref[...] = u
    bd = bd_ref[...]
    q = u[:, U_QB:U_QB + 512]
    qss = jnp.dot((q * q).astype(BF16), bd, preferred_element_type=F32) * (1.0 / HEAD_DIM)
    qn = q * lax.rsqrt(qss + NORM_EPS) * qg_ref[...] * (HEAD_DIM ** -0.5)
    u_ref[:, U_QB:U_QB + 512] = qn
    k = u[:, U_KB:U_KB + 256]
    kss = jnp.dot((k * k).astype(BF16), bd[:256, :256], preferred_element_type=F32) * (1.0 / HEAD_DIM)
    kn = k * lax.rsqrt(kss + NORM_EPS) * kg_ref[...]
    u_ref[:, U_KB:U_KB + 256] = kn
    k_ref[...] = kn
    v = u[:, U_VB:U_VB + 256]
    v_ref[...] = v
    if t_refs:
        kt_ref, vt_ref = t_refs
        kt_ref[0] = kn.T
        vt_ref[0] = v.T


def _inproj(x2d, norm_g, w_u, qg, kg, bd, tm, bsz, t):
    m = x2d.shape[0]
    const = lambda i: (0, 0)
    transposed = t % tm == 0
    nt = t // tm if transposed else 1
    t_specs = [pl.BlockSpec((1, 256, tm), lambda i: (i // nt, 0, i % nt))] * 2 if transposed else []
    t_shapes = [jax.ShapeDtypeStruct((bsz, 256, t), F32)] * 2 if transposed else []
    return pl.pallas_call(
        _inproj_kernel,
        grid=(m // tm,),
        in_specs=[
            pl.BlockSpec((tm, D_MODEL), lambda i: (i, 0)),
            pl.BlockSpec((1, D_MODEL), const),
            pl.BlockSpec((D_MODEL, D_U), const),
            pl.BlockSpec((1, 512), const),
            pl.BlockSpec((1, 256), const),
            pl.BlockSpec((512, 512), const),
        ],
        out_specs=[
            pl.BlockSpec((tm, D_U), lambda i: (i, 0)),
            pl.BlockSpec((tm, 256), lambda i: (i, 0)),
            pl.BlockSpec((tm, 256), lambda i: (i, 0)),
        ] + t_specs,
        out_shape=[
            jax.ShapeDtypeStruct((m, D_U), F32),
            jax.ShapeDtypeStruct((m, 256), F32),
            jax.ShapeDtypeStruct((m, 256), F32),
        ] + t_shapes,
        compiler_params=_cparams(("parallel",)),
        name="inproj",
    )(x2d, norm_g, w_u, qg, kg, bd)


def _rglru_kernel(xa_ref, ga_ref, conv0_ref, h0_ref, cw_ref, cb_ref, wri_ref, bri_ref, lam_ref,
                  ya_ref, convn_ref, hl_ref, xbuf, abuf, bbuf, hbuf, hcar, *, tt):
    ti = pl.program_id(1)
    nt = pl.num_programs(1)

    @pl.when(ti == 0)
    def _():
        xbuf[5:8, :] = conv0_ref[0]
        hcar[...] = jnp.broadcast_to(h0_ref[0], (SUBLANES, A_WIDTH))

    xa = xa_ref[...]
    xbuf[8:8 + tt, :] = xa
    cw = cw_ref[...]
    xc = jnp.broadcast_to(cb_ref[...], (tt, A_WIDTH))
    for j in range(CONV_WIDTH):
        xc = xc + xbuf[5 + j:5 + j + tt, :] * cw[j:j + 1, :]
    xbuf[5:8, :] = xa[tt - 3:tt, :]

    gates = jnp.dot(xc.astype(BF16), wri_ref[...], preferred_element_type=F32) + bri_ref[...]
    r = _sigmoid(gates[:, :A_WIDTH])
    ig = _sigmoid(gates[:, A_WIDTH:])
    log_a = (-RG_C) * r * _softplus(-lam_ref[...])
    a = jnp.exp(log_a)
    abuf[...] = a
    bbuf[...] = jnp.sqrt(-jnp.tanh(log_a) * (a * a + 1.0)) * (ig * xc)

    rows = _row_iota((SUBLANES, A_WIDTH))

    def group(gi, carry):
        r0 = pl.multiple_of(gi * SUBLANES, SUBLANES)
        a = abuf[pl.ds(r0, SUBLANES), :]
        b = bbuf[pl.ds(r0, SUBLANES), :]
        for d in (1, 2, 4):
            a_sh = pltpu.roll(a, d, 0)
            b_sh = pltpu.roll(b, d, 0)
            keep = rows >= d
            b = jnp.where(keep, a * b_sh + b, b)
            a = jnp.where(keep, a * a_sh, a)
        h = a * hcar[...] + b
        hbuf[pl.ds(r0, SUBLANES), :] = h
        hcar[...] = jnp.broadcast_to(h[SUBLANES - 1:SUBLANES, :], (SUBLANES, A_WIDTH))
        return carry

    lax.fori_loop(0, tt // SUBLANES, group, 0)

    ya_ref[...] = hbuf[...] * _silu(ga_ref[...])

    @pl.when(ti == nt - 1)
    def _():
        convn_ref[0] = xa[tt - 3:tt, :]
        hl_ref[0] = hcar[0:1, :]


def _rglru(u, conv0, h0, cw, cb, wri, bri, lam, bsz, t, tt):
    nt = t // tt
    kern = functools.partial(_rglru_kernel, tt=tt)
    const = lambda b, i: (0, 0)
    return pl.pallas_call(
        kern,
        grid=(bsz, nt),
        in_specs=[
            pl.BlockSpec((tt, A_WIDTH), lambda b, i: (b * nt + i, U_XA // A_WIDTH)),
            pl.BlockSpec((tt, A_WIDTH), lambda b, i: (b * nt + i, U_GA // A_WIDTH)),
            pl.BlockSpec((1, 3, A_WIDTH), lambda b, i: (b, 0, 0)),
            pl.BlockSpec((1, 1, A_WIDTH), lambda b, i: (b, 0, 0)),
            pl.BlockSpec((CONV_WIDTH, A_WIDTH), const),
            pl.BlockSpec((1, A_WIDTH), const),
            pl.BlockSpec((A_WIDTH, 2 * A_WIDTH), const),
            pl.BlockSpec((1, 2 * A_WIDTH), const),
            pl.BlockSpec((1, A_WIDTH), const),
        ],
        out_specs=[
            pl.BlockSpec((tt, A_WIDTH), lambda b, i: (b * nt + i, 0)),
            pl.BlockSpec((1, 3, A_WIDTH), lambda b, i: (b, 0, 0)),
            pl.BlockSpec((1, 1, A_WIDTH), lambda b, i: (b, 0, 0)),
        ],
        out_shape=[
            jax.ShapeDtypeStruct((bsz * t, A_WIDTH), F32),
            jax.ShapeDtypeStruct((bsz, 3, A_WIDTH), F32),
            jax.ShapeDtypeStruct((bsz, 1, A_WIDTH), F32),
        ],
        scratch_shapes=[
            pltpu.VMEM((tt + 8, A_WIDTH), F32),
            pltpu.VMEM((tt, A_WIDTH), F32),
            pltpu.VMEM((tt, A_WIDTH), F32),
            pltpu.VMEM((tt, A_WIDTH), F32),
            pltpu.VMEM((SUBLANES, A_WIDTH), F32),
        ],
        compiler_params=_cparams(("parallel", "arbitrary")),
        name="rglru",
    )(u, u, conv0, h0, cw, cb, wri, bri, lam)


def _kmean_kernel(k_ref, o_ref, *, nblk):
    k = k_ref[...].reshape(nblk, MOBA_BLOCK, 256)
    o_ref[...] = jnp.sum(k, axis=1) * (1.0 / MOBA_BLOCK)


def _kmean(k2d, nblk):
    m = k2d.shape[0]
    rows = nblk * MOBA_BLOCK
    return pl.pallas_call(
        functools.partial(_kmean_kernel, nblk=nblk),
        grid=(m // rows,),
        in_specs=[pl.BlockSpec((rows, 256), lambda i: (i, 0))],
        out_specs=pl.BlockSpec((nblk, 256), lambda i: (i, 0)),
        out_shape=jax.ShapeDtypeStruct((m // MOBA_BLOCK, 256), F32),
        compiler_params=_cparams(("parallel",)),
        name="kmean",
    )(k2d)


def _select_kernel(q_ref, k_ref, v_ref, w_ref, shift_ref, qa_ref, ka_ref, va_ref, *, ts, rs):
    tile0 = pl.program_id(1) * ts
    w = w_ref[0]
    neg_shift = -shift_ref[...]

    def sub_tile(si, carry):
        r0 = pl.multiple_of(si * rs, rs)
        t0 = tile0 + r0
        q = q_ref[pl.ds(r0, rs), :]
        sc = _dot_nt(w, q, precision=HIGHEST)
        shp = (B_HEADS * NB_MAX, rs)
        j = _row_iota(shp) & (NB_MAX - 1)
        n_full = (t0 + _lane_iota(shp)) >> 8
        valid = j < n_full
        scm = jnp.where(valid, sc, -jnp.inf)
        def over_sublanes(op, v):
            for d in (4, 2, 1):
                v = op(v, pltpu.roll(v, d, 0))
            return v

        nv = NB_MAX // SUBLANES
        js = [j[SUBLANES * v:SUBLANES * (v + 1)] for v in range(nv)]
        nf = n_full[0:SUBLANES]
        picked = []
        for h in range(B_HEADS):
            xs = [scm[h * NB_MAX + SUBLANES * v:h * NB_MAX + SUBLANES * (v + 1)] for v in range(nv)]
            sel = [jnp.zeros((SUBLANES, rs), jnp.int32) for _ in range(nv)]
            for r in range(MOBA_TOPK):
                mx = over_sublanes(jnp.maximum, functools.reduce(jnp.maximum, xs))
                cand = [jnp.where(xs[v] == mx, js[v], NB_MAX) for v in range(nv)]
                first = over_sublanes(jnp.minimum, functools.reduce(jnp.minimum, cand))
                for v in range(nv):
                    hit = js[v] == first
                    sel[v] = jnp.where(hit & (nf > r), 1, sel[v])
                    xs[v] = jnp.where(hit, -jnp.inf, xs[v])
            picked.extend(sel)
        allowed = (jnp.concatenate(picked, axis=0) > 0) | (j == n_full)
        bias = jnp.where(allowed, 0.0, NEG).T

        shp1 = (rs, LANES)
        lane = _lane_iota(shp1)
        pos1 = t0 + _row_iota(shp1)
        pos_hi = (pos1 >> 6).astype(F32)
        pos_lo = (pos1 & 63).astype(F32)
        blk_onehot = jnp.where((lane - 64) == (pos1 >> 8), 1.0, 0.0)
        kfeat = jnp.where(lane == 96, pos_hi, jnp.where(lane == 97, pos_lo,
                          jnp.where((lane >= 98) & (lane <= 100), 1.0, 0.0)))
        kextra = jnp.where(lane < 96, blk_onehot, kfeat)
        vextra = jnp.where(lane == 64, 1.0, 0.0)
        k = k_ref[pl.ds(r0, rs), :]
        v = v_ref[pl.ds(r0, rs), :]
        for g in range(B_KV_HEADS):
            kt = k[:, (g // 2) * LANES:(g // 2 + 1) * LANES]
            vt = v[:, (g // 2) * LANES:(g // 2 + 1) * LANES]
            if g % 2:
                kt = pltpu.roll(kt, 64, 1)
                vt = pltpu.roll(vt, 64, 1)
            ka_ref[0, g, pl.ds(r0, rs), :] = jnp.where(lane < 64, kt, kextra).astype(BF16)
            va_ref[0, g, pl.ds(r0, rs), :] = jnp.where(lane < 64, vt, vextra).astype(BF16)
            qt = q[:, g * LANES:(g + 1) * LANES]
            for h2 in range(2):
                h = 2 * g + h2
                base = qt if h2 == 0 else pltpu.roll(qt, 64, 1)
                bt = bias[:, (h // 4) * LANES:(h // 4 + 1) * LANES]
                sh = (64 - (h % 4) * NB_MAX) % LANES
                if sh:
                    bt = pltpu.roll(bt, sh, 1)
                slope = 2.0 ** (-(h + 1))
                qfeat = jnp.where(lane == 96, slope * 64.0, jnp.where(lane == 97, slope,
                                  jnp.where(lane == 98, (-slope * 64.0) * pos_hi,
                                            jnp.where(lane == 99, (-slope) * pos_lo,
                                                      jnp.where(lane == 100, neg_shift, 0.0)))))
                aug = jnp.where(lane < 64, base, jnp.where(lane < 96, bt, qfeat))
                qa_ref[0, h, pl.ds(r0, rs), :] = aug.astype(BF16)
        return carry

    lax.fori_loop(0, ts // rs, sub_tile, 0)


def _select(u, k2d, v2d, wsel, shift, bsz, s, ts):
    nt = s // ts
    return pl.pallas_call(
        functools.partial(_select_kernel, ts=ts, rs=min(ts, LANES)),
        grid=(bsz, nt),
        in_specs=[
            pl.BlockSpec((ts, 512), lambda b, i: (b * nt + i, U_QB // 512)),
            pl.BlockSpec((ts, 256), lambda b, i: (b * nt + i, 0)),
            pl.BlockSpec((ts, 256), lambda b, i: (b * nt + i, 0)),
            pl.BlockSpec((1, B_HEADS * NB_MAX, 512), lambda b, i: (b, 0, 0)),
            pl.BlockSpec((1, LANES), lambda b, i: (0, 0)),
        ],
        out_specs=[
            pl.BlockSpec((1, B_HEADS, ts, LANES), lambda b, i: (b, 0, i, 0)),
            pl.BlockSpec((1, B_KV_HEADS, ts, LANES), lambda b, i: (b, 0, i, 0)),
            pl.BlockSpec((1, B_KV_HEADS, ts, LANES), lambda b, i: (b, 0, i, 0)),
        ],
        out_shape=[
            jax.ShapeDtypeStruct((bsz, B_HEADS, s, LANES), BF16),
            jax.ShapeDtypeStruct((bsz, B_KV_HEADS, s, LANES), BF16),
            jax.ShapeDtypeStruct((bsz, B_KV_HEADS, s, LANES), BF16),
        ],
        compiler_params=_cparams(("parallel", "parallel")),
        name="moba_select",
    )(u, k2d, v2d, wsel, shift)


ATTN_KV_BLOCKS = 4
ATTN_GROUPS = 4
SHIFT_LIMIT = 30.0


def _attn_finish(o_ref, acc, gi, tq):
    res = acc / acc[:, HEAD_DIM:HEAD_DIM + 1]
    lane = _lane_iota((tq, LANES))
    o_ref[:, gi * LANES:(gi + 1) * LANES] = jnp.where(lane < 64, res[:tq], pltpu.roll(res[tq:], 64, 1))


def _attn_online(qa_ref, ka_ref, va_ref, o_ref, m_sc, acc_sc, tq):
    qi = pl.program_id(2)
    rows = 2 * tq
    for gi in range(ATTN_GROUPS):
        q = qa_ref[0, 2 * gi:2 * gi + 2].reshape(rows, LANES)

        k0 = ka_ref[0, gi, pl.ds(pl.multiple_of(qi * tq, tq), tq), :]
        v0 = va_ref[0, gi, pl.ds(pl.multiple_of(qi * tq, tq), tq), :]
        s = _dot_nt(q, k0)
        shp = (rows, tq)
        qpos = _row_iota(shp) & (tq - 1)
        s = jnp.where(_lane_iota(shp) <= qpos, s, NEG)
        m0 = jnp.max(s, axis=-1, keepdims=True)
        p = jnp.exp(s - m0)
        m_sc[gi] = jnp.broadcast_to(m0, (rows, LANES))
        acc_sc[gi] = jnp.dot(p.astype(BF16), v0, preferred_element_type=F32)

        def body(jb, carry, gi=gi, q=q):
            r0 = pl.multiple_of(jb * tq, tq)
            kj = ka_ref[0, gi, pl.ds(r0, tq), :]
            vj = va_ref[0, gi, pl.ds(r0, tq), :]
            sj = _dot_nt(q, kj)
            m_prev = m_sc[gi]
            m_new = jnp.maximum(m_prev, jnp.max(sj, axis=-1, keepdims=True))
            alpha = jnp.exp(m_prev - m_new)
            pj = jnp.exp(sj - m_new[:, 0:1])
            acc_sc[gi] = alpha * acc_sc[gi] + jnp.dot(pj.astype(BF16), vj, preferred_element_type=F32)
            m_sc[gi] = m_new
            return carry

        lax.fori_loop(0, qi, body, 0)
        _attn_finish(o_ref, acc_sc[gi], gi, tq)


def _attn_shifted(qa_ref, ka_ref, va_ref, o_ref, acc_sc, tq):
    qi = pl.program_id(2)
    rows = 2 * tq
    qs = [qa_ref[0, 2 * gi:2 * gi + 2].reshape(rows, LANES) for gi in range(ATTN_GROUPS)]
    acc_sc[...] = jnp.zeros((ATTN_GROUPS, rows, LANES), F32)

    def step(r0, nblk, own_last):
        keys = nblk * tq
        if own_last:
            shp = (rows, keys)
            visible = _lane_iota(shp) - (keys - tq) <= (_row_iota(shp) & (tq - 1))
        for gi in range(ATTN_GROUPS):
            kj = ka_ref[0, gi, pl.ds(r0, keys), :]
            vj = va_ref[0, gi, pl.ds(r0, keys), :]
            s = _dot_nt(qs[gi], kj)
            p = jnp.exp(jnp.where(visible, s, NEG) if own_last else s)
            acc_sc[gi] += jnp.dot(p.astype(BF16), vj, preferred_element_type=F32)

    def body(jp, carry):
        step(pl.multiple_of(jp * (ATTN_KV_BLOCKS * tq), ATTN_KV_BLOCKS * tq), ATTN_KV_BLOCKS, False)
        return carry

    n_full = qi // ATTN_KV_BLOCKS
    lax.fori_loop(0, n_full, body, 0)
    tail0 = pl.multiple_of(n_full * (ATTN_KV_BLOCKS * tq), ATTN_KV_BLOCKS * tq)
    for nblk in range(1, ATTN_KV_BLOCKS + 1):
        @pl.when(qi - n_full * ATTN_KV_BLOCKS + 1 == nblk)
        def _():
            step(tail0, nblk, True)

    for gi in range(ATTN_GROUPS):
        _attn_finish(o_ref, acc_sc[gi], gi, tq)


def _attn_kernel(mode_ref, qa_ref, ka_ref, va_ref, o_ref, m_sc, acc_sc, *, tq):
    @pl.when(mode_ref[0] == 1)
    def _():
        _attn_shifted(qa_ref, ka_ref, va_ref, o_ref, acc_sc, tq)

    @pl.when(mode_ref[0] != 1)
    def _():
        _attn_online(qa_ref, ka_ref, va_ref, o_ref, m_sc, acc_sc, tq)


def _attn(mode, qa, ka, va, bsz, s):
    tq = MOBA_BLOCK
    nq = s // tq
    ng = ATTN_GROUPS
    assert nq % ATTN_KV_BLOCKS == 0 and B_KV_HEADS % ng == 0
    grid_spec = pltpu.PrefetchScalarGridSpec(
        num_scalar_prefetch=1,
        grid=(bsz, B_KV_HEADS // ng, nq),
        in_specs=[
            pl.BlockSpec((1, 2 * ng, tq, LANES), lambda b, g, i, mode: (b, g, i, 0)),
            pl.BlockSpec((1, ng, s, LANES), lambda b, g, i, mode: (b, g, 0, 0)),
            pl.BlockSpec((1, ng, s, LANES), lambda b, g, i, mode: (b, g, 0, 0)),
        ],
        out_specs=pl.BlockSpec((tq, ng * LANES), lambda b, g, i, mode: (b * nq + i, g)),
        scratch_shapes=[
            pltpu.VMEM((ng, 2 * tq, LANES), F32),
            pltpu.VMEM((ng, 2 * tq, LANES), F32),
        ],
    )
    return pl.pallas_call(
        functools.partial(_attn_kernel, tq=tq),
        grid_spec=grid_spec,
        out_shape=jax.ShapeDtypeStruct((bsz * s, 512), F32),
        compiler_params=_cparams(("parallel", "parallel", "arbitrary")),
        name="moba_attn",
    )(mode, qa, ka, va)


PAGES_PER_STEP = 32


def _page_copies(pt_ref, src_hbm, buf, sem, b, c, slot):
    return [pltpu.make_async_copy(src_hbm.at[pt_ref[b, c * PAGES_PER_STEP + p]], buf.at[slot, p], sem.at[slot])
            for p in range(PAGES_PER_STEP)]


def _sample_moba_kernel(pt_ref, q_ref, kn_ref, vn_ref, slope_ref, ck_hbm, cv_hbm, o_ref,
                        kst, vst, kres, kmt, ksem, vsem, *, nseq, nch, nblk, tnew, past):
    b = pl.program_id(0)
    cur = b % 2
    nxt = 1 - cur
    has_next = b + 1 < nseq
    rows = 64
    blocks = PAGES_PER_STEP * PAGE_SIZE // MOBA_BLOCK
    ppb = MOBA_BLOCK // PAGE_SIZE
    km_lane = _lane_iota((256, LANES)) & (nblk - 1)

    def k_copies(seq, c, st):
        return _page_copies(pt_ref, ck_hbm, kst, ksem, seq, c, st)

    def v_copies(c, st):
        return _page_copies(pt_ref, cv_hbm, vst, vsem, b, c, st)

    def stage_k_chunk(res_slot, c, st):
        kres[res_slot, pl.ds(c * PAGES_PER_STEP, PAGES_PER_STEP)] = kst[st].astype(BF16)
        tile = jnp.zeros((256, LANES), F32)
        for jj in range(blocks):
            x = kst[st, ppb * jj]
            for pp in range(1, ppb):
                x = x + kst[st, ppb * jj + pp]
            mean = jnp.sum(x, axis=-1, keepdims=True) * (1.0 / MOBA_BLOCK)
            tile = jnp.where(km_lane == c * blocks + jj, mean, tile)
        if c == 0:
            kmt[res_slot] = tile
        else:
            kmt[res_slot] = kmt[res_slot] + tile

    @pl.when(b == 0)
    def _():
        for cp in k_copies(0, 0, 0):
            cp.start()
        for c in range(nch):
            st = c % 2
            if c + 1 < nch:
                for cp in k_copies(0, c + 1, 1 - st):
                    cp.start()
            for cp in k_copies(0, c, st):
                cp.wait()
            stage_k_chunk(0, c, st)

    @pl.when(has_next)
    def _():
        for cp in k_copies(b + 1, 0, 0):
            cp.start()

    for cp in v_copies(0, 0):
        cp.start()

    qf = q_ref[0]
    qb = qf.astype(BF16)
    slope = slope_ref[...]
    shp = (rows, LANES)
    lane = _lane_iota(shp)
    tok = _row_iota(shp) & (tnew - 1)

    sc = jnp.dot(qf, kmt[cur], preferred_element_type=F32, precision=HIGHEST)
    j = lane & (nblk - 1)
    rank = jnp.zeros(shp, jnp.int32)
    for s in range(1, nblk):
        a = pltpu.roll(sc, 2 * nblk - s, 1)
        wrapped = j >= nblk - s
        rank = rank + jnp.where(wrapped, jnp.where(a >= sc, 1, 0), jnp.where(a > sc, 1, 0))
    bias = jnp.where(rank < MOBA_TOPK, 0.0, NEG)

    s_own = _dot_nt(qb, kn_ref[0].astype(BF16))
    s_own = s_own - slope * (tok - lane).astype(F32)
    s_own = jnp.where((lane <= tok) & (lane < tnew), s_own, NEG)
    m = jnp.max(s_own, axis=-1, keepdims=True)
    p = jnp.exp(s_own - m)
    l = jnp.sum(p, axis=-1, keepdims=True)
    acc = jnp.dot(p.astype(BF16), vn_ref[0].astype(BF16), preferred_element_type=F32)

    keys = PAGES_PER_STEP * PAGE_SIZE
    blocks = keys // MOBA_BLOCK
    kshape = (rows, keys)
    klane = _lane_iota(kshape)
    slope_col = slope[:, 0:1]
    alibi_in_chunk = slope_col * (klane - (_row_iota(kshape) & (tnew - 1))).astype(F32)
    for c in range(nch):
        st = c % 2
        if c + 1 < nch:
            for cp in v_copies(c + 1, 1 - st):
                cp.start()

            @pl.when(has_next)
            def _():
                for cp in k_copies(b + 1, c + 1, 1 - st):
                    cp.start()

        for cp in v_copies(c, st):
            cp.wait()
        vc = vst[st].astype(BF16)
        s_c = jnp.concatenate(
            [jnp.dot(qb, kres[cur, c * PAGES_PER_STEP + pg], preferred_element_type=F32)
             for pg in range(PAGES_PER_STEP)], axis=1)
        chunk_term = slope_col * float(c * keys - past)
        sel = jnp.concatenate(
            [jnp.broadcast_to(bias[:, c * blocks + jj:c * blocks + jj + 1] + chunk_term, (rows, MOBA_BLOCK))
             for jj in range(blocks)], axis=1)
        s_c = s_c + alibi_in_chunk + sel
        m_new = jnp.maximum(m, jnp.max(s_c, axis=-1, keepdims=True))
        alpha = jnp.exp(m - m_new)
        p = jnp.exp(s_c - m_new)
        l = alpha * l + jnp.sum(p, axis=-1, keepdims=True)
        pb = p.astype(BF16)
        pv = _dot_nt(pb[:, 0:PAGE_SIZE], vc[0])
        for pg in range(1, PAGES_PER_STEP):
            pv = pv + _dot_nt(pb[:, pg * PAGE_SIZE:(pg + 1) * PAGE_SIZE], vc[pg])
        acc = alpha * acc + pv
        m = m_new

        @pl.when(has_next)
        def _():
            for cp in k_copies(b + 1, c, st):
                cp.wait()
            stage_k_chunk(nxt, c, st)

    res = acc / l
    oshape = (rows, 256)
    own = (_lane_iota(oshape) >> 6) == (_row_iota(oshape) >> 4)
    res = jnp.where(own, res, 0.0)
    folded = res + pltpu.roll(res, 64, 1) + pltpu.roll(res, 128, 1) + pltpu.roll(res, 192, 1)
    o_ref[0] = folded[:, :HEAD_DIM]


def _sample_moba(pt, qbd, knew, vnew, slopes, ck, cv, bsz, n_pages, tnew):
    nch = n_pages // PAGES_PER_STEP
    nblk = n_pages * PAGE_SIZE // MOBA_BLOCK
    past = n_pages * PAGE_SIZE
    grid_spec = pltpu.PrefetchScalarGridSpec(
        num_scalar_prefetch=1,
        grid=(bsz,),
        in_specs=[
            pl.BlockSpec((1, 64, 256), lambda b, pt: (b, 0, 0)),
            pl.BlockSpec((1, LANES, 256), lambda b, pt: (b, 0, 0)),
            pl.BlockSpec((1, LANES, 256), lambda b, pt: (b, 0, 0)),
            pl.BlockSpec((64, LANES), lambda b, pt: (0, 0)),
            pl.BlockSpec(memory_space=pl.ANY),
            pl.BlockSpec(memory_space=pl.ANY),
        ],
        out_specs=pl.BlockSpec((1, 64, HEAD_DIM), lambda b, pt: (b, 0, 0)),
        scratch_shapes=[
            pltpu.VMEM((2, PAGES_PER_STEP, 256, PAGE_SIZE), F32),
            pltpu.VMEM((2, PAGES_PER_STEP, 256, PAGE_SIZE), F32),
            pltpu.VMEM((2, n_pages, 256, PAGE_SIZE), BF16),
            pltpu.VMEM((2, 256, LANES), F32),
            pltpu.SemaphoreType.DMA((2,)),
            pltpu.SemaphoreType.DMA((2,)),
        ],
    )
    return pl.pallas_call(
        functools.partial(_sample_moba_kernel, nseq=bsz, nch=nch, nblk=nblk, tnew=tnew, past=past),
        grid_spec=grid_spec,
        out_shape=jax.ShapeDtypeStruct((bsz, 64, HEAD_DIM), F32),
        compiler_params=_cparams(("arbitrary",)),
        name="sample_moba",
    )(pt, qbd, knew, vnew, slopes, ck, cv)


def _mlstm_kernel(q_ref, k_ref, v_ref, if_ref, o_ref, g_ref, bias_ref, og_ref, c0_ref, m0_ref,
                  y_ref, cout_ref, mout_ref, c_sc, m_sc, *, tc, L):
    ti = pl.program_id(1)
    nt = pl.num_programs(1)

    @pl.when(ti == 0)
    def _():
        c_sc[...] = c0_ref[0]
        m_sc[...] = m0_ref[0]

    gates = if_ref[...] + bias_ref[...]
    lf = jnp.minimum(gates, 0.0) - jnp.log1p(jnp.exp(-jnp.abs(gates)))
    tri = _lane_iota((L, L)) <= _row_iota((L, L))
    tri_f = jnp.where(tri, 1.0, 0.0)
    lane = _lane_iota((L, LANES))
    og = og_ref[...]
    for c in range(tc // L):
        r0 = c * L
        fcum = jnp.dot(tri_f, lf[r0:r0 + L], preferred_element_type=F32, precision=HIGHEST)
        z = gates[r0:r0 + L] - pltpu.roll(fcum, LANES - 4, 1)
        qt = q_ref[r0:r0 + L, :]
        kt = k_ref[r0:r0 + L, :] * (HEAD_DIM ** -0.5)
        vt = v_ref[r0:r0 + L, :]
        ot = _sigmoid(o_ref[r0:r0 + L, :])
        for pair in range(C_HEADS // 2):
            ytile = jnp.zeros((L, LANES), F32)
            for h2 in range(2):
                h = 2 * pair + h2
                sl = slice(pair * LANES, (pair + 1) * LANES)
                qh, kh, vh, oh = qt[:, sl], kt[:, sl], vt[:, sl], ot[:, sl]
                if h2:
                    qh, kh, vh, oh = [pltpu.roll(a, 64, 1) for a in (qh, kh, vh, oh)]
                qh = jnp.where(lane < 64, qh, 0.0).astype(BF16)
                kh = jnp.where(lane < 64, kh, 0.0).astype(BF16)
                vaug = jnp.where(lane < 64, vh, jnp.where(lane == 64, 1.0, 0.0))
                m_prev = m_sc[h:h + 1, 0:1]
                f_col = fcum[:, 4 + h:5 + h]
                z_col = z[:, h:h + 1]
                pick = jnp.where(lane == h, 1.0, 0.0)
                z_row = _dot_nt(pick, z, precision=HIGHEST)
                d_log = jnp.where(tri, f_col + z_row, -jnp.inf)
                inter = f_col + m_prev
                m_t = jnp.maximum(jnp.max(d_log, axis=-1, keepdims=True), inter)
                w_intra = _dot_nt(qh, kh) * jnp.exp(d_log - m_t)
                w_inter = jnp.exp(inter - m_t)
                c_prev = c_sc[h]
                num = (w_inter * jnp.dot(qh, c_prev.astype(BF16), preferred_element_type=F32)
                       + jnp.dot(w_intra.astype(BF16), vaug.astype(BF16), preferred_element_type=F32))
                den = num[:, HEAD_DIM:HEAD_DIM + 1]
                hh = num / jnp.maximum(jnp.abs(den), jnp.exp(-m_t))
                f_tot = f_col[L - 1:L, :]
                w_log = f_tot + z_col
                m_new = jnp.maximum(f_tot + m_prev, jnp.max(w_log, axis=0, keepdims=True))
                decay = jnp.exp(f_tot + m_prev - m_new)
                w = jnp.exp(w_log - m_new)
                upd = lax.dot_general(kh, (w * vaug).astype(BF16), (((0,), (0,)), ((), ())),
                                      preferred_element_type=F32)
                c_sc[h] = decay * c_prev + upd
                m_sc[h:h + 1, :] = jnp.broadcast_to(m_new, (1, LANES))
                ho = jnp.where(lane < 64, hh * oh, 0.0)
                ms = jnp.sum(ho * ho, axis=-1, keepdims=True) * (1.0 / HEAD_DIM)
                yh = ho * lax.rsqrt(ms + NORM_EPS)
                ytile = ytile + (pltpu.roll(yh, 64, 1) if h2 else yh)
            sl = slice(pair * LANES, (pair + 1) * LANES)
            y_ref[r0:r0 + L, sl] = ytile * og[:, sl] * _silu(g_ref[r0:r0 + L, sl])

    @pl.when(ti == nt - 1)
    def _():
        cout_ref[0] = c_sc[...]
        mout_ref[0] = m_sc[...]


def _mlstm(u, gate_bias, onorm_g, c0aug, m0, bsz, t, tc, L, wide):
    nt = t // tc
    const = lambda b, i: (0, 0)
    ucol = lambda off, w: (lambda b, i: (b * nt + i, off // w))
    return pl.pallas_call(
        functools.partial(_mlstm_wide_kernel if wide else _mlstm_kernel, tc=tc, L=L),
        grid=(bsz, nt),
        in_specs=[
            pl.BlockSpec((tc, 256), ucol(U_CQ, 256)),
            pl.BlockSpec((tc, 256), ucol(U_CK, 256)),
            pl.BlockSpec((tc, 256), ucol(U_CV, 256)),
            pl.BlockSpec((tc, LANES), ucol(U_CIF, LANES)),
            pl.BlockSpec((tc, 256), ucol(U_CO, 256)),
            pl.BlockSpec((tc, 256), ucol(U_CG, 256)),
            pl.BlockSpec((1, LANES), const),
            pl.BlockSpec((1, 256), const),
            pl.BlockSpec((1, C_HEADS, LANES, LANES), lambda b, i: (b, 0, 0, 0)),
            pl.BlockSpec((1, SUBLANES, LANES), lambda b, i: (b, 0, 0)),
        ],
        out_specs=[
            pl.BlockSpec((tc, 256), lambda b, i: (b * nt + i, 0)),
            pl.BlockSpec((1, C_HEADS, LANES, LANES), lambda b, i: (b, 0, 0, 0)),
            pl.BlockSpec((1, SUBLANES, LANES), lambda b, i: (b, 0, 0)),
        ],
        out_shape=[
            jax.ShapeDtypeStruct((bsz * t, 256), F32),
            jax.ShapeDtypeStruct((bsz, C_HEADS, LANES, LANES), F32),
            jax.ShapeDtypeStruct((bsz, SUBLANES, LANES), F32),
        ],
        scratch_shapes=[
            pltpu.VMEM((C_HEADS, LANES, LANES), F32),
            pltpu.VMEM((SUBLANES, LANES), F32),
        ],
        compiler_params=_cparams(("parallel", "arbitrary")),
        name="mlstm",
    )(u, u, u, u, u, u, gate_bias, onorm_g, c0aug, m0)


def _mlstm_wide_kernel(q_ref, k_ref, v_ref, if_ref, o_ref, g_ref, bias_ref, og_ref, c0_ref, m0_ref,
                       y_ref, cout_ref, mout_ref, c_sc, m_sc, *, tc, L):
    ti = pl.program_id(1)
    nt = pl.num_programs(1)

    @pl.when(ti == 0)
    def _():
        c_sc[...] = c0_ref[0]
        m_sc[...] = m0_ref[0]

    gates = if_ref[...] + bias_ref[...]
    lf = jnp.minimum(gates, 0.0) - jnp.log1p(jnp.exp(-jnp.abs(gates)))
    in_chunk = _row_iota((tc, LANES)) & (L - 1)
    fcum = lf
    d = 1
    while d < L:
        fcum = fcum + jnp.where(in_chunk >= d, pltpu.roll(fcum, d, 0), 0.0)
        d *= 2
    z = gates - pltpu.roll(fcum, LANES - 4, 1)
    z_t = z.T
    f_rep, z_rep, zmax_rep = [], [], []
    for h in range(C_HEADS):
        fh = jnp.broadcast_to(fcum[:, 4 + h:5 + h], (tc, LANES))
        zh = jnp.broadcast_to(z[:, h:h + 1], (tc, LANES))
        zm = zh
        d = 1
        while d < L:
            zm = jnp.maximum(zm, jnp.where(in_chunk >= d, pltpu.roll(zm, d, 0), -jnp.inf))
            d *= 2
        f_rep.append(fh)
        z_rep.append(zh)
        zmax_rep.append(zm)
    k_t = (k_ref[...] * (HEAD_DIM ** -0.5)).T

    tri = _lane_iota((L, L)) <= _row_iota((L, L))
    lane = _lane_iota((L, LANES))
    srow = _row_iota((LANES, L))
    og = og_ref[...]
    for c in range(tc // L):
        r0 = c * L
        for pair in range(C_HEADS // 2):
            sl = slice(pair * LANES, (pair + 1) * LANES)
            qt = q_ref[r0:r0 + L, sl]
            vt = v_ref[r0:r0 + L, sl]
            ot = _sigmoid(o_ref[r0:r0 + L, sl])
            ktt = k_t[pair * LANES:(pair + 1) * LANES, r0:r0 + L]
            ytile = jnp.zeros((L, LANES), F32)
            for h2 in range(2):
                h = 2 * pair + h2
                mine = (lane >= 64) if h2 else (lane < 64)
                ones_col = 0 if h2 else HEAD_DIM
                qh = jnp.where(mine, qt, 0.0).astype(BF16)
                kth = jnp.where((srow >= 64) if h2 else (srow < 64), ktt, 0.0).astype(BF16)
                vaug = jnp.where(mine, vt, jnp.where(lane == ones_col, 1.0, 0.0))
                m_prev = m_sc[h:h + 1, :]
                f_t = f_rep[h][r0:r0 + L]
                z_s = z_rep[h][r0:r0 + L]
                z_row = z_t[h:h + 1, r0:r0 + L]
                a_t = jnp.maximum(zmax_rep[h][r0:r0 + L], m_prev)
                decay_intra = jnp.exp(jnp.where(tri, z_row - a_t, -jnp.inf))
                w_intra = jnp.dot(qh, kth, preferred_element_type=F32) * decay_intra
                w_inter = jnp.exp(m_prev - a_t)
                c_prev = c_sc[h]
                num = (w_inter * jnp.dot(qh, c_prev.astype(BF16), preferred_element_type=F32)
                       + jnp.dot(w_intra.astype(BF16), vaug.astype(BF16), preferred_element_type=F32))
                den = num[:, ones_col:ones_col + 1]
                hh = num / jnp.maximum(jnp.abs(den), jnp.exp(-(f_t + a_t)))
                a_last = a_t[L - 1:L, :]
                decay = jnp.exp(m_prev - a_last)
                w = jnp.exp(z_s - a_last)
                c_sc[h] = decay * c_prev + jnp.dot(kth, (w * vaug).astype(BF16), preferred_element_type=F32)
                m_sc[h:h + 1, :] = f_t[L - 1:L, :] + a_last
                ho = jnp.where(mine, hh * ot, 0.0)
                ms = jnp.sum(ho * ho, axis=-1, keepdims=True) * (1.0 / HEAD_DIM)
                ytile = ytile + ho * lax.rsqrt(ms + NORM_EPS)
            y_ref[r0:r0 + L, sl] = ytile * og[:, sl] * _silu(g_ref[r0:r0 + L, sl])

    @pl.when(ti == nt - 1)
    def _():
        cout_ref[0] = c_sc[...]
        mout_ref[0] = m_sc[...]


def _outproj_kernel(x_ref, ya_ref, yb_ref, gb_ref, yc_ref, p_ref, wa_ref, wb_ref, wc_ref,
                    pg_ref, pp_ref, png_ref, o_ref):
    yb = yb_ref[...] * _silu(gb_ref[...])
    x = (x_ref[...]
         + jnp.dot(ya_ref[...].astype(BF16), wa_ref[...], preferred_element_type=F32)
         + jnp.dot(yb.astype(BF16), wb_ref[...], preferred_element_type=F32)
         + jnp.dot(yc_ref[...].astype(BF16), wc_ref[...], preferred_element_type=F32))
    gate = _sigmoid(jnp.dot(x.astype(BF16), pg_ref[...], preferred_element_type=F32))
    e = jnp.dot(p_ref[...].astype(BF16), pp_ref[...], preferred_element_type=F32)
    e = e * lax.rsqrt(jnp.mean(e * e, axis=-1, keepdims=True) + NORM_EPS) * png_ref[...]
    o_ref[...] = x + gate * e


def _outproj(x2d, ya, yb, u, yc, p_all, layer, wa, wb, wc, pg, pp, png, tm):
    m = x2d.shape[0]
    row = lambda i: (i, 0)
    p_row = lambda i: (layer * (m // tm) + i, 0)
    const = lambda i: (0, 0)
    return pl.pallas_call(
        _outproj_kernel,
        grid=(m // tm,),
        in_specs=[
            pl.BlockSpec((tm, D_MODEL), row),
            pl.BlockSpec((tm, 256), row),
            pl.BlockSpec((tm, 512), row),
            pl.BlockSpec((tm, 512), lambda i: (i, U_GB // 512)),
            pl.BlockSpec((tm, 256), row),
            pl.BlockSpec((tm, PLE_DIM), p_row),
            pl.BlockSpec((256, D_MODEL), const),
            pl.BlockSpec((512, D_MODEL), const),
            pl.BlockSpec((256, D_MODEL), const),
            pl.BlockSpec((D_MODEL, D_MODEL), const),
            pl.BlockSpec((PLE_DIM, D_MODEL), const),
            pl.BlockSpec((1, D_MODEL), const),
        ],
        out_specs=pl.BlockSpec((tm, D_MODEL), row),
        out_shape=jax.ShapeDtypeStruct((m, D_MODEL), F32),
        compiler_params=_cparams(("parallel",)),
        name="outproj",
    )(x2d, ya, yb, u, yc, p_all, wa, wb, wc, pg, pp, png)


def _block_diag(w):
    n, a, b = w.shape
    eye = jnp.eye(n, dtype=w.dtype)
    return jnp.einsum("nab,nm->namb", w, eye).reshape(n * a, n * b)


def _prep_layer(l, w):
    (norm_g, w_in, a_conv_w, a_conv_b, a_w_r, a_b_r, a_w_i, a_b_i, a_lambda, b_qnorm_g, b_knorm_g,
     c_b_i, c_b_f, c_onorm_g, w_out, ple_gate, ple_proj, ple_norm_g) = [a[l] for a in w]
    col = lambda i: w_in[:, _REF_OFFS[i]:_REF_OFFS[i + 1]]
    w_u = jnp.concatenate(
        [col(0), col(1), col(2), col(5), col(3), col(4), col(6), col(7), col(8), col(11), col(12),
         col(9), col(10), jnp.zeros((D_MODEL, LANES - 2 * C_HEADS), F32)], axis=1).astype(BF16)
    gate_bias = jnp.concatenate([c_b_i, c_b_f, jnp.zeros((LANES - 2 * C_HEADS,), F32)])[None, :]
    shift = 1.02 * HEAD_DIM ** 0.5 * jnp.max(jnp.abs(b_qnorm_g)) * jnp.max(jnp.abs(b_knorm_g)) + 0.5
    return dict(
        shift=shift,
        norm_g=norm_g[None, :], w_u=w_u,
        qg=jnp.tile(b_qnorm_g, B_HEADS)[None, :], kg=jnp.tile(b_knorm_g, B_KV_HEADS)[None, :],
        bd=jnp.asarray(np.kron(np.eye(B_HEADS), np.ones((HEAD_DIM, HEAD_DIM))), BF16),
        cw=a_conv_w, cb=a_conv_b[None, :],
        wri=jnp.concatenate([_block_diag(a_w_r), _block_diag(a_w_i)], axis=1).astype(BF16),
        bri=jnp.concatenate([a_b_r, a_b_i])[None, :], lam=a_lambda[None, :],
        gate_bias=gate_bias, onorm_g=c_onorm_g[None, :],
        wa=w_out[:256].astype(BF16), wb=w_out[256:768].astype(BF16), wc=w_out[768:].astype(BF16),
        pg=ple_gate.astype(BF16), pp=ple_proj.astype(BF16), png=ple_norm_g[None, :],
    )


def _pick_tile(n, cap):
    t = min(n, cap)
    while n % t:
        t //= 2
    return t


def _state_tiles(c, n, upper):
    pad_cols = jnp.zeros(c.shape[:3] + (HEAD_DIM - 1,), F32)
    rows = jnp.concatenate([n[..., None], pad_cols, c] if upper else [c, n[..., None], pad_cols], axis=-1)
    blank = jnp.zeros_like(rows)
    return jnp.concatenate([blank, rows] if upper else [rows, blank], axis=-2)


def _layer(x2d, p_all, layer, bsz, t, conv0, h0, c0, n0, m0, lw, attend):
    m = bsz * t
    tm = _pick_tile(m, 512)
    proj = _inproj(x2d, lw["norm_g"], lw["w_u"], lw["qg"], lw["kg"], lw["bd"], tm, bsz, t)
    u, k2d, v2d = proj[:3]
    if len(proj) > 3:
        k_state, v_state = [a.reshape(bsz, B_KV_HEADS, HEAD_DIM, t).transpose(0, 3, 1, 2) for a in proj[3:]]
    else:
        k_state, v_state = [a.reshape(bsz, t, B_KV_HEADS, HEAD_DIM) for a in (k2d, v2d)]
    ya, conv_new, h_last = _rglru(u, conv0, h0[:, None, :], lw["cw"], lw["cb"], lw["wri"], lw["bri"],
                                  lw["lam"], bsz, t, _pick_tile(t, 512))
    yb = attend(u, k2d, v2d)
    wide = t % LANES == 0
    L = LANES if wide else int(np.gcd(t, MLSTM_CHUNK))
    lo, hi = slice(0, HEAD_DIM), slice(HEAD_DIM, 2 * HEAD_DIM)
    if wide:
        even = _state_tiles(c0[:, 0::2], n0[:, 0::2], upper=False)
        odd = _state_tiles(c0[:, 1::2], n0[:, 1::2], upper=True)
        c0aug = jnp.stack([even, odd], axis=2).reshape(bsz, C_HEADS, LANES, LANES)
    else:
        c0aug = _state_tiles(c0, n0, upper=False)
    m0b = jnp.concatenate([jnp.broadcast_to(m0[:, :, None], (bsz, C_HEADS, LANES)),
                           jnp.zeros((bsz, SUBLANES - C_HEADS, LANES), F32)], axis=1)
    yc, caug, mout = _mlstm(u, lw["gate_bias"], lw["onorm_g"], c0aug, m0b, bsz, t, _pick_tile(t, 512), L, wide)
    if wide:
        c_new = jnp.stack([caug[:, 0, lo, lo], caug[:, 1, hi, hi], caug[:, 2, lo, lo], caug[:, 3, hi, hi]], axis=1)
        n_new = jnp.stack([caug[:, 0, lo, HEAD_DIM], caug[:, 1, hi, 0],
                           caug[:, 2, lo, HEAD_DIM], caug[:, 3, hi, 0]], axis=1)
    else:
        c_new, n_new = caug[:, :, lo, lo], caug[:, :, lo, HEAD_DIM]
    xo = _outproj(x2d, ya, yb, u, yc, p_all, layer, lw["wa"], lw["wb"], lw["wc"], lw["pg"], lw["pp"], lw["png"], tm)
    state = (k_state, v_state, conv_new, h_last[:, 0, :], c_new, n_new, mout[:, :C_HEADS, 0])
    return xo, state


def _attend_prompt(u, k2d, v2d, bsz, s, shift):
    nb = s // MOBA_BLOCK
    assert s % MOBA_BLOCK == 0 and nb <= NB_MAX
    kmean = _kmean(k2d, _pick_tile(bsz * nb, 8)).reshape(bsz, nb, B_KV_HEADS, HEAD_DIM)
    kmh = jnp.repeat(kmean, B_HEADS // B_KV_HEADS, axis=2).transpose(0, 2, 1, 3)
    kmh = jnp.pad(kmh, ((0, 0), (0, 0), (0, NB_MAX - nb), (0, 0)))
    wsel = jnp.einsum("bhjd,hg->bhjgd", kmh, jnp.eye(B_HEADS, dtype=F32)).reshape(bsz, B_HEADS * NB_MAX, 512)
    qa, ka, va = _select(u, k2d, v2d, wsel, jnp.full((1, LANES), shift, F32), bsz, s, _pick_tile(s, 512))
    mode = (shift <= SHIFT_LIMIT).astype(jnp.int32).reshape(1)
    return _attn(mode, qa, ka, va, bsz, s)


def _attend_sample(u, k2d, v2d, bsz, t, pt, ck, cv, slopes):
    n_pages = pt.shape[1]
    assert t == 8 and n_pages % PAGES_PER_STEP == 0
    nblk = n_pages * PAGE_SIZE // MOBA_BLOCK
    assert nblk == 64
    q = u[:, U_QB:U_QB + 512].reshape(bsz, t, B_KV_HEADS, 2, HEAD_DIM).transpose(0, 2, 3, 1, 4)
    qbd = jnp.einsum("bghtd,gk->bghtkd", q, jnp.eye(B_KV_HEADS, dtype=F32)).reshape(bsz, 64, 256)
    pad = ((0, 0), (0, LANES - t), (0, 0))
    knew = jnp.pad(k2d.reshape(bsz, t, 256), pad)
    vnew = jnp.pad(v2d.reshape(bsz, t, 256), pad)
    o = _sample_moba(pt, qbd, knew, vnew, slopes, ck, cv, bsz, n_pages, t)
    return o.reshape(bsz, B_HEADS, t, HEAD_DIM).transpose(0, 2, 1, 3).reshape(bsz * t, 512)


def kernel(x_prompt, x_sample, cache_k, cache_v, state_rglru_conv, state_rglru_h, state_mlstm_C, state_mlstm_n, state_mlstm_m, page_table, p_prompt, p_sample, norm_g, w_in, a_conv_w, a_conv_b, a_w_r, a_b_r, a_w_i, a_b_i, a_lambda, b_qnorm_g, b_knorm_g, c_b_i, c_b_f, c_onorm_g, w_out, ple_gate, ple_proj, ple_norm_g):
    weights = (norm_g, w_in, a_conv_w, a_conv_b, a_w_r, a_b_r, a_w_i, a_b_i, a_lambda, b_qnorm_g, b_knorm_g,
               c_b_i, c_b_f, c_onorm_g, w_out, ple_gate, ple_proj, ple_norm_g)
    depth = w_in.shape[0]
    bp, sp = x_prompt.shape[:2]
    bs, ts = x_sample.shape[:2]
    n_phys = cache_k.shape[1]
    ck = cache_k.transpose(0, 1, 3, 4, 2).reshape(depth * n_phys, 256, PAGE_SIZE)
    cv = cache_v.transpose(0, 1, 3, 4, 2).reshape(depth * n_phys, 256, PAGE_SIZE)
    slopes = jnp.asarray(np.repeat(2.0 ** -(np.arange(1, B_HEADS + 1)), ts)[:, None]
                         * np.ones((1, LANES)), F32)

    xp = x_prompt.reshape(bp * sp, D_MODEL)
    pp_all = p_prompt.reshape(depth * bp * sp, PLE_DIM)
    ps_all = p_sample.reshape(depth * bs * ts, PLE_DIM)
    xs = x_sample.reshape(bs * ts, D_MODEL)
    zeros = lambda *s: jnp.zeros(s, F32)
    sp_all, ss_all = [], []
    for l in range(depth):
        lw = _prep_layer(l, weights)
        xp, st_p = _layer(xp, pp_all, l, bp, sp,
                          zeros(bp, CONV_WIDTH - 1, A_WIDTH), zeros(bp, A_WIDTH),
                          zeros(bp, C_HEADS, HEAD_DIM, HEAD_DIM), zeros(bp, C_HEADS, HEAD_DIM), zeros(bp, C_HEADS),
                          lw, functools.partial(_attend_prompt, bsz=bp, s=sp, shift=lw["shift"]))
        attend_s = functools.partial(_attend_sample, bsz=bs, t=ts, pt=page_table + l * n_phys,
                                     ck=ck, cv=cv, slopes=slopes)
        xs, st_s = _layer(xs, ps_all, l, bs, ts,
                          state_rglru_conv[l], state_rglru_h[l], state_mlstm_C[l], state_mlstm_n[l],
                          state_mlstm_m[l], lw, attend_s)
        sp_all.append(st_p)
        ss_all.append(st_s)

    stk = lambda states, j: jnp.stack([s_[j] for s_ in states])
    return (xp.reshape(bp, sp, D_MODEL), xs.reshape(bs, ts, D_MODEL),
            stk(sp_all, 0), stk(sp_all, 1), stk(ss_all, 0), stk(ss_all, 1),
            stk(sp_all, 2), stk(ss_all, 2), stk(sp_all, 3), stk(ss_all, 3),
            stk(sp_all, 4), stk(ss_all, 4), stk(sp_all, 5), stk(ss_all, 5),
            stk(sp_all, 6), stk(ss_all, 6))
```
